```python
import math
import jax, jax.numpy as jnp
from jax import lax
import numpy as np

D_MODEL = 2048
BATCH = 4
SEQ = 8192
DEPTH = 1
DEC_BATCH = 8
DEC_SEQ = 32
PAST_LEN = 4096

CHUNK = 64
Q_BLOCK = 128
HEAD_DIM = 128
N_FOX_HEADS = D_MODEL // (2 * HEAD_DIM)
N_DIFF_HEADS = D_MODEL // (4 * HEAD_DIM)
FOX_WIDTH = N_FOX_HEADS * HEAD_DIM
DIFF_WIDTH = N_DIFF_HEADS * 2 * HEAD_DIM
ROPE_DIM = HEAD_DIM // 4
ROPE_THETA = 500000.0
FORGET_BIAS_MEAN = 2.0
N_EXPERTS = 32
TOP_K = 4
D_EXPERT = D_MODEL
SWIGLU_LIMIT = 7.0
SWIGLU_ALPHA = 1.702
MOE_BLOCK = 128
PLE_DIM = 256
NORM_EPS = 1e-6
SUBLN_EPS = 1e-5
IN_SIZES = (FOX_WIDTH, FOX_WIDTH, FOX_WIDTH, N_FOX_HEADS,
            DIFF_WIDTH, DIFF_WIDTH, DIFF_WIDTH, D_MODEL, D_MODEL)
IN_COLS = sum(IN_SIZES)

kernel_name = 'chunk_streaming_fox_diffattn_moe_step'


def rms_norm(x, gain, eps=NORM_EPS):
    xf = x.astype(jnp.float32)
    y = xf * lax.rsqrt(jnp.mean(xf * xf, axis=-1, keepdims=True) + eps)
    return (y * gain.astype(jnp.float32)).astype(x.dtype)


def partial_rope(x, pos):
    half = ROPE_DIM // 2
    inv_freq = ROPE_THETA ** (-2.0 * jnp.arange(half, dtype=jnp.float32) / ROPE_DIM)
    ang = pos.astype(jnp.float32)[:, None] * inv_freq
    cos = jnp.cos(ang)[None, :, None, None, :]
    sin = jnp.sin(ang)[None, :, None, None, :]
    xf = x.astype(jnp.float32)
    x1 = xf[..., :half]
    x2 = xf[..., half:ROPE_DIM]
    out = jnp.concatenate([x1 * cos - x2 * sin, x2 * cos + x1 * sin, xf[..., ROPE_DIM:]], axis=-1)
    return out.astype(x.dtype)


def fox_attend(q, k, v, c_q, c_k, q_pos, k_pos):
    s = jnp.einsum('bqhd,bkhd->bhqk', q, k, preferred_element_type=jnp.float32) * (HEAD_DIM ** -0.5)
    decay = jnp.swapaxes(c_q, 1, 2)[..., :, None] - jnp.swapaxes(c_k, 1, 2)[..., None, :]
    mask = k_pos[None, :] <= q_pos[:, None]
    p = jax.nn.softmax(jnp.where(mask, s + decay, -jnp.inf), axis=-1)
    return jnp.einsum('bhqk,bkhd->bqhd', p.astype(v.dtype), v)


def diff_attend(q, k, v, q_pos, k_pos, lam):
    s = jnp.einsum('bqhmd,bkhmd->bhmqk', q, k, preferred_element_type=jnp.float32) * (HEAD_DIM ** -0.5)
    mask = (k_pos // CHUNK)[None, :] <= (q_pos // CHUNK)[:, None]
    a = jax.nn.softmax(jnp.where(mask, s, -jnp.inf), axis=-1)
    w = a[:, :, 0] - lam * a[:, :, 1]
    return jnp.einsum('bhqk,bkhe->bqhe', w.astype(v.dtype), v)


def query_sweep(fn, q_args, q_pos):
    t = q_pos.shape[0]
    if t <= Q_BLOCK:
        return fn(*q_args, q_pos)
    nb = t // Q_BLOCK

    def to_blocks(a):
        return jnp.moveaxis(a.reshape(a.shape[0], nb, Q_BLOCK, *a.shape[2:]), 1, 0)

    blocks = tuple(to_blocks(a) for a in q_args) + (q_pos.reshape(nb, Q_BLOCK),)
    out = lax.map(lambda xs: fn(*xs), blocks)
    out = jnp.moveaxis(out, 0, 1)
    return out.reshape(out.shape[0], t, *out.shape[3:])


def moe_ffn(x, w_router, b_router, w_gate, b_gate, w_up, b_up, w_down, b_down):
    xt = x.reshape(-1, D_MODEL)
    n = xt.shape[0]
    logits = jnp.einsum('nd,de->ne', xt, w_router, preferred_element_type=jnp.float32) + b_router.astype(jnp.float32)
    top_val, top_idx = lax.top_k(logits, TOP_K)
    gate_w = jax.nn.softmax(top_val, axis=-1).astype(xt.dtype)
    a = n * TOP_K
    expert = top_idx.reshape(-1)
    token = jnp.repeat(jnp.arange(n, dtype=jnp.int32), TOP_K)
    weight = gate_w.reshape(-1)
    order = jnp.argsort(expert)
    se = expert[order]
    counts = jnp.bincount(expert, length=N_EXPERTS)
    padded = (counts + MOE_BLOCK - 1) // MOE_BLOCK * MOE_BLOCK
    start = jnp.cumsum(counts) - counts
    pend = jnp.cumsum(padded)
    pstart = pend - padded
    dest = pstart[se] + jnp.arange(a, dtype=jnp.int32) - start[se]
    n_blocks = -(-a // MOE_BLOCK) + N_EXPERTS
    rows = n_blocks * MOE_BLOCK
    row_tok = jnp.zeros((rows,), jnp.int32).at[dest].set(token[order])
    row_w = jnp.zeros((rows,), xt.dtype).at[dest].set(weight[order])
    block_exp = jnp.minimum(
        jnp.searchsorted(pend, jnp.arange(n_blocks, dtype=jnp.int32) * MOE_BLOCK, side='right'),
        N_EXPERTS - 1)

    def expert_block(args):
        tok, wts, e = args
        hb = xt[tok]
        g = hb @ w_gate[e] + b_gate[e]
        u = hb @ w_up[e] + b_up[e]
        g = jnp.minimum(g, SWIGLU_LIMIT)
        u = jnp.clip(u, -SWIGLU_LIMIT, SWIGLU_LIMIT)
        act = (u + 1.0) * (g * jax.nn.sigmoid(SWIGLU_ALPHA * g))
        return (act @ w_down[e] + b_down[e]) * wts[:, None]

    out = lax.map(expert_block, (row_tok.reshape(n_blocks, MOE_BLOCK),
                                 row_w.reshape(n_blocks, MOE_BLOCK), block_exp))
    y = jnp.zeros_like(xt).at[row_tok].add(out.reshape(rows, D_MODEL))
    return y.reshape(x.shape)


def split_points():
    pts, acc = [], 0
    for s in IN_SIZES[:-1]:
        acc += s
        pts.append(acc)
    return pts


def run_layer(h, pe, past, lw, lam_init):
    b, t, _ = h.shape
    offset = 0 if past is None else past[0].shape[1]
    xn = rms_norm(h, lw['norm_mix'])
    proj = jnp.einsum('btd,dc->btc', xn, lw['w_in'])
    fq, fk, fv, f_logit, dq, dk, dv, g_fox, g_diff = jnp.split(proj, split_points(), axis=-1)
    fq = fq.reshape(b, t, N_FOX_HEADS, HEAD_DIM)
    fk = fk.reshape(b, t, N_FOX_HEADS, HEAD_DIM)
    fv = fv.reshape(b, t, N_FOX_HEADS, HEAD_DIM)
    logf = jax.nn.log_sigmoid(f_logit.astype(jnp.float32) + lw['b_forget'].astype(jnp.float32))
    q_pos = offset + jnp.arange(t, dtype=jnp.int32)
    dq = partial_rope(dq.reshape(b, t, N_DIFF_HEADS, 2, HEAD_DIM), q_pos)
    dk = partial_rope(dk.reshape(b, t, N_DIFF_HEADS, 2, HEAD_DIM), q_pos)
    dv = dv.reshape(b, t, N_DIFF_HEADS, 2 * HEAD_DIM)
    new_state = (fk, fv, logf, dk.reshape(b, t, N_DIFF_HEADS, 2 * HEAD_DIM), dv)
    if past is None:
        fk_all, fv_all, logf_all, dk_all, dv_all = fk, fv, logf, dk, dv
    else:
        pk, pv, plogf, pdk, pdv = past
        fk_all = jnp.concatenate([pk, fk], axis=1)
        fv_all = jnp.concatenate([pv, fv], axis=1)
        logf_all = jnp.concatenate([plogf.astype(jnp.float32), logf], axis=1)
        dk_all = jnp.concatenate([pdk.reshape(b, -1, N_DIFF_HEADS, 2, HEAD_DIM), dk], axis=1)
        dv_all = jnp.concatenate([pdv, dv], axis=1)
    k_pos = jnp.arange(fk_all.shape[1], dtype=jnp.int32)
    c_all = jnp.cumsum(logf_all, axis=1)
    c_q = c_all[:, c_all.shape[1] - t:]
    lam = (jnp.exp(jnp.sum(lw['lambda_q1'].astype(jnp.float32) * lw['lambda_k1'].astype(jnp.float32)))
           - jnp.exp(jnp.sum(lw['lambda_q2'].astype(jnp.float32) * lw['lambda_k2'].astype(jnp.float32)))
           + lam_init)
    o_fox = query_sweep(lambda q, cq, qp: fox_attend(q, fk_all, fv_all, cq, c_all, qp, k_pos),
                        (fq, c_q), q_pos)
    o_diff = query_sweep(lambda q, qp: diff_attend(q, dk_all, dv_all, qp, k_pos, lam),
                         (dq,), q_pos)
    o_diff = rms_norm(o_diff, lw['diff_subln'], SUBLN_EPS) * (1.0 - lam_init)
    y_fox = jnp.einsum('bte,ed->btd', o_fox.reshape(b, t, FOX_WIDTH), lw['w_o_fox'])
    y_diff = jnp.einsum('bte,ed->btd', o_diff.reshape(b, t, DIFF_WIDTH), lw['w_o_diff'])
    merged = jax.nn.sigmoid(g_fox) * y_fox + jax.nn.sigmoid(g_diff) * y_diff
    h = h + jnp.einsum('btd,de->bte', merged, lw['w_out'])
    h = h + moe_ffn(rms_norm(h, lw['norm_ffn']), lw['w_router'], lw['b_router'], lw['w_gate'],
                    lw['b_gate'], lw['w_up'], lw['b_up'], lw['w_down'], lw['b_down'])
    gate = jax.nn.sigmoid(jnp.einsum('btd,de->bte', rms_norm(h, lw['norm_ple']), lw['w_ple_gate']))
    h = h + gate * jnp.einsum('btp,pd->btd', pe, lw['w_ple_proj'])
    return h, new_state


def setup_inputs(seed: int = 0) -> dict:
    key = jax.random.key(seed)
    ks = iter(jax.random.split(key, 40))

    def nrm(shape, scale=1.0):
        return scale * jax.random.normal(next(ks), shape, jnp.float32)

    d, e, f = D_MODEL, N_EXPERTS, D_EXPERT
    return {
        'x_prompt': nrm((BATCH, SEQ, d)),
        'x_sample': nrm((DEC_BATCH, DEC_SEQ, d)),
        'cache_fox_k': nrm((DEPTH, DEC_BATCH, PAST_LEN, N_FOX_HEADS, HEAD_DIM)),
        'cache_fox_v': nrm((DEPTH, DEC_BATCH, PAST_LEN, N_FOX_HEADS, HEAD_DIM)),
        'cache_fox_logf': jax.nn.log_sigmoid(FORGET_BIAS_MEAN + nrm((DEPTH, DEC_BATCH, PAST_LEN, N_FOX_HEADS))),
        'cache_diff_k': nrm((DEPTH, DEC_BATCH, PAST_LEN, N_DIFF_HEADS, 2 * HEAD_DIM)),
        'cache_diff_v': nrm((DEPTH, DEC_BATCH, PAST_LEN, N_DIFF_HEADS, 2 * HEAD_DIM)),
        'p_prompt': nrm((DEPTH, BATCH, SEQ, PLE_DIM)),
        'p_sample': nrm((DEPTH, DEC_BATCH, DEC_SEQ, PLE_DIM)),
        'norm_mix': 1.0 + nrm((DEPTH, d), 0.02),
        'w_in': nrm((DEPTH, d, IN_COLS), d ** -0.5),
        'b_forget': FORGET_BIAS_MEAN + nrm((DEPTH, N_FOX_HEADS), 0.1),
        'lambda_q1': nrm((DEPTH, HEAD_DIM), 0.1),
        'lambda_k1': nrm((DEPTH, HEAD_DIM), 0.1),
        'lambda_q2': nrm((DEPTH, HEAD_DIM), 0.1),
        'lambda_k2': nrm((DEPTH, HEAD_DIM), 0.1),
        'diff_subln': 1.0 + nrm((DEPTH, 2 * HEAD_DIM), 0.02),
        'w_o_fox': nrm((DEPTH, FOX_WIDTH, d), FOX_WIDTH ** -0.5),
        'w_o_diff': nrm((DEPTH, DIFF_WIDTH, d), DIFF_WIDTH ** -0.5),
        'w_out': nrm((DEPTH, d, d), d ** -0.5),
        'norm_ffn': 1.0 + nrm((DEPTH, d), 0.02),
        'w_router': nrm((DEPTH, d, e), d ** -0.5),
        'b_router': nrm((DEPTH, e), 0.01),
        'w_gate': nrm((DEPTH, e, d, f), d ** -0.5),
        'b_gate': nrm((DEPTH, e, f), 0.01),
        'w_up': nrm((DEPTH, e, d, f), d ** -0.5),
        'b_up': nrm((DEPTH, e, f), 0.01),
        'w_down': nrm((DEPTH, e, f, d), f ** -0.5),
        'b_down': nrm((DEPTH, e, d), 0.01),
        'norm_ple': 1.0 + nrm((DEPTH, d), 0.02),
        'w_ple_gate': nrm((DEPTH, d, d), d ** -0.5),
        'w_ple_proj': nrm((DEPTH, PLE_DIM, d), PLE_DIM ** -0.5),
        'norm_final': 1.0 + nrm((d,), 0.02),
    }


def reference(x_prompt, x_sample, cache_fox_k, cache_fox_v, cache_fox_logf, cache_diff_k, cache_diff_v,
              p_prompt, p_sample, norm_mix, w_in, b_forget, lambda_q1, lambda_k1, lambda_q2, lambda_k2,
              diff_subln, w_o_fox, w_o_diff, w_out, norm_ffn, w_router, b_router, w_gate, b_gate,
              w_up, b_up, w_down, b_down, norm_ple, w_ple_gate, w_ple_proj, norm_final):
    h_p, h_s = x_prompt, x_sample
    st_p, st_s = [], []
    for i in range(DEPTH):
        lw = {
            'norm_mix': norm_mix[i], 'w_in': w_in[i], 'b_forget': b_forget[i],
            'lambda_q1': lambda_q1[i], 'lambda_k1': lambda_k1[i],
            'lambda_q2': lambda_q2[i], 'lambda_k2': lambda_k2[i],
            'diff_subln': diff_subln[i], 'w_o_fox': w_o_fox[i], 'w_o_diff': w_o_diff[i],
            'w_out': w_out[i], 'norm_ffn': norm_ffn[i], 'w_router': w_router[i],
            'b_router': b_router[i], 'w_gate': w_gate[i], 'b_gate': b_gate[i],
            'w_up': w_up[i], 'b_up': b_up[i], 'w_down': w_down[i], 'b_down': b_down[i],
            'norm_ple': norm_ple[i], 'w_ple_gate': w_ple_gate[i], 'w_ple_proj': w_ple_proj[i],
        }
        lam_init = 0.8 - 0.6 * math.exp(-0.3 * i)
        h_p, sp = run_layer(h_p, p_prompt[i], None, lw, lam_init)
        past = (cache_fox_k[i], cache_fox_v[i], cache_fox_logf[i], cache_diff_k[i], cache_diff_v[i])
        h_s, ss = run_layer(h_s, p_sample[i], past, lw, lam_init)
        st_p.append(sp)
        st_s.append(ss)
    y_prompt = rms_norm(h_p, norm_final)
    y_sample = rms_norm(h_s, norm_final)
    fox_k_p, fox_v_p, fox_lf_p, diff_k_p, diff_v_p = [jnp.stack([s[j] for s in st_p]) for j in range(5)]
    fox_k_s, fox_v_s, fox_lf_s, diff_k_s, diff_v_s = [jnp.stack([s[j] for s in st_s]) for j in range(5)]
    return (y_prompt, y_sample, fox_k_p, fox_v_p, fox_lf_p, diff_k_p, diff_v_p,
            fox_k_s, fox_v_s, fox_lf_s, diff_k_s, diff_v_s)
```

```python
import functools
import math

import jax
import jax.numpy as jnp
from jax import lax
from jax.experimental import pallas as pl
from jax.experimental.pallas import tpu as pltpu

F32 = jnp.float32
BF16 = jnp.bfloat16

HEAD_DIM = 128
CHUNK = 64
CHUNK_SHIFT = 6
assert 1 << CHUNK_SHIFT == CHUNK
ROPE_DIM = HEAD_DIM // 4
ROPE_THETA = 500000.0
TOP_K = 4
SWIGLU_LIMIT = 7.0
SWIGLU_ALPHA = 1.702
NORM_EPS = 1e-6
SUBLN_EPS = 1e-5
LOG2E = 1.4426950408889634
NEG = -1e30
LANES = 128
VMEM_LIMIT = 56 * 1024 * 1024


def _cparams(sem):
    return pltpu.CompilerParams(dimension_semantics=sem, vmem_limit_bytes=VMEM_LIMIT)


def _tile(n, pref):
    t = min(n, pref)
    assert n % t == 0, (n, pref)
    return t


def _rmsnorm_kernel(x_ref, g_ref, o_ref, *, eps):
    x = x_ref[...]
    y = x * lax.rsqrt(jnp.mean(x * x, axis=-1, keepdims=True) + eps) * g_ref[...]
    o_ref[...] = y.astype(o_ref.dtype)


def _rmsnorm(x, gain, eps):
    n, d = x.shape
    tm = _tile(n, 1024)
    return pl.pallas_call(
        functools.partial(_rmsnorm_kernel, eps=eps),
        out_shape=jax.ShapeDtypeStruct((n, d), BF16),
        grid=(n // tm,),
        in_specs=[pl.BlockSpec((tm, d), lambda i: (i, 0)),
                  pl.BlockSpec((1, d), lambda i: (0, 0))],
        out_specs=pl.BlockSpec((tm, d), lambda i: (i, 0)),
        compiler_params=_cparams(("parallel",)),
        name="rmsnorm",
    )(x, gain.reshape(1, d))


def _rope_slab(x, c, s1, s2):
    return x * c + pltpu.roll(x, ROPE_DIM // 2, 1) * s1 + pltpu.roll(x, LANES - ROPE_DIM // 2, 1) * s2


def _proj_kernel(*refs, kind, scale):
    xn_ref, w_ref = refs[0], refs[1]
    acc = jnp.dot(xn_ref[...], w_ref[...], preferred_element_type=F32)
    if kind == "scale_bf":
        refs[2][...] = (acc * scale).astype(BF16)
    elif kind == "f32_bf":
        refs[2][...] = acc
        refs[3][...] = acc.astype(BF16)
    elif kind == "sigmoid_bf":
        refs[2][...] = jax.nn.sigmoid(acc).astype(BF16)
    elif kind == "logsig":
        z = acc + refs[2][...]
        val = jnp.minimum(z, 0.0) - jnp.log1p(jnp.exp(-jnp.abs(z)))
        refs[3][...] = val[:, :refs[3].shape[1]]
    elif kind in ("rope_bf", "rope_f32_bf"):
        c, s1, s2 = refs[2][...], refs[3][...], refs[4][...]
        for j in range(acc.shape[1] // LANES):
            sl = slice(j * LANES, (j + 1) * LANES)
            r = _rope_slab(acc[:, sl], c, s1, s2)
            if kind == "rope_bf":
                refs[5][:, sl] = (r * scale).astype(BF16)
            else:
                refs[5][:, sl] = r
                refs[6][:, sl] = r.astype(BF16)
    else:
        raise ValueError(kind)


def _proj(xn, w, kind, *, scale=1.0, tables=None, bias=None, out_cols=None):
    n, d = xn.shape
    c = w.shape[1]
    tm = _tile(n, 1024)
    tn = _tile(c, 1024)
    grid = (c // tn, n // tm)
    in_specs = [pl.BlockSpec((tm, d), lambda j, i: (i, 0)),
                pl.BlockSpec((d, tn), lambda j, i: (0, j))]
    args = [xn, w]
    blk = pl.BlockSpec((tm, tn), lambda j, i: (i, j))
    if kind == "scale_bf" or kind == "sigmoid_bf":
        out_shape = [jax.ShapeDtypeStruct((n, c), BF16)]
        out_specs = [blk]
    elif kind == "f32_bf":
        out_shape = [jax.ShapeDtypeStruct((n, c), F32), jax.ShapeDtypeStruct((n, c), BF16)]
        out_specs = [blk, blk]
    elif kind == "logsig":
        in_specs.append(pl.BlockSpec((1, tn), lambda j, i: (0, j)))
        args.append(bias)
        out_shape = [jax.ShapeDtypeStruct((n, out_cols), F32)]
        out_specs = [pl.BlockSpec((tm, out_cols), lambda j, i: (i, 0))]
    else:
        nt = tables[0].shape[0] // tm
        for t in tables:
            in_specs.append(pl.BlockSpec((tm, LANES), lambda j, i, nt=nt: (i % nt, 0)))
            args.append(t)
        if kind == "rope_bf":
            out_shape = [jax.ShapeDtypeStruct((n, c), BF16)]
            out_specs = [blk]
        else:
            out_shape = [jax.ShapeDtypeStruct((n, c), F32), jax.ShapeDtypeStruct((n, c), BF16)]
            out_specs = [blk, blk]
    return pl.pallas_call(
        functools.partial(_proj_kernel, kind=kind, scale=scale),
        out_shape=out_shape,
        grid=grid,
        in_specs=in_specs,
        out_specs=out_specs,
        compiler_params=_cparams(("parallel", "parallel")),
        name="proj_" + kind,
    )(*args)


def _cumsum_kernel(x_ref, o_ref):
    x = x_ref[...]
    r = x.shape[0]
    li = lax.broadcasted_iota(jnp.int32, (LANES, LANES), 0)
    lj = lax.broadcasted_iota(jnp.int32, (LANES, LANES), 1)
    upper = (li <= lj).astype(F32)
    within = jnp.dot(x, upper, preferred_element_type=F32, precision=lax.Precision.HIGHEST)
    tot = jnp.broadcast_to(within[:, LANES - 1:LANES], (r, LANES))
    ri = lax.broadcasted_iota(jnp.int32, (r, r), 0)
    rj = lax.broadcasted_iota(jnp.int32, (r, r), 1)
    strict = (rj < ri).astype(F32)
    off = jnp.dot(strict, tot, preferred_element_type=F32, precision=lax.Precision.HIGHEST)
    o_ref[...] = within + off


def _cumsum(x):
    g, s = x.shape
    r = s // LANES
    out = pl.pallas_call(
        _cumsum_kernel,
        out_shape=jax.ShapeDtypeStruct((g, r, LANES), F32),
        grid=(g,),
        in_specs=[pl.BlockSpec((None, r, LANES), lambda i: (i, 0, 0))],
        out_specs=pl.BlockSpec((None, r, LANES), lambda i: (i, 0, 0)),
        compiler_params=_cparams(("parallel",)),
        name="cumsum",
    )(x.reshape(g, r, LANES))
    return out.reshape(g, s)


def _online_update(s, v, m_s, l_s, acc_s):
    m_prev = m_s[...]
    m_new = jnp.maximum(m_prev, jnp.max(s, axis=-1, keepdims=True))
    alpha = jnp.exp2(m_prev - m_new)
    p = jnp.exp2(s - m_new)
    l_s[...] = alpha * l_s[...] + jnp.sum(p, axis=-1, keepdims=True)
    acc_s[...] = alpha * acc_s[...] + jnp.dot(p.astype(BF16), v, preferred_element_type=F32)
    m_s[...] = m_new


def _qk(q, k):
    return lax.dot_general(q, k, (((1,), (1,)), ((), ())), preferred_element_type=F32)


def _fox_kernel(q_ref, k_ref, v_ref, c_ref, o_ref, m_s, l_s, acc_s, *, tq, tk):
    qi = pl.program_id(2)
    q = q_ref[...]
    m_s[...] = jnp.full(m_s.shape, NEG, F32)
    l_s[...] = jnp.zeros(l_s.shape, F32)
    acc_s[...] = jnp.zeros(acc_s.shape, F32)

    def step(kj, masked):
        off = pl.multiple_of(kj * tk, tk)
        k = k_ref[pl.ds(off, tk), :]
        v = v_ref[pl.ds(off, tk), :]
        s = _qk(q, k) - c_ref[:, pl.ds(off, tk)] * LOG2E
        if masked:
            row = qi * tq + lax.broadcasted_iota(jnp.int32, (tq, tk), 0)
            col = off + lax.broadcasted_iota(jnp.int32, (tq, tk), 1)
            s = jnp.where(col <= row, s, NEG)
        _online_update(s, v, m_s, l_s, acc_s)

    nfull = qi * (tq // tk)

    def body(kj, carry):
        step(kj, False)
        return carry

    lax.fori_loop(0, nfull, body, 0)
    for d in range(tq // tk):
        step(nfull + d, True)
    o_ref[...] = (acc_s[...] / l_s[...]).astype(o_ref.dtype)


def _fox_prompt(q, k, v, c, b, t):
    n, w = q.shape
    h = w // HEAD_DIM
    tq = _tile(t, 512)
    tk = tq
    nq = t // tq
    return pl.pallas_call(
        functools.partial(_fox_kernel, tq=tq, tk=tk),
        out_shape=jax.ShapeDtypeStruct((n, w), BF16),
        grid=(b, h, nq),
        in_specs=[pl.BlockSpec((tq, HEAD_DIM), lambda bi, hi, qi: (bi * nq + qi, hi)),
                  pl.BlockSpec((t, HEAD_DIM), lambda bi, hi, qi: (bi, hi)),
                  pl.BlockSpec((t, HEAD_DIM), lambda bi, hi, qi: (bi, hi)),
                  pl.BlockSpec((None, None, 1, t), lambda bi, hi, qi: (bi, hi, 0, 0))],
        out_specs=pl.BlockSpec((tq, HEAD_DIM), lambda bi, hi, qi: (bi * nq + qi, hi)),
        scratch_shapes=[pltpu.VMEM((tq, 1), F32), pltpu.VMEM((tq, 1), F32),
                        pltpu.VMEM((tq, HEAD_DIM), F32)],
        compiler_params=_cparams(("parallel", "parallel", "arbitrary")),
        name="fox_prompt",
    )(q, k, v, c)


def _diff_finish(acc1, l1, acc2, l2, lam, gain, out_scale):
    o = acc1 / l1 - lam * (acc2 / l2)
    o = o * lax.rsqrt(jnp.mean(o * o, axis=-1, keepdims=True) + SUBLN_EPS) * gain
    return o * out_scale


def _diff_kernel(lam_ref, q_ref, k_ref, v_ref, g_ref, o_ref, m1, l1, a1, m2, l2, a2, *, tq, tk, out_scale):
    qi = pl.program_id(2)
    q1 = q_ref[:, :HEAD_DIM]
    q2 = q_ref[:, HEAD_DIM:]
    for m_s, l_s, a_s in ((m1, l1, a1), (m2, l2, a2)):
        m_s[...] = jnp.full(m_s.shape, NEG, F32)
        l_s[...] = jnp.zeros(l_s.shape, F32)
        a_s[...] = jnp.zeros(a_s.shape, F32)

    def step(kj, masked):
        off = pl.multiple_of(kj * tk, tk)
        v = v_ref[pl.ds(off, tk), :]
        if masked:
            row = qi * tq + lax.broadcasted_iota(jnp.int32, (tq, tk), 0)
            col = off + lax.broadcasted_iota(jnp.int32, (tq, tk), 1)
            keep = (col >> CHUNK_SHIFT) <= (row >> CHUNK_SHIFT)
        for q, lo, (m_s, l_s, a_s) in ((q1, 0, (m1, l1, a1)), (q2, HEAD_DIM, (m2, l2, a2))):
            k = k_ref[pl.ds(off, tk), lo:lo + HEAD_DIM]
            s = _qk(q, k)
            if masked:
                s = jnp.where(keep, s, NEG)
            _online_update(s, v, m_s, l_s, a_s)

    nfull = qi * (tq // tk)

    def body(kj, carry):
        step(kj, False)
        return carry

    lax.fori_loop(0, nfull, body, 0)
    for d in range(tq // tk):
        step(nfull + d, True)
    o = _diff_finish(a1[...], l1[...], a2[...], l2[...], lam_ref[0], g_ref[...], out_scale)
    o_ref[...] = o.astype(o_ref.dtype)


def _diff_prompt(lam, q, k, v, gain, b, t, out_scale):
    n, w = q.shape
    hw = 2 * HEAD_DIM
    h = w // hw
    tq = _tile(t, 512)
    tk = tq
    assert tq % CHUNK == 0
    nq = t // tq
    return pl.pallas_call(
        functools.partial(_diff_kernel, tq=tq, tk=tk, out_scale=out_scale),
        out_shape=jax.ShapeDtypeStruct((n, w), BF16),
        grid_spec=pltpu.PrefetchScalarGridSpec(
            num_scalar_prefetch=1,
            grid=(b, h, nq),
            in_specs=[pl.BlockSpec((tq, hw), lambda bi, hi, qi, lam: (bi * nq + qi, hi)),
                      pl.BlockSpec((t, hw), lambda bi, hi, qi, lam: (bi, hi)),
                      pl.BlockSpec((t, hw), lambda bi, hi, qi, lam: (bi, hi)),
                      pl.BlockSpec((1, hw), lambda bi, hi, qi, lam: (0, 0))],
            out_specs=pl.BlockSpec((tq, hw), lambda bi, hi, qi, lam: (bi * nq + qi, hi)),
            scratch_shapes=[pltpu.VMEM((tq, 1), F32), pltpu.VMEM((tq, 1), F32), pltpu.VMEM((tq, hw), F32),
                            pltpu.VMEM((tq, 1), F32), pltpu.VMEM((tq, 1), F32), pltpu.VMEM((tq, hw), F32)]),
        compiler_params=_cparams(("parallel", "parallel", "arbitrary")),
        name="diff_prompt",
    )(lam, q, k, v, gain)


def _fox_decode_kernel(q_ref, kc_ref, vc_ref, kn_ref, vn_ref, c_ref, o_ref, *, past, tq):
    q = q_ref[...]
    s_c = _qk(q, kc_ref[...].astype(BF16)) - c_ref[:, :past] * LOG2E
    c_new = c_ref[:, past:past + LANES]
    s_n = _qk(q, kn_ref[...]) - c_new[:, :tq] * LOG2E
    row = lax.broadcasted_iota(jnp.int32, (tq, tq), 0)
    col = lax.broadcasted_iota(jnp.int32, (tq, tq), 1)
    s_n = jnp.where(col <= row, s_n, NEG)
    m = jnp.maximum(jnp.max(s_c, axis=-1, keepdims=True), jnp.max(s_n, axis=-1, keepdims=True))
    p_c = jnp.exp2(s_c - m)
    p_n = jnp.exp2(s_n - m)
    l = jnp.sum(p_c, axis=-1, keepdims=True) + jnp.sum(p_n, axis=-1, keepdims=True)
    o = (jnp.dot(p_c.astype(BF16), vc_ref[...].astype(BF16), preferred_element_type=F32)
         + jnp.dot(p_n.astype(BF16), vn_ref[...], preferred_element_type=F32))
    o_ref[...] = (o / l).astype(o_ref.dtype)


def _fox_decode(q, kc, vc, kn, vn, c, b, t, past):
    n, w = q.shape
    h = w // HEAD_DIM
    assert past % LANES == 0 and t <= LANES
    spad = c.shape[-1]
    new = pl.BlockSpec((t, HEAD_DIM), lambda bi, hi: (bi, hi))
    old = pl.BlockSpec((past, HEAD_DIM), lambda bi, hi: (bi, hi))
    return pl.pallas_call(
        functools.partial(_fox_decode_kernel, past=past, tq=t),
        out_shape=jax.ShapeDtypeStruct((n, w), BF16),
        grid=(b, h),
        in_specs=[new, old, old, new, new,
                  pl.BlockSpec((None, None, 1, spad), lambda bi, hi: (bi, hi, 0, 0))],
        out_specs=new,
        compiler_params=_cparams(("parallel", "parallel")),
        name="fox_decode",
    )(q, kc, vc, kn, vn, c)


def _diff_decode_kernel(lam_ref, q_ref, kc_ref, vc_ref, kn_ref, vn_ref, g_ref, o_ref, *, past, tq, out_scale):
    vc = vc_ref[...].astype(BF16)
    vn = vn_ref[...]
    row = past + lax.broadcasted_iota(jnp.int32, (tq, tq), 0)
    col = past + lax.broadcasted_iota(jnp.int32, (tq, tq), 1)
    keep = (col >> CHUNK_SHIFT) <= (row >> CHUNK_SHIFT)
    accs, ls = [], []
    for lo in (0, HEAD_DIM):
        q = q_ref[:, lo:lo + HEAD_DIM]
        s_c = _qk(q, kc_ref[:, lo:lo + HEAD_DIM].astype(BF16))
        s_n = jnp.where(keep, _qk(q, kn_ref[:, lo:lo + HEAD_DIM]), NEG)
        m = jnp.maximum(jnp.max(s_c, axis=-1, keepdims=True), jnp.max(s_n, axis=-1, keepdims=True))
        p_c = jnp.exp2(s_c - m)
        p_n = jnp.exp2(s_n - m)
        ls.append(jnp.sum(p_c, axis=-1, keepdims=True) + jnp.sum(p_n, axis=-1, keepdims=True))
        accs.append(jnp.dot(p_c.astype(BF16), vc, preferred_element_type=F32)
                    + jnp.dot(p_n.astype(BF16), vn, preferred_element_type=F32))
    o = _diff_finish(accs[0], ls[0], accs[1], ls[1], lam_ref[0], g_ref[...], out_scale)
    o_ref[...] = o.astype(o_ref.dtype)


def _diff_decode(lam, q, kc, vc, kn, vn, gain, b, t, past, out_scale):
    n, w = q.shape
    hw = 2 * HEAD_DIM
    h = w // hw
    assert (past - 1) // CHUNK <= past // CHUNK
    new = pl.BlockSpec((t, hw), lambda bi, hi, lam: (bi, hi))
    old = pl.BlockSpec((past, hw), lambda bi, hi, lam: (bi, hi))
    return pl.pallas_call(
        functools.partial(_diff_decode_kernel, past=past, tq=t, out_scale=out_scale),
        out_shape=jax.ShapeDtypeStruct((n, w), BF16),
        grid_spec=pltpu.PrefetchScalarGridSpec(
            num_scalar_prefetch=1,
            grid=(b, h),
            in_specs=[new, old, old, new, new,
                      pl.BlockSpec((1, hw), lambda bi, hi, lam: (0, 0))],
            out_specs=new),
        compiler_params=_cparams(("parallel", "parallel")),
        name="diff_decode",
    )(lam, q, kc, vc, kn, vn, gain)


def _mix_kernel(of_ref, od_ref, gf_ref, gd_ref, x_ref, wof_ref, wod_ref, wout_ref, nrm_ref, wr_ref, br_ref,
                h_ref, xn_ref, ridx_ref, rw_ref, *, n_experts):
    yf = jnp.dot(of_ref[...], wof_ref[...], preferred_element_type=F32)
    yd = jnp.dot(od_ref[...], wod_ref[...], preferred_element_type=F32)
    merged = gf_ref[...].astype(F32) * yf + gd_ref[...].astype(F32) * yd
    h = x_ref[...] + jnp.dot(merged.astype(BF16), wout_ref[...], preferred_element_type=F32)
    h_ref[...] = h
    xf = h * lax.rsqrt(jnp.mean(h * h, axis=-1, keepdims=True) + NORM_EPS) * nrm_ref[...]
    xn_ref[...] = xf
    logits = jnp.dot(xf, wr_ref[...], preferred_element_type=F32,
                     precision=lax.Precision.HIGHEST) + br_ref[...]
    lane = lax.broadcasted_iota(jnp.int32, logits.shape, 1).astype(F32)
    cur = jnp.where(lane < n_experts, logits, -jnp.inf)
    vals, idxs = [], []
    for _ in range(TOP_K):
        mk = jnp.max(cur, axis=-1, keepdims=True)
        ik = jnp.min(jnp.where(cur == mk, lane, float(LANES)), axis=-1, keepdims=True)
        cur = jnp.where(lane == ik, -jnp.inf, cur)
        vals.append(mk)
        idxs.append(ik)
    es = [jnp.exp(vk - vals[0]) for vk in vals]
    den = es[0] + es[1] + es[2] + es[3]
    ridx = jnp.zeros(logits.shape, F32)
    rw = jnp.zeros(logits.shape, F32)
    for k in range(TOP_K):
        ridx = jnp.where(lane == float(k), idxs[k], ridx)
        rw = jnp.where(lane == float(k), es[k] / den, rw)
    ridx_ref[...] = ridx.astype(jnp.int32)
    rw_ref[...] = rw


def _mix(o_fox, o_diff, gates, x, w_o_fox, w_o_diff, w_out, norm_ffn, w_router, b_router):
    n, d = x.shape
    e = w_router.shape[1]
    tm = _tile(n, 256)
    wr = jnp.zeros((d, LANES), F32).at[:, :e].set(w_router)
    br = jnp.zeros((1, LANES), F32).at[0, :e].set(b_router)
    row = lambda c: pl.BlockSpec((tm, c), lambda i: (i, 0))
    full = lambda a: pl.BlockSpec(a.shape, lambda i: (0, 0))
    args = [o_fox, o_diff, gates, gates, x, w_o_fox, w_o_diff, w_out, norm_ffn.reshape(1, d), wr, br]
    in_specs = [row(o_fox.shape[1]), row(o_diff.shape[1]),
                pl.BlockSpec((tm, d), lambda i: (i, 0)), pl.BlockSpec((tm, d), lambda i: (i, 1)),
                row(d), full(w_o_fox), full(w_o_diff), full(w_out), full(args[8]), full(wr), full(br)]
    return pl.pallas_call(
        functools.partial(_mix_kernel, n_experts=e),
        out_shape=[jax.ShapeDtypeStruct((n, d), F32), jax.ShapeDtypeStruct((n, d), F32),
                   jax.ShapeDtypeStruct((n, LANES), jnp.int32), jax.ShapeDtypeStruct((n, LANES), F32)],
        grid=(n // tm,),
        in_specs=in_specs,
        out_specs=[row(d), row(d), row(LANES), row(LANES)],
        compiler_params=_cparams(("parallel",)),
        name="mix",
    )(*args)


def _moe_kernel(be_ref, nact_ref, nvalid_ref, tok_hbm, slot_hbm, x_hbm, wg_ref, bg_ref, wu_ref, bu_ref, wd_ref,
                bd_ref, out_hbm, xbuf, xb, tok_s, slot_s, acc, ostage, gsem, ssem, isem, *, tb, nft):
    blk = pl.program_id(0)
    ft = pl.program_id(1)
    nb = pl.num_programs(0)
    nact = nact_ref[0]
    active = blk < nact
    cur = lax.rem(blk, 2)
    nxt = 1 - cur
    last = nft - 1

    def idx_copies(b, s):
        return (pltpu.make_async_copy(tok_hbm.at[pl.ds(b, 1)], tok_s.at[pl.ds(s, 1)], isem.at[0]),
                pltpu.make_async_copy(slot_hbm.at[pl.ds(b, 1)], slot_s.at[pl.ds(s, 1)], isem.at[1]))

    def issue_gather(s):
        def body(r, carry):
            t = tok_s[s, r]
            pltpu.make_async_copy(x_hbm.at[pl.ds(t, 1)], xbuf.at[s, pl.ds(r, 1)], gsem.at[s]).start()
            return carry
        lax.fori_loop(0, tb, body, 0)

    def wait_gather(s):
        pltpu.make_async_copy(xbuf.at[s], xbuf.at[s], gsem.at[s]).wait()

    def issue_scatter(s, nv):
        def body(r, carry):
            d = slot_s[s, r]
            pltpu.make_async_copy(ostage.at[pl.ds(r, 1)], out_hbm.at[pl.ds(d, 1)], ssem.at[0]).start()
            return carry
        lax.fori_loop(0, nv, body, 0)

    def wait_scatter(nv):
        p = tb
        while p >= 1:
            @pl.when((nv & p) != 0)
            def _(p=p):
                pltpu.make_async_copy(ostage.at[pl.ds(0, p)], ostage.at[pl.ds(0, p)], ssem.at[0]).wait()
            p //= 2

    @pl.when((blk == 0) & (ft == 0))
    def _():
        c0, c1 = idx_copies(0, 0)
        c0.start()
        c1.start()
        c0.wait()
        c1.wait()
        issue_gather(0)

    @pl.when((ft == 0) & active)
    def _():
        wait_gather(cur)
        xb[...] = xbuf[cur].astype(BF16)

    @pl.when((ft == 0) & (blk + 1 < nact))
    def _():
        c0, c1 = idx_copies(blk + 1, nxt)
        c0.start()
        c1.start()

    @pl.when((ft == 1) & (blk + 1 < nact))
    def _():
        c0, c1 = idx_copies(blk + 1, nxt)
        c0.wait()
        c1.wait()
        issue_gather(nxt)

    @pl.when(active)
    def _():
        x = xb[...]
        g = jnp.dot(x, wg_ref[...], preferred_element_type=F32) + bg_ref[...]
        u = jnp.dot(x, wu_ref[...], preferred_element_type=F32) + bu_ref[...]
        g = jnp.minimum(g, SWIGLU_LIMIT)
        u = jnp.clip(u, -SWIGLU_LIMIT, SWIGLU_LIMIT)
        act = (u + 1.0) * (g * jax.nn.sigmoid(SWIGLU_ALPHA * g))
        part = jnp.dot(act.astype(BF16), wd_ref[...], preferred_element_type=F32)

        @pl.when(ft == 0)
        def _():
            acc[...] = part

        @pl.when(ft != 0)
        def _():
            acc[...] += part

    @pl.when((ft == last) & (blk >= 1) & (blk - 1 < nact))
    def _():
        wait_scatter(nvalid_ref[blk - 1])

    @pl.when((ft == last) & active)
    def _():
        ostage[...] = acc[...] + bd_ref[...]
        issue_scatter(cur, nvalid_ref[blk])

    @pl.when((ft == last) & (blk == nb - 1) & active)
    def _():
        wait_scatter(nvalid_ref[blk])


def _moe(block_exp, nact, nvalid, row_tok, row_slot, xn, w_gate, b_gate, w_up, b_up, w_down, b_down, n_slots):
    nb, tb = row_tok.shape
    e, d, f = w_gate.shape
    tf = _tile(f, 512)
    nft = f // tf
    assert nft >= 2

    def ftile(b, j, be, na, nv):
        return jnp.where(b < na[0], j, nft - 1)

    any_spec = pl.BlockSpec(memory_space=pl.ANY)
    in_specs = [any_spec, any_spec, any_spec,
                pl.BlockSpec((None, d, tf), lambda b, j, be, na, nv: (be[b], 0, ftile(b, j, be, na, nv))),
                pl.BlockSpec((None, 1, tf), lambda b, j, be, na, nv: (be[b], 0, ftile(b, j, be, na, nv))),
                pl.BlockSpec((None, d, tf), lambda b, j, be, na, nv: (be[b], 0, ftile(b, j, be, na, nv))),
                pl.BlockSpec((None, 1, tf), lambda b, j, be, na, nv: (be[b], 0, ftile(b, j, be, na, nv))),
                pl.BlockSpec((None, tf, d), lambda b, j, be, na, nv: (be[b], ftile(b, j, be, na, nv), 0)),
                pl.BlockSpec((None, 1, d), lambda b, j, be, na, nv: (be[b], 0, 0))]
    return pl.pallas_call(
        functools.partial(_moe_kernel, tb=tb, nft=nft),
        out_shape=jax.ShapeDtypeStruct((n_slots, d), F32),
        grid_spec=pltpu.PrefetchScalarGridSpec(
            num_scalar_prefetch=3,
            grid=(nb, nft),
            in_specs=in_specs,
            out_specs=any_spec,
            scratch_shapes=[pltpu.VMEM((2, tb, d), F32), pltpu.VMEM((tb, d), BF16),
                            pltpu.SMEM((2, tb), jnp.int32), pltpu.SMEM((2, tb), jnp.int32),
                            pltpu.VMEM((tb, d), F32), pltpu.VMEM((tb, d), F32),
                            pltpu.SemaphoreType.DMA((2,)), pltpu.SemaphoreType.DMA((1,)),
                            pltpu.SemaphoreType.DMA((2,))]),
        compiler_params=_cparams(("arbitrary", "arbitrary")),
        name="moe",
    )(block_exp, nact, nvalid, row_tok, row_slot, xn, w_gate, b_gate.reshape(e, 1, f), w_up,
      b_up.reshape(e, 1, f), w_down, b_down.reshape(e, 1, d))


def _routing(top_idx, n_experts, tb):
    n = top_idx.shape[0]
    a = n * TOP_K
    expert = top_idx.reshape(-1)
    order = jnp.argsort(expert).astype(jnp.int32)
    se = expert[order]
    counts = jnp.bincount(expert, length=n_experts).astype(jnp.int32)
    padded = (counts + tb - 1) // tb * tb
    start = jnp.cumsum(counts) - counts
    pend = jnp.cumsum(padded)
    pstart = pend - padded
    dest = pstart[se] + jnp.arange(a, dtype=jnp.int32) - start[se]
    nb = -(-a // tb) + n_experts
    rows = nb * tb
    row_tok = jnp.zeros((rows,), jnp.int32).at[dest].set(order // TOP_K)
    row_slot = jnp.zeros((rows,), jnp.int32).at[dest].set(order)
    row0 = jnp.arange(nb, dtype=jnp.int32) * tb
    block_exp = jnp.minimum(jnp.searchsorted(pend, row0, side="right"), n_experts - 1).astype(jnp.int32)
    nvalid = jnp.clip(pstart[block_exp] + counts[block_exp] - row0, 0, tb).astype(jnp.int32)
    nact = (pend[-1] // tb).reshape(1).astype(jnp.int32)
    return block_exp, nact, nvalid, row_tok.reshape(nb, tb), row_slot.reshape(nb, tb)


def _ple_kernel(h_ref, slot_ref, rw_ref, pe_ref, nple_ref, wg_ref, wp_ref, nfin_ref, hout_ref, y_ref, *, d):
    h = h_ref[...]
    rw = rw_ref[...]
    for k in range(TOP_K):
        h = h + rw[:, k:k + 1] * slot_ref[:, k * d:(k + 1) * d]
    xn = h * lax.rsqrt(jnp.mean(h * h, axis=-1, keepdims=True) + NORM_EPS) * nple_ref[...]
    gate = jax.nn.sigmoid(jnp.dot(xn.astype(BF16), wg_ref[...], preferred_element_type=F32))
    h = h + gate * jnp.dot(pe_ref[...].astype(BF16), wp_ref[...], preferred_element_type=F32)
    hout_ref[...] = h
    y_ref[...] = h * lax.rsqrt(jnp.mean(h * h, axis=-1, keepdims=True) + NORM_EPS) * nfin_ref[...]


def _ple(h, slots, slot_row0, rw, pe, norm_ple, w_ple_gate, w_ple_proj, norm_final):
    n, d = h.shape
    tm = _tile(n, 256)
    assert slot_row0 % tm == 0
    t0 = slot_row0 // tm
    row = lambda c: pl.BlockSpec((tm, c), lambda i: (i, 0))
    full = lambda a: pl.BlockSpec(a.shape, lambda i: (0, 0))
    g1 = norm_ple.reshape(1, d)
    g2 = norm_final.reshape(1, d)
    return pl.pallas_call(
        functools.partial(_ple_kernel, d=d),
        out_shape=[jax.ShapeDtypeStruct((n, d), F32), jax.ShapeDtypeStruct((n, d), F32)],
        grid=(n // tm,),
        in_specs=[row(d), pl.BlockSpec((tm, TOP_K * d), lambda i: (i + t0, 0)), row(LANES), row(pe.shape[1]),
                  full(g1), full(w_ple_gate), full(w_ple_proj), full(g2)],
        out_specs=[row(d), row(d)],
        compiler_params=_cparams(("parallel",)),
        name="ple",
    )(h, slots, rw, pe, g1, w_ple_gate, w_ple_proj, g2)


def _rope_tables(pos):
    half = ROPE_DIM // 2
    inv_freq = ROPE_THETA ** (-2.0 * jnp.arange(half, dtype=F32) / ROPE_DIM)
    ang = pos.astype(F32)[:, None] * inv_freq
    cos, sin = jnp.cos(ang), jnp.sin(ang)
    t = pos.shape[0]
    one = jnp.ones((t, LANES - ROPE_DIM), F32)
    zero = jnp.zeros((t, LANES - ROPE_DIM), F32)
    zh = jnp.zeros((t, half), F32)
    c = jnp.concatenate([cos, cos, one], axis=1)
    s1 = jnp.concatenate([zh, sin, zero], axis=1)
    s2 = jnp.concatenate([-sin, zh, zero], axis=1)
    return c, s1, s2


def _pad_cols(w, mult):
    c = w.shape[1]
    cp = -(-c // mult) * mult
    return jnp.pad(w, ((0, 0), (0, cp - c)))


def _layer_weights(w_in, b_forget, d):
    h_f = d // (2 * HEAD_DIM)
    fw = h_f * HEAD_DIM
    dw = fw
    o = 0
    seg = {}
    for name, width in (("fq", fw), ("fk", fw), ("fv", fw), ("fl", h_f), ("dq", dw), ("dk", dw), ("dv", dw),
                        ("gates", 2 * d)):
        seg[name] = w_in[:, o:o + width]
        o += width
    assert o == w_in.shape[1]
    out = {k: v.astype(BF16) for k, v in seg.items() if k != "fl"}
    out["fl"] = _pad_cols(seg["fl"], LANES).astype(BF16)
    out["fl_bias"] = _pad_cols(b_forget.reshape(1, h_f), LANES)
    return out, h_f


def _project(x, norm_mix, pw, tables, h_f):
    qscale = LOG2E * HEAD_DIM ** -0.5
    xn = _rmsnorm(x, norm_mix, NORM_EPS)
    fq, = _proj(xn, pw["fq"], "scale_bf", scale=qscale)
    fk, fk_b = _proj(xn, pw["fk"], "f32_bf")
    fv, fv_b = _proj(xn, pw["fv"], "f32_bf")
    logf, = _proj(xn, pw["fl"], "logsig", bias=pw["fl_bias"], out_cols=h_f)
    dq, = _proj(xn, pw["dq"], "rope_bf", scale=qscale, tables=tables)
    dk, dk_b = _proj(xn, pw["dk"], "rope_f32_bf", tables=tables)
    dv, dv_b = _proj(xn, pw["dv"], "f32_bf")
    gates, = _proj(xn, pw["gates"], "sigmoid_bf")
    return dict(fq=fq, fk=fk, fk_b=fk_b, fv=fv, fv_b=fv_b, logf=logf, dq=dq, dk=dk, dk_b=dk_b,
                dv=dv, dv_b=dv_b, gates=gates)


def _cum_logf(logf_bth):
    b, s, h = logf_bth.shape
    spad = -(-s // 1024) * 1024
    x = jnp.pad(jnp.swapaxes(logf_bth, 1, 2), ((0, 0), (0, 0), (0, spad - s)))
    return _cumsum(x.reshape(b * h, spad)).reshape(b, h, 1, spad)


def kernel(x_prompt, x_sample, cache_fox_k, cache_fox_v, cache_fox_logf, cache_diff_k, cache_diff_v,
           p_prompt, p_sample, norm_mix, w_in, b_forget, lambda_q1, lambda_k1, lambda_q2, lambda_k2,
           diff_subln, w_o_fox, w_o_diff, w_out, norm_ffn, w_router, b_router, w_gate, b_gate,
           w_up, b_up, w_down, b_down, norm_ple, w_ple_gate, w_ple_proj, norm_final):
    depth = w_in.shape[0]
    bp, tp, d = x_prompt.shape
    bs, ts, _ = x_sample.shape
    past = cache_fox_k.shape[2]
    n_p, n_s = bp * tp, bs * ts
    n_experts = w_router.shape[-1]
    tb = 512

    h_p = x_prompt.reshape(n_p, d)
    h_s = x_sample.reshape(n_s, d)
    tab_p = _rope_tables(jnp.arange(tp, dtype=jnp.int32))
    tab_s = tuple(jnp.tile(t, (bs, 1)) for t in _rope_tables(past + jnp.arange(ts, dtype=jnp.int32)))
    st_p, st_s = [], []
    for i in range(depth):
        lam_init = 0.8 - 0.6 * math.exp(-0.3 * i)
        lam = (jnp.exp(jnp.sum(lambda_q1[i].astype(F32) * lambda_k1[i].astype(F32)))
               - jnp.exp(jnp.sum(lambda_q2[i].astype(F32) * lambda_k2[i].astype(F32)))
               + lam_init).reshape(1).astype(F32)
        out_scale = 1.0 - lam_init
        pw, h_f = _layer_weights(w_in[i], b_forget[i], d)
        h_d = h_f // 2
        subln = diff_subln[i].reshape(1, 2 * HEAD_DIM)
        wof, wod, wo = w_o_fox[i].astype(BF16), w_o_diff[i].astype(BF16), w_out[i].astype(BF16)
        wg, wu, wd = w_gate[i].astype(BF16), w_up[i].astype(BF16), w_down[i].astype(BF16)
        wpg, wpp = w_ple_gate[i].astype(BF16), w_ple_proj[i].astype(BF16)

        pr = _project(h_p, norm_mix[i], pw, tab_p, h_f)
        c_p = _cum_logf(pr["logf"].reshape(bp, tp, h_f))
        o_fox_p = _fox_prompt(pr["fq"], pr["fk_b"], pr["fv_b"], c_p, bp, tp)
        o_diff_p = _diff_prompt(lam, pr["dq"], pr["dk_b"], pr["dv_b"], subln, bp, tp, out_scale)
        h1_p, xn_p, ridx_p, rw_p = _mix(o_fox_p, o_diff_p, pr["gates"], h_p, wof, wod, wo,
                                        norm_ffn[i], w_router[i], b_router[i])

        sr = _project(h_s, norm_mix[i], pw, tab_s, h_f)
        logf_all = jnp.concatenate([cache_fox_logf[i].astype(F32), sr["logf"].reshape(bs, ts, h_f)], axis=1)
        c_s = _cum_logf(logf_all)
        o_fox_s = _fox_decode(sr["fq"], cache_fox_k[i].reshape(bs * past, -1), cache_fox_v[i].reshape(bs * past, -1),
                              sr["fk_b"], sr["fv_b"], c_s, bs, ts, past)
        o_diff_s = _diff_decode(lam, sr["dq"], cache_diff_k[i].reshape(bs * past, -1),
                                cache_diff_v[i].reshape(bs * past, -1), sr["dk_b"], sr["dv_b"], subln,
                                bs, ts, past, out_scale)
        h1_s, xn_s, ridx_s, rw_s = _mix(o_fox_s, o_diff_s, sr["gates"], h_s, wof, wod, wo,
                                        norm_ffn[i], w_router[i], b_router[i])

        xn_all = jnp.concatenate([xn_p, xn_s], axis=0)
        top_idx = jnp.concatenate([ridx_p[:, :TOP_K], ridx_s[:, :TOP_K]], axis=0)
        block_exp, nact, nvalid, row_tok, row_slot = _routing(top_idx, n_experts, tb)
        n_slots = (n_p + n_s) * TOP_K
        slots = _moe(block_exp, nact, nvalid, row_tok, row_slot, xn_all, wg, b_gate[i], wu, b_up[i], wd,
                     b_down[i], n_slots)
        slots = slots.reshape(n_p + n_s, TOP_K * d)

        last = i == depth - 1
        nf = norm_final if last else jnp.ones((d,), F32)
        h_p, y_p = _ple(h1_p, slots, 0, rw_p, p_prompt[i].reshape(n_p, -1), norm_ple[i], wpg, wpp, nf)
        h_s, y_s = _ple(h1_s, slots, n_p, rw_s, p_sample[i].reshape(n_s, -1), norm_ple[i], wpg, wpp, nf)

        st_p.append((pr["fk"].reshape(bp, tp, h_f, HEAD_DIM), pr["fv"].reshape(bp, tp, h_f, HEAD_DIM),
                     pr["logf"].reshape(bp, tp, h_f), pr["dk"].reshape(bp, tp, h_d, 2 * HEAD_DIM),
                     pr["dv"].reshape(bp, tp, h_d, 2 * HEAD_DIM)))
        st_s.append((sr["fk"].reshape(bs, ts, h_f, HEAD_DIM), sr["fv"].reshape(bs, ts, h_f, HEAD_DIM),
                     sr["logf"].reshape(bs, ts, h_f), sr["dk"].reshape(bs, ts, h_d, 2 * HEAD_DIM),
                     sr["dv"].reshape(bs, ts, h_d, 2 * HEAD_DIM)))

    y_prompt = y_p.reshape(bp, tp, d)
    y_sample = y_s.reshape(bs, ts, d)
    outs_p = [jnp.stack([s[j] for s in st_p]) for j in range(5)]
    outs_s = [jnp.stack([s[j] for s in st_s]) for j in range(5)]
    return (y_prompt, y_sample, *outs_p, *outs_s)
```

```python
import functools
import math

import jax
import jax.numpy as jnp
from jax import lax
from jax.experimental import pallas as pl
from jax.experimental.pallas import tpu as pltpu

F32 = jnp.float32
BF16 = jnp.bfloat16

HEAD_DIM = 128
CHUNK = 64
CHUNK_SHIFT = 6
assert 1 << CHUNK_SHIFT == CHUNK
ROPE_DIM = HEAD_DIM // 4
ROPE_THETA = 500000.0
TOP_K = 4
SWIGLU_LIMIT = 7.0
SWIGLU_ALPHA = 1.702
NORM_EPS = 1e-6
SUBLN_EPS = 1e-5
LOG2E = 1.4426950408889634
NEG = -1e30
LANES = 128
VMEM_LIMIT = 56 * 1024 * 1024


def _cparams(sem):
    return pltpu.CompilerParams(dimension_semantics=sem, vmem_limit_bytes=VMEM_LIMIT)


def _tile(n, pref):
    t = min(n, pref)
    assert n % t == 0, (n, pref)
    return t


def _rmsnorm_kernel(x_ref, g_ref, o_ref, *, eps):
    x = x_ref[...]
    y = x * lax.rsqrt(jnp.mean(x * x, axis=-1, keepdims=True) + eps) * g_ref[...]
    o_ref[...] = y.astype(o_ref.dtype)


def _rmsnorm(x, gain, eps):
    n, d = x.shape
    tm = _tile(n, 1024)
    return pl.pallas_call(
        functools.partial(_rmsnorm_kernel, eps=eps),
        out_shape=jax.ShapeDtypeStruct((n, d), BF16),
        grid=(n // tm,),
        in_specs=[pl.BlockSpec((tm, d), lambda i: (i, 0)),
                  pl.BlockSpec((1, d), lambda i: (0, 0))],
        out_specs=pl.BlockSpec((tm, d), lambda i: (i, 0)),
        compiler_params=_cparams(("parallel",)),
        name="rmsnorm",
    )(x, gain.reshape(1, d))


def _rope_slab(x, c, s1, s2):
    return x * c + pltpu.roll(x, ROPE_DIM // 2, 1) * s1 + pltpu.roll(x, LANES - ROPE_DIM // 2, 1) * s2


def _proj_kernel(*refs, kind, scale):
    xn_ref, w_ref = refs[0], refs[1]
    acc = jnp.dot(xn_ref[...], w_ref[...], preferred_element_type=F32)
    if kind == "scale_bf":
        refs[2][...] = (acc * scale).astype(BF16)
    elif kind == "f32_bf":
        refs[2][...] = acc
        refs[3][...] = acc.astype(BF16)
    elif kind == "sigmoid_bf":
        refs[2][...] = jax.nn.sigmoid(acc).astype(BF16)
    elif kind == "logsig":
        z = acc + refs[2][...]
        val = jnp.minimum(z, 0.0) - jnp.log1p(jnp.exp(-jnp.abs(z)))
        refs[3][...] = val[:, :refs[3].shape[1]]
    elif kind in ("rope_bf", "rope_f32_bf"):
        c, s1, s2 = refs[2][...], refs[3][...], refs[4][...]
        for j in range(acc.shape[1] // LANES):
            sl = slice(j * LANES, (j + 1) * LANES)
            r = _rope_slab(acc[:, sl], c, s1, s2)
            if kind == "rope_bf":
                refs[5][:, sl] = (r * scale).astype(BF16)
            else:
                refs[5][:, sl] = r
                refs[6][:, sl] = r.astype(BF16)
    else:
        raise ValueError(kind)


def _proj(xn, w, kind, *, scale=1.0, tables=None, bias=None, out_cols=None):
    n, d = xn.shape
    c = w.shape[1]
    tm = _tile(n, 1024)
    tn = _tile(c, 1024)
    grid = (c // tn, n // tm)
    in_specs = [pl.BlockSpec((tm, d), lambda j, i: (i, 0)),
                pl.BlockSpec((d, tn), lambda j, i: (0, j))]
    args = [xn, w]
    blk = pl.BlockSpec((tm, tn), lambda j, i: (i, j))
    if kind == "scale_bf" or kind == "sigmoid_bf":
        out_shape = [jax.ShapeDtypeStruct((n, c), BF16)]
        out_specs = [blk]
    elif kind == "f32_bf":
        out_shape = [jax.ShapeDtypeStruct((n, c), F32), jax.ShapeDtypeStruct((n, c), BF16)]
        out_specs = [blk, blk]
    elif kind == "logsig":
        in_specs.append(pl.BlockSpec((1, tn), lambda j, i: (0, j)))
        args.append(bias)
        out_shape = [jax.ShapeDtypeStruct((n, out_cols), F32)]
        out_specs = [pl.BlockSpec((tm, out_cols), lambda j, i: (i, 0))]
    else:
        nt = tables[0].shape[0] // tm
        for t in tables:
            in_specs.append(pl.BlockSpec((tm, LANES), lambda j, i, nt=nt: (i % nt, 0)))
            args.append(t)
        if kind == "rope_bf":
            out_shape = [jax.ShapeDtypeStruct((n, c), BF16)]
            out_specs = [blk]
        else:
            out_shape = [jax.ShapeDtypeStruct((n, c), F32), jax.ShapeDtypeStruct((n, c), BF16)]
            out_specs = [blk, blk]
    return pl.pallas_call(
        functools.partial(_proj_kernel, kind=kind, scale=scale),
        out_shape=out_shape,
        grid=grid,
        in_specs=in_specs,
        out_specs=out_specs,
        compiler_params=_cparams(("parallel", "parallel")),
        name="proj_" + kind,
    )(*args)


def _cumsum_kernel(x_ref, o_ref):
    x = x_ref[...]
    r = x.shape[0]
    li = lax.broadcasted_iota(jnp.int32, (LANES, LANES), 0)
    lj = lax.broadcasted_iota(jnp.int32, (LANES, LANES), 1)
    upper = (li <= lj).astype(F32)
    within = jnp.dot(x, upper, preferred_element_type=F32, precision=lax.Precision.HIGHEST)
    tot = jnp.broadcast_to(within[:, LANES - 1:LANES], (r, LANES))
    ri = lax.broadcasted_iota(jnp.int32, (r, r), 0)
    rj = lax.broadcasted_iota(jnp.int32, (r, r), 1)
    strict = (rj < ri).astype(F32)
    off = jnp.dot(strict, tot, preferred_element_type=F32, precision=lax.Precision.HIGHEST)
    o_ref[...] = within + off


def _cumsum(x):
    g, s = x.shape
    r = s // LANES
    out = pl.pallas_call(
        _cumsum_kernel,
        out_shape=jax.ShapeDtypeStruct((g, r, LANES), F32),
        grid=(g,),
        in_specs=[pl.BlockSpec((None, r, LANES), lambda i: (i, 0, 0))],
        out_specs=pl.BlockSpec((None, r, LANES), lambda i: (i, 0, 0)),
        compiler_params=_cparams(("parallel",)),
        name="cumsum",
    )(x.reshape(g, r, LANES))
    return out.reshape(g, s)


def _qk(q, k):
    return lax.dot_general(q, k, (((1,), (1,)), ((), ())), preferred_element_type=F32)


def _lane_chunks(s):
    return [s[:, j * LANES:(j + 1) * LANES] for j in range(s.shape[1] // LANES)]


def _running_max(sj, m_s):
    smax = sj[0]
    for x in sj[1:]:
        smax = jnp.maximum(smax, x)
    m_prev = m_s[...]
    m_new = jnp.maximum(m_prev, jnp.max(smax, axis=-1, keepdims=True))
    m_s[...] = m_new
    return m_new, jnp.exp2(m_prev - m_new)


def _causal_tiles(t):
    tq = _tile(t, 1024)
    return tq, tq


def _fox_kernel(q_ref, k_ref, v_ref, ca_ref, o_ref, m_s, acc_s, s_s, *, tq, tk):
    qi = pl.program_id(2)
    lane = lax.broadcasted_iota(jnp.int32, (tq, LANES), 1)
    q_aug = jnp.concatenate([q_ref[...], jnp.where(lane < 3, 1.0, 0.0).astype(BF16)], axis=1)
    ones = jnp.ones((tk, LANES), BF16)
    m_s[...] = jnp.full(m_s.shape, NEG, F32)
    acc_s[...] = jnp.zeros(acc_s.shape, F32)

    def scores(kj):
        off = pl.multiple_of(kj * tk, tk)
        k_aug = jnp.concatenate([k_ref[pl.ds(off, tk), :], ca_ref[pl.ds(off, tk), :]], axis=1)
        return _qk(q_aug, k_aug)

    def consume(s, kj):
        off = pl.multiple_of(kj * tk, tk)
        v_aug = jnp.concatenate([v_ref[pl.ds(off, tk), :], ones], axis=1)
        sj = _lane_chunks(s)
        m_new, alpha = _running_max(sj, m_s)
        p = jnp.concatenate([jnp.exp2(x - m_new).astype(BF16) for x in sj], axis=1)
        pv = jnp.dot(p, v_aug, preferred_element_type=F32)
        acc_s[...] = jnp.concatenate([alpha, alpha], axis=1) * acc_s[...] + pv

    nfull = (qi * tq) // tk
    s_s[...] = scores(0)

    def body(kj, carry):
        s = s_s[...]
        s_next = scores(kj + 1)
        consume(s, kj)
        s_s[...] = s_next
        return carry

    lax.fori_loop(0, nfull, body, 0)
    row = qi * tq + lax.broadcasted_iota(jnp.int32, (tq, tk), 0)
    col = lax.broadcasted_iota(jnp.int32, (tq, tk), 1)
    for d in range(tq // tk):
        kj = nfull + d
        s = s_s[...] if d == 0 else scores(kj)
        consume(jnp.where(col + kj * tk <= row, s, NEG), kj)
    acc = acc_s[...]
    o_ref[...] = (acc[:, :HEAD_DIM] / acc[:, HEAD_DIM:]).astype(o_ref.dtype)


def _fox_prompt(q, k, v, ca, b, t):
    n, w = q.shape
    h = w // HEAD_DIM
    tq, tk = _causal_tiles(t)
    nq = t // tq
    return pl.pallas_call(
        functools.partial(_fox_kernel, tq=tq, tk=tk),
        out_shape=jax.ShapeDtypeStruct((n, w), BF16),
        grid=(b, h, nq),
        in_specs=[pl.BlockSpec((tq, HEAD_DIM), lambda bi, hi, qi: (bi * nq + qi, hi)),
                  pl.BlockSpec((t, HEAD_DIM), lambda bi, hi, qi: (bi, hi)),
                  pl.BlockSpec((t, HEAD_DIM), lambda bi, hi, qi: (bi, hi)),
                  pl.BlockSpec((None, None, t, LANES), lambda bi, hi, qi: (bi, hi, 0, 0))],
        out_specs=pl.BlockSpec((tq, HEAD_DIM), lambda bi, hi, qi: (bi * nq + qi, hi)),
        scratch_shapes=[pltpu.VMEM((tq, LANES), F32), pltpu.VMEM((tq, 2 * HEAD_DIM), F32),
                        pltpu.VMEM((tq, tk), F32)],
        compiler_params=_cparams(("parallel", "parallel", "arbitrary")),
        name="fox_prompt",
    )(q, k, v, ca)


def _diff_finish(acc1, l1, acc2, l2, lam, gain, out_scale):
    o = acc1 / l1 - lam * (acc2 / l2)
    o = o * lax.rsqrt(jnp.mean(o * o, axis=-1, keepdims=True) + SUBLN_EPS) * gain
    return o * out_scale


def _diff_kernel(lam_ref, q_ref, k_ref, v_ref, g_ref, o_ref, m1, l1, a1, s1, m2, l2, a2, s2, *, tq, tk, out_scale):
    qi = pl.program_id(2)
    state = ((0, m1, l1, a1, s1), (HEAD_DIM, m2, l2, a2, s2))
    for _, m_s, l_s, a_s, _ in state:
        m_s[...] = jnp.full(m_s.shape, NEG, F32)
        l_s[...] = jnp.zeros(l_s.shape, F32)
        a_s[...] = jnp.zeros(a_s.shape, F32)

    def scores(kj, lo):
        off = pl.multiple_of(kj * tk, tk)
        return _qk(q_ref[:, lo:lo + HEAD_DIM], k_ref[pl.ds(off, tk), lo:lo + HEAD_DIM])

    def consume(s, v, m_s, l_s, a_s):
        sj = _lane_chunks(s)
        m_new, alpha = _running_max(sj, m_s)
        pj = [jnp.exp2(x - m_new) for x in sj]
        psum = pj[0]
        for x in pj[1:]:
            psum = psum + x
        p = jnp.concatenate([x.astype(BF16) for x in pj], axis=1)
        l_s[...] = alpha * l_s[...] + psum
        a_s[...] = jnp.concatenate([alpha, alpha], axis=1) * a_s[...] + jnp.dot(p, v, preferred_element_type=F32)

    nfull = (qi * tq) // tk
    for lo, _, _, _, s_s in state:
        s_s[...] = scores(0, lo)

    def body(kj, carry):
        off = pl.multiple_of(kj * tk, tk)
        v = v_ref[pl.ds(off, tk), :]
        for lo, m_s, l_s, a_s, s_s in state:
            s = s_s[...]
            s_next = scores(kj + 1, lo)
            consume(s, v, m_s, l_s, a_s)
            s_s[...] = s_next
        return carry

    lax.fori_loop(0, nfull, body, 0)
    row = qi * tq + lax.broadcasted_iota(jnp.int32, (tq, tk), 0)
    col = lax.broadcasted_iota(jnp.int32, (tq, tk), 1)
    for d in range(tq // tk):
        kj = nfull + d
        off = pl.multiple_of(kj * tk, tk)
        v = v_ref[pl.ds(off, tk), :]
        keep = ((col + kj * tk) >> CHUNK_SHIFT) <= (row >> CHUNK_SHIFT)
        for lo, m_s, l_s, a_s, s_s in state:
            s = s_s[...] if d == 0 else scores(kj, lo)
            consume(jnp.where(keep, s, NEG), v, m_s, l_s, a_s)
    o = _diff_finish(a1[...], jnp.sum(l1[...], axis=-1, keepdims=True),
                     a2[...], jnp.sum(l2[...], axis=-1, keepdims=True), lam_ref[0], g_ref[...], out_scale)
    o_ref[...] = o.astype(o_ref.dtype)


def _diff_prompt(lam, q, k, v, gain, b, t, out_scale):
    n, w = q.shape
    hw = 2 * HEAD_DIM
    h = w // hw
    tq, tk = _causal_tiles(t)
    assert tq % CHUNK == 0
    nq = t // tq
    per_map = [pltpu.VMEM((tq, LANES), F32), pltpu.VMEM((tq, LANES), F32), pltpu.VMEM((tq, hw), F32),
               pltpu.VMEM((tq, tk), F32)]
    return pl.pallas_call(
        functools.partial(_diff_kernel, tq=tq, tk=tk, out_scale=out_scale),
        out_shape=jax.ShapeDtypeStruct((n, w), BF16),
        grid_spec=pltpu.PrefetchScalarGridSpec(
            num_scalar_prefetch=1,
            grid=(b, h, nq),
            in_specs=[pl.BlockSpec((tq, hw), lambda bi, hi, qi, lam: (bi * nq + qi, hi)),
                      pl.BlockSpec((t, hw), lambda bi, hi, qi, lam: (bi, hi)),
                      pl.BlockSpec((t, hw), lambda bi, hi, qi, lam: (bi, hi)),
                      pl.BlockSpec((1, hw), lambda bi, hi, qi, lam: (0, 0))],
            out_specs=pl.BlockSpec((tq, hw), lambda bi, hi, qi, lam: (bi * nq + qi, hi)),
            scratch_shapes=per_map + per_map),
        compiler_params=_cparams(("parallel", "parallel", "arbitrary")),
        name="diff_prompt",
    )(lam, q, k, v, gain)


def _fox_decode_kernel(q_ref, kc_ref, vc_ref, kn_ref, vn_ref, c_ref, o_ref, *, past, tq):
    q = q_ref[...]
    s_c = _qk(q, kc_ref[...].astype(BF16)) - c_ref[:, :past] * LOG2E
    c_new = c_ref[:, past:past + LANES]
    s_n = _qk(q, kn_ref[...]) - c_new[:, :tq] * LOG2E
    row = lax.broadcasted_iota(jnp.int32, (tq, tq), 0)
    col = lax.broadcasted_iota(jnp.int32, (tq, tq), 1)
    s_n = jnp.where(col <= row, s_n, NEG)
    m = jnp.maximum(jnp.max(s_c, axis=-1, keepdims=True), jnp.max(s_n, axis=-1, keepdims=True))
    p_c = jnp.exp2(s_c - m)
    p_n = jnp.exp2(s_n - m)
    l = jnp.sum(p_c, axis=-1, keepdims=True) + jnp.sum(p_n, axis=-1, keepdims=True)
    o = (jnp.dot(p_c.astype(BF16), vc_ref[...].astype(BF16), preferred_element_type=F32)
         + jnp.dot(p_n.astype(BF16), vn_ref[...], preferred_element_type=F32))
    o_ref[...] = (o / l).astype(o_ref.dtype)


def _fox_decode(q, kc, vc, kn, vn, c, b, t, past):
    n, w = q.shape
    h = w // HEAD_DIM
    assert past % LANES == 0 and t <= LANES
    spad = c.shape[-1]
    new = pl.BlockSpec((t, HEAD_DIM), lambda bi, hi: (bi, hi))
    old = pl.BlockSpec((past, HEAD_DIM), lambda bi, hi: (bi, hi))
    return pl.pallas_call(
        functools.partial(_fox_decode_kernel, past=past, tq=t),
        out_shape=jax.ShapeDtypeStruct((n, w), BF16),
        grid=(b, h),
        in_specs=[new, old, old, new, new,
                  pl.BlockSpec((None, None, 1, spad), lambda bi, hi: (bi, hi, 0, 0))],
        out_specs=new,
        compiler_params=_cparams(("parallel", "parallel")),
        name="fox_decode",
    )(q, kc, vc, kn, vn, c)


def _diff_decode_kernel(lam_ref, q_ref, kc_ref, vc_ref, kn_ref, vn_ref, g_ref, o_ref, *, past, tq, out_scale):
    vc = vc_ref[...].astype(BF16)
    vn = vn_ref[...]
    row = past + lax.broadcasted_iota(jnp.int32, (tq, tq), 0)
    col = past + lax.broadcasted_iota(jnp.int32, (tq, tq), 1)
    keep = (col >> CHUNK_SHIFT) <= (row >> CHUNK_SHIFT)
    accs, ls = [], []
    for lo in (0, HEAD_DIM):
        q = q_ref[:, lo:lo + HEAD_DIM]
        s_c = _qk(q, kc_ref[:, lo:lo + HEAD_DIM].astype(BF16))
        s_n = jnp.where(keep, _qk(q, kn_ref[:, lo:lo + HEAD_DIM]), NEG)
        m = jnp.maximum(jnp.max(s_c, axis=-1, keepdims=True), jnp.max(s_n, axis=-1, keepdims=True))
        p_c = jnp.exp2(s_c - m)
        p_n = jnp.exp2(s_n - m)
        ls.append(jnp.sum(p_c, axis=-1, keepdims=True) + jnp.sum(p_n, axis=-1, keepdims=True))
        accs.append(jnp.dot(p_c.astype(BF16), vc, preferred_element_type=F32)
                    + jnp.dot(p_n.astype(BF16), vn, preferred_element_type=F32))
    o = _diff_finish(accs[0], ls[0], accs[1], ls[1], lam_ref[0], g_ref[...], out_scale)
    o_ref[...] = o.astype(o_ref.dtype)


def _diff_decode(lam, q, kc, vc, kn, vn, gain, b, t, past, out_scale):
    n, w = q.shape
    hw = 2 * HEAD_DIM
    h = w // hw
    assert (past - 1) // CHUNK <= past // CHUNK
    new = pl.BlockSpec((t, hw), lambda bi, hi, lam: (bi, hi))
    old = pl.BlockSpec((past, hw), lambda bi, hi, lam: (bi, hi))
    return pl.pallas_call(
        functools.partial(_diff_decode_kernel, past=past, tq=t, out_scale=out_scale),
        out_shape=jax.ShapeDtypeStruct((n, w), BF16),
        grid_spec=pltpu.PrefetchScalarGridSpec(
            num_scalar_prefetch=1,
            grid=(b, h),
            in_specs=[new, old, old, new, new,
                      pl.BlockSpec((1, hw), lambda bi, hi, lam: (0, 0))],
            out_specs=new),
        compiler_params=_cparams(("parallel", "parallel")),
        name="diff_decode",
    )(lam, q, kc, vc, kn, vn, gain)


def _mix_kernel(of_ref, od_ref, gf_ref, gd_ref, x_ref, wof_ref, wod_ref, wout_ref, nrm_ref, wr_ref, br_ref,
                h_ref, xn_ref, ridx_ref, rw_ref, *, n_experts):
    yf = jnp.dot(of_ref[...], wof_ref[...], preferred_element_type=F32)
    yd = jnp.dot(od_ref[...], wod_ref[...], preferred_element_type=F32)
    merged = gf_ref[...].astype(F32) * yf + gd_ref[...].astype(F32) * yd
    h = x_ref[...] + jnp.dot(merged.astype(BF16), wout_ref[...], preferred_element_type=F32)
    h_ref[...] = h
    xf = h * lax.rsqrt(jnp.mean(h * h, axis=-1, keepdims=True) + NORM_EPS) * nrm_ref[...]
    xn_ref[...] = xf
    logits = jnp.dot(xf, wr_ref[...], preferred_element_type=F32,
                     precision=lax.Precision.HIGHEST) + br_ref[...]
    lane = lax.broadcasted_iota(jnp.int32, logits.shape, 1).astype(F32)
    cur = jnp.where(lane < n_experts, logits, -jnp.inf)
    vals, idxs = [], []
    for _ in range(TOP_K):
        mk = jnp.max(cur, axis=-1, keepdims=True)
        ik = jnp.min(jnp.where(cur == mk, lane, float(LANES)), axis=-1, keepdims=True)
        cur = jnp.where(lane == ik, -jnp.inf, cur)
        vals.append(mk)
        idxs.append(ik)
    es = [jnp.exp(vk - vals[0]) for vk in vals]
    den = es[0] + es[1] + es[2] + es[3]
    ridx = jnp.zeros(logits.shape, F32)
    rw = jnp.zeros(logits.shape, F32)
    for k in range(TOP_K):
        ridx = jnp.where(lane == float(k), idxs[k], ridx)
        rw = jnp.where(lane == float(k), es[k] / den, rw)
    ridx_ref[...] = ridx.astype(jnp.int32)
    rw_ref[...] = rw


def _mix(o_fox, o_diff, gates, x, w_o_fox, w_o_diff, w_out, norm_ffn, w_router, b_router):
    n, d = x.shape
    e = w_router.shape[1]
    tm = _tile(n, 256)
    wr = jnp.zeros((d, LANES), F32).at[:, :e].set(w_router)
    br = jnp.zeros((1, LANES), F32).at[0, :e].set(b_router)
    row = lambda c: pl.BlockSpec((tm, c), lambda i: (i, 0))
    full = lambda a: pl.BlockSpec(a.shape, lambda i: (0, 0))
    args = [o_fox, o_diff, gates, gates, x, w_o_fox, w_o_diff, w_out, norm_ffn.reshape(1, d), wr, br]
    in_specs = [row(o_fox.shape[1]), row(o_diff.shape[1]),
                pl.BlockSpec((tm, d), lambda i: (i, 0)), pl.BlockSpec((tm, d), lambda i: (i, 1)),
                row(d), full(w_o_fox), full(w_o_diff), full(w_out), full(args[8]), full(wr), full(br)]
    return pl.pallas_call(
        functools.partial(_mix_kernel, n_experts=e),
        out_shape=[jax.ShapeDtypeStruct((n, d), F32), jax.ShapeDtypeStruct((n, d), F32),
                   jax.ShapeDtypeStruct((n, LANES), jnp.int32), jax.ShapeDtypeStruct((n, LANES), F32)],
        grid=(n // tm,),
        in_specs=in_specs,
        out_specs=[row(d), row(d), row(LANES), row(LANES)],
        compiler_params=_cparams(("parallel",)),
        name="mix",
    )(*args)


def _moe_kernel(be_ref, nact_ref, tok_hbm, slot_hbm, x_hbm, wg_ref, bg_ref, wu_ref, bu_ref, wd_ref,
                bd_ref, out_hbm, xbuf, xb, tok_s, slot_s, ostage, gsem, ssem, isem, *, tb, nft, n_real):
    blk = pl.program_id(0)
    ft = pl.program_id(1)
    nact = nact_ref[0]
    active = blk < nact
    cur = lax.rem(blk, 2)
    nxt = 1 - cur
    last = nft - 1
    per_step = tb // nft

    def tok_copy(b, s):
        return pltpu.make_async_copy(tok_hbm.at[pl.ds(b, 1)], tok_s.at[pl.ds(s, 1)], isem.at[0])

    def slot_copy(b, s):
        return pltpu.make_async_copy(slot_hbm.at[pl.ds(b, 1)], slot_s.at[pl.ds(s, 1)], isem.at[1])

    def gather_row(s, r):
        t = tok_s[s, r]
        pltpu.make_async_copy(x_hbm.at[pl.ds(t, 1)], xbuf.at[s, pl.ds(r, 1)], gsem.at[s]).start()

    def scatter_row(s, r):
        d = slot_s[s, r]
        pltpu.make_async_copy(ostage.at[s, pl.ds(r, 1)], out_hbm.at[pl.ds(d, 1)], ssem.at[s]).start()

    def wait_gather(s):
        pltpu.make_async_copy(xbuf.at[s], xbuf.at[s], gsem.at[s]).wait()

    def wait_scatter(s):
        pltpu.make_async_copy(ostage.at[s], ostage.at[s], ssem.at[s]).wait()

    def loop_rows(fn, s):
        def body(r, carry):
            fn(s, r)
            return carry
        lax.fori_loop(0, tb, body, 0)

    @pl.when((blk == 0) & (ft == 0))
    def _():
        ostage[0] = jnp.zeros((tb, ostage.shape[2]), F32)
        fill = pltpu.make_async_copy(ostage.at[0], out_hbm.at[pl.ds(n_real, tb)], ssem.at[0])
        fill.start()
        fill.wait()
        for b in range(2):
            c = tok_copy(b, b)
            c.start()
            c.wait()
        loop_rows(gather_row, 0)

    @pl.when((ft == 0) & (blk >= 1) & (blk <= nact))
    def _():
        tok_copy(blk + 1, nxt).wait()
        slot_copy(blk - 1, nxt).wait()

    @pl.when((ft == 0) & (blk <= nact))
    def _():
        wait_gather(cur)

    @pl.when((ft == 0) & (blk >= 2) & (blk - 2 < nact))
    def _():
        wait_scatter(cur)

    @pl.when((ft == 0) & active)
    def _():
        tok_copy(blk + 2, cur).start()
        slot_copy(blk, cur).start()
        xb[...] = xbuf[cur].astype(BF16)

    @pl.when((ft == 0) & (blk == nact) & (blk >= 1))
    def _():
        loop_rows(scatter_row, nxt)

    def compute(with_scatter):
        base = ft * per_step
        for j in range(per_step):
            gather_row(nxt, base + j)
        if with_scatter:
            for j in range(per_step):
                scatter_row(nxt, base + j)
        x = xb[...]
        g = jnp.dot(x, wg_ref[...], preferred_element_type=F32) + bg_ref[...]
        u = jnp.dot(x, wu_ref[...], preferred_element_type=F32) + bu_ref[...]
        g = jnp.minimum(g, SWIGLU_LIMIT)
        u = jnp.clip(u, -SWIGLU_LIMIT, SWIGLU_LIMIT)
        act = (u + 1.0) * (g * jax.nn.sigmoid(SWIGLU_ALPHA * g))
        part = jnp.dot(act.astype(BF16), wd_ref[...], preferred_element_type=F32)

        @pl.when(ft == 0)
        def _():
            ostage[cur] = part

        @pl.when((ft != 0) & (ft != last))
        def _():
            ostage[cur] += part

        @pl.when(ft == last)
        def _():
            ostage[cur] += part + bd_ref[...]

    @pl.when(active & (blk == 0))
    def _():
        compute(False)

    @pl.when(active & (blk > 0))
    def _():
        compute(True)


def _moe(block_exp, nact, row_tok, row_slot, xn, w_gate, b_gate, w_up, b_up, w_down, b_down, n_real):
    nb, tb = row_tok.shape
    e, d, f = w_gate.shape
    tf = _tile(f, 512)
    nft = f // tf
    assert nft >= 2

    assert tb % nft == 0

    def ftile(b, j, be, na):
        return jnp.where(b < na[0], j, nft - 1)

    any_spec = pl.BlockSpec(memory_space=pl.ANY)
    in_specs = [any_spec, any_spec, any_spec,
                pl.BlockSpec((None, d, tf), lambda b, j, be, na: (be[b], 0, ftile(b, j, be, na))),
                pl.BlockSpec((None, 1, tf), lambda b, j, be, na: (be[b], 0, ftile(b, j, be, na))),
                pl.BlockSpec((None, d, tf), lambda b, j, be, na: (be[b], 0, ftile(b, j, be, na))),
                pl.BlockSpec((None, 1, tf), lambda b, j, be, na: (be[b], 0, ftile(b, j, be, na))),
                pl.BlockSpec((None, tf, d), lambda b, j, be, na: (be[b], ftile(b, j, be, na), 0)),
                pl.BlockSpec((None, 1, d), lambda b, j, be, na: (be[b], 0, 0))]
    return pl.pallas_call(
        functools.partial(_moe_kernel, tb=tb, nft=nft, n_real=n_real),
        out_shape=jax.ShapeDtypeStruct((n_real + tb, d), F32),
        grid_spec=pltpu.PrefetchScalarGridSpec(
            num_scalar_prefetch=2,
            grid=(nb, nft),
            in_specs=in_specs,
            out_specs=any_spec,
            scratch_shapes=[pltpu.VMEM((2, tb, d), F32), pltpu.VMEM((tb, d), BF16),
                            pltpu.SMEM((2, tb), jnp.int32), pltpu.SMEM((2, tb), jnp.int32),
                            pltpu.VMEM((2, tb, d), F32),
                            pltpu.SemaphoreType.DMA((2,)), pltpu.SemaphoreType.DMA((2,)),
                            pltpu.SemaphoreType.DMA((2,))]),
        compiler_params=_cparams(("arbitrary", "arbitrary")),
        name="moe",
    )(block_exp, nact, row_tok, row_slot, xn, w_gate, b_gate.reshape(e, 1, f), w_up,
      b_up.reshape(e, 1, f), w_down, b_down.reshape(e, 1, d))


def _routing(top_idx, n_experts, tb):
    n = top_idx.shape[0]
    a = n * TOP_K
    expert = top_idx.reshape(-1)
    order = jnp.argsort(expert).astype(jnp.int32)
    counts = jnp.sum(expert[:, None] == jnp.arange(n_experts, dtype=jnp.int32)[None, :], axis=0, dtype=jnp.int32)
    padded = (counts + tb - 1) // tb * tb
    start = jnp.cumsum(counts) - counts
    pend = jnp.cumsum(padded)
    pstart = pend - padded
    nb = -(-a // tb) + n_experts + 1
    row0 = jnp.arange(nb, dtype=jnp.int32) * tb
    block_exp = jnp.minimum(jnp.sum(pend[None, :] <= row0[:, None], axis=1), n_experts - 1).astype(jnp.int32)
    nvalid = jnp.clip(pstart[block_exp] + counts[block_exp] - row0, 0, tb)
    src0 = jnp.clip(start[block_exp] + row0 - pstart[block_exp], 0, a)
    order_pad = jnp.concatenate([order, jnp.zeros((tb,), jnp.int32)])
    win = jax.vmap(lambda s: lax.dynamic_slice(order_pad, (s,), (tb,)))(src0)
    r = jnp.arange(tb, dtype=jnp.int32)[None, :]
    valid = r < nvalid[:, None]
    tok = win // TOP_K
    row_tok = jnp.where(valid, tok, 0)
    row_slot = jnp.where(valid, (win % TOP_K) * n + tok, a + r)
    nact = (pend[-1] // tb).reshape(1).astype(jnp.int32)
    return block_exp, nact, row_tok.astype(jnp.int32), row_slot.astype(jnp.int32)


def _ple_kernel(h_ref, s0_ref, s1_ref, s2_ref, s3_ref, rw_ref, pe_ref, nple_ref, wg_ref, wp_ref, nfin_ref,
                hout_ref, y_ref):
    h = h_ref[...]
    rw = rw_ref[...]
    for k, s_ref in enumerate((s0_ref, s1_ref, s2_ref, s3_ref)):
        h = h + rw[:, k:k + 1] * s_ref[...]
    xn = h * lax.rsqrt(jnp.mean(h * h, axis=-1, keepdims=True) + NORM_EPS) * nple_ref[...]
    gate = jax.nn.sigmoid(jnp.dot(xn.astype(BF16), wg_ref[...], preferred_element_type=F32))
    h = h + gate * jnp.dot(pe_ref[...].astype(BF16), wp_ref[...], preferred_element_type=F32)
    hout_ref[...] = h
    y_ref[...] = h * lax.rsqrt(jnp.mean(h * h, axis=-1, keepdims=True) + NORM_EPS) * nfin_ref[...]


def _ple(h, slots, n_all, tok0, rw, pe, norm_ple, w_ple_gate, w_ple_proj, norm_final):
    n, d = h.shape
    tm = _tile(n, 256)
    assert TOP_K == 4 and n_all % tm == 0 and tok0 % tm == 0
    row = lambda c: pl.BlockSpec((tm, c), lambda i: (i, 0))
    full = lambda a: pl.BlockSpec(a.shape, lambda i: (0, 0))
    slot = lambda k: pl.BlockSpec((tm, d), lambda i, t0=(k * n_all + tok0) // tm: (t0 + i, 0))
    g1 = norm_ple.reshape(1, d)
    g2 = norm_final.reshape(1, d)
    return pl.pallas_call(
        _ple_kernel,
        out_shape=[jax.ShapeDtypeStruct((n, d), F32), jax.ShapeDtypeStruct((n, d), F32)],
        grid=(n // tm,),
        in_specs=[row(d), slot(0), slot(1), slot(2), slot(3), row(LANES), row(pe.shape[1]),
                  full(g1), full(w_ple_gate), full(w_ple_proj), full(g2)],
        out_specs=[row(d), row(d)],
        compiler_params=_cparams(("parallel",)),
        name="ple",
    )(h, slots, slots, slots, slots, rw, pe, g1, w_ple_gate, w_ple_proj, g2)


def _rope_tables(pos):
    half = ROPE_DIM // 2
    inv_freq = ROPE_THETA ** (-2.0 * jnp.arange(half, dtype=F32) / ROPE_DIM)
    ang = pos.astype(F32)[:, None] * inv_freq
    cos, sin = jnp.cos(ang), jnp.sin(ang)
    t = pos.shape[0]
    one = jnp.ones((t, LANES - ROPE_DIM), F32)
    zero = jnp.zeros((t, LANES - ROPE_DIM), F32)
    zh = jnp.zeros((t, half), F32)
    c = jnp.concatenate([cos, cos, one], axis=1)
    s1 = jnp.concatenate([zh, sin, zero], axis=1)
    s2 = jnp.concatenate([-sin, zh, zero], axis=1)
    return c, s1, s2


def _pad_cols(w, mult):
    c = w.shape[1]
    cp = -(-c // mult) * mult
    return jnp.pad(w, ((0, 0), (0, cp - c)))


def _layer_weights(w_in, b_forget, d):
    h_f = d // (2 * HEAD_DIM)
    fw = h_f * HEAD_DIM
    dw = fw
    o = 0
    seg = {}
    for name, width in (("fq", fw), ("fk", fw), ("fv", fw), ("fl", h_f), ("dq", dw), ("dk", dw), ("dv", dw),
                        ("gates", 2 * d)):
        seg[name] = w_in[:, o:o + width]
        o += width
    assert o == w_in.shape[1]
    out = {k: v.astype(BF16) for k, v in seg.items() if k != "fl"}
    out["fl"] = _pad_cols(seg["fl"], LANES).astype(BF16)
    out["fl_bias"] = _pad_cols(b_forget.reshape(1, h_f), LANES)
    return out, h_f


def _project(x, norm_mix, pw, tables, h_f):
    qscale = LOG2E * HEAD_DIM ** -0.5
    xn = _rmsnorm(x, norm_mix, NORM_EPS)
    fq, = _proj(xn, pw["fq"], "scale_bf", scale=qscale)
    fk, fk_b = _proj(xn, pw["fk"], "f32_bf")
    fv, fv_b = _proj(xn, pw["fv"], "f32_bf")
    logf, = _proj(xn, pw["fl"], "logsig", bias=pw["fl_bias"], out_cols=h_f)
    dq, = _proj(xn, pw["dq"], "rope_bf", scale=qscale, tables=tables)
    dk, dk_b = _proj(xn, pw["dk"], "rope_f32_bf", tables=tables)
    dv, dv_b = _proj(xn, pw["dv"], "f32_bf")
    gates, = _proj(xn, pw["gates"], "sigmoid_bf")
    return dict(fq=fq, fk=fk, fk_b=fk_b, fv=fv, fv_b=fv_b, logf=logf, dq=dq, dk=dk, dk_b=dk_b,
                dv=dv, dv_b=dv_b, gates=gates)


def _cum_logf(logf_bth):
    b, s, h = logf_bth.shape
    spad = -(-s // 1024) * 1024
    x = jnp.pad(jnp.swapaxes(logf_bth, 1, 2), ((0, 0), (0, 0), (0, spad - s)))
    return _cumsum(x.reshape(b * h, spad)).reshape(b, h, 1, spad)


def _decay_columns(c):
    def top_bits(v):
        bits = lax.bitcast_convert_type(v, jnp.uint32) & jnp.uint32(0xFFFF0000)
        return lax.bitcast_convert_type(bits, F32)

    x = -LOG2E * c
    hi = top_bits(x)
    mid = top_bits(x - hi)
    lo = top_bits(x - hi - mid)
    parts = jnp.stack([hi, mid, lo], axis=-1).astype(BF16)
    return jnp.pad(parts, ((0, 0), (0, 0), (0, 0), (0, LANES - 3)))


def kernel(x_prompt, x_sample, cache_fox_k, cache_fox_v, cache_fox_logf, cache_diff_k, cache_diff_v,
           p_prompt, p_sample, norm_mix, w_in, b_forget, lambda_q1, lambda_k1, lambda_q2, lambda_k2,
           diff_subln, w_o_fox, w_o_diff, w_out, norm_ffn, w_router, b_router, w_gate, b_gate,
           w_up, b_up, w_down, b_down, norm_ple, w_ple_gate, w_ple_proj, norm_final):
    depth = w_in.shape[0]
    bp, tp, d = x_prompt.shape
    bs, ts, _ = x_sample.shape
    past = cache_fox_k.shape[2]
    n_p, n_s = bp * tp, bs * ts
    n_experts = w_router.shape[-1]
    tb = 512

    h_p = x_prompt.reshape(n_p, d)
    h_s = x_sample.reshape(n_s, d)
    tab_p = _rope_tables(jnp.arange(tp, dtype=jnp.int32))
    tab_s = tuple(jnp.tile(t, (bs, 1)) for t in _rope_tables(past + jnp.arange(ts, dtype=jnp.int32)))
    st_p, st_s = [], []
    for i in range(depth):
        lam_init = 0.8 - 0.6 * math.exp(-0.3 * i)
        lam = (jnp.exp(jnp.sum(lambda_q1[i].astype(F32) * lambda_k1[i].astype(F32)))
               - jnp.exp(jnp.sum(lambda_q2[i].astype(F32) * lambda_k2[i].astype(F32)))
               + lam_init).reshape(1).astype(F32)
        out_scale = 1.0 - lam_init
        pw, h_f = _layer_weights(w_in[i], b_forget[i], d)
        h_d = h_f // 2
        subln = diff_subln[i].reshape(1, 2 * HEAD_DIM)
        wof, wod, wo = w_o_fox[i].astype(BF16), w_o_diff[i].astype(BF16), w_out[i].astype(BF16)
        wg, wu, wd = w_gate[i].astype(BF16), w_up[i].astype(BF16), w_down[i].astype(BF16)
        wpg, wpp = w_ple_gate[i].astype(BF16), w_ple_proj[i].astype(BF16)

        pr = _project(h_p, norm_mix[i], pw, tab_p, h_f)
        c_p = _cum_logf(pr["logf"].reshape(bp, tp, h_f))
        o_fox_p = _fox_prompt(pr["fq"], pr["fk_b"], pr["fv_b"], _decay_columns(c_p[:, :, 0, :tp]), bp, tp)
        o_diff_p = _diff_prompt(lam, pr["dq"], pr["dk_b"], pr["dv_b"], subln, bp, tp, out_scale)
        h1_p, xn_p, ridx_p, rw_p = _mix(o_fox_p, o_diff_p, pr["gates"], h_p, wof, wod, wo,
                                        norm_ffn[i], w_router[i], b_router[i])

        sr = _project(h_s, norm_mix[i], pw, tab_s, h_f)
        logf_all = jnp.concatenate([cache_fox_logf[i].astype(F32), sr["logf"].reshape(bs, ts, h_f)], axis=1)
        c_s = _cum_logf(logf_all)
        o_fox_s = _fox_decode(sr["fq"], cache_fox_k[i].reshape(bs * past, -1), cache_fox_v[i].reshape(bs * past, -1),
                              sr["fk_b"], sr["fv_b"], c_s, bs, ts, past)
        o_diff_s = _diff_decode(lam, sr["dq"], cache_diff_k[i].reshape(bs * past, -1),
                                cache_diff_v[i].reshape(bs * past, -1), sr["dk_b"], sr["dv_b"], subln,
                                bs, ts, past, out_scale)
        h1_s, xn_s, ridx_s, rw_s = _mix(o_fox_s, o_diff_s, sr["gates"], h_s, wof, wod, wo,
                                        norm_ffn[i], w_router[i], b_router[i])

        xn_all = jnp.concatenate([xn_p, xn_s], axis=0)
        top_idx = jnp.concatenate([ridx_p[:, :TOP_K], ridx_s[:, :TOP_K]], axis=0)
        block_exp, nact, row_tok, row_slot = _routing(top_idx, n_experts, tb)
        n_all = n_p + n_s
        slots = _moe(block_exp, nact, row_tok, row_slot, xn_all, wg, b_gate[i], wu, b_up[i], wd,
                     b_down[i], n_all * TOP_K)

        last = i == depth - 1
        nf = norm_final if last else jnp.ones((d,), F32)
        h_p, y_p = _ple(h1_p, slots, n_all, 0, rw_p, p_prompt[i].reshape(n_p, -1), norm_ple[i], wpg, wpp, nf)
        h_s, y_s = _ple(h1_s, slots, n_all, n_p, rw_s, p_sample[i].reshape(n_s, -1), norm_ple[i], wpg, wpp, nf)

        st_p.append((pr["fk"].reshape(bp, tp, h_f, HEAD_DIM), pr["fv"].reshape(bp, tp, h_f, HEAD_DIM),
                     pr["logf"].reshape(bp, tp, h_f), pr["dk"].reshape(bp, tp, h_d, 2 * HEAD_DIM),
                     pr["dv"].reshape(bp, tp, h_d, 2 * HEAD_DIM)))
        st_s.append((sr["fk"].reshape(bs, ts, h_f, HEAD_DIM), sr["fv"].reshape(bs, ts, h_f, HEAD_DIM),
                     sr["logf"].reshape(bs, ts, h_f), sr["dk"].reshape(bs, ts, h_d, 2 * HEAD_DIM),
                     sr["dv"].reshape(bs, ts, h_d, 2 * HEAD_DIM)))

    y_prompt = y_p.reshape(bp, tp, d)
    y_sample = y_s.reshape(bs, ts, d)
    outs_p = [jnp.stack([s[j] for s in st_p]) for j in range(5)]
    outs_s = [jnp.stack([s[j] for s in st_s]) for j in range(5)]
    return (y_prompt, y_sample, *outs_p, *outs_s)
```

```python
import functools
import math

import jax
import jax.numpy as jnp
from jax import lax
from jax.experimental import pallas as pl
from jax.experimental.pallas import tpu as pltpu

F32 = jnp.float32
BF16 = jnp.bfloat16

HEAD_DIM = 128
CHUNK = 64
CHUNK_SHIFT = 6
assert 1 << CHUNK_SHIFT == CHUNK
ROPE_DIM = HEAD_DIM // 4
ROPE_THETA = 500000.0
TOP_K = 4
SWIGLU_LIMIT = 7.0
SWIGLU_ALPHA = 1.702
NORM_EPS = 1e-6
SUBLN_EPS = 1e-5
LOG2E = 1.4426950408889634
NEG = -1e30
LANES = 128
VMEM_LIMIT = 56 * 1024 * 1024


def _cparams(sem):
    return pltpu.CompilerParams(dimension_semantics=sem, vmem_limit_bytes=VMEM_LIMIT)


def _top_bits(v):
    bits = lax.bitcast_convert_type(v, jnp.uint32) & jnp.uint32(0xFFFF0000)
    return lax.bitcast_convert_type(bits, F32)


def _tile(n, pref):
    t = min(n, pref)
    assert n % t == 0, (n, pref)
    return t


def _rmsnorm_kernel(x_ref, g_ref, o_ref, *, eps):
    x = x_ref[...]
    y = x * lax.rsqrt(jnp.mean(x * x, axis=-1, keepdims=True) + eps) * g_ref[...]
    o_ref[...] = y.astype(o_ref.dtype)


def _rmsnorm(x, gain, eps):
    n, d = x.shape
    tm = _tile(n, 1024)
    return pl.pallas_call(
        functools.partial(_rmsnorm_kernel, eps=eps),
        out_shape=jax.ShapeDtypeStruct((n, d), BF16),
        grid=(n // tm,),
        in_specs=[pl.BlockSpec((tm, d), lambda i: (i, 0)),
                  pl.BlockSpec((1, d), lambda i: (0, 0))],
        out_specs=pl.BlockSpec((tm, d), lambda i: (i, 0)),
        compiler_params=_cparams(("parallel",)),
        name="rmsnorm",
    )(x, gain.reshape(1, d))


def _rope_slab(x, c, s1, s2):
    return x * c + pltpu.roll(x, ROPE_DIM // 2, 1) * s1 + pltpu.roll(x, LANES - ROPE_DIM // 2, 1) * s2


def _proj_kernel(*refs, kind, scale):
    xn_ref, w_ref = refs[0], refs[1]
    acc = jnp.dot(xn_ref[...], w_ref[...], preferred_element_type=F32)
    if kind == "scale_bf":
        refs[2][...] = (acc * scale).astype(BF16)
    elif kind == "f32_bf":
        refs[2][...] = acc
        refs[3][...] = acc.astype(BF16)
    elif kind == "sigmoid_bf":
        refs[2][...] = jax.nn.sigmoid(acc).astype(BF16)
    elif kind == "logsig":
        z = acc + refs[2][...]
        val = jnp.minimum(z, 0.0) - jnp.log1p(jnp.exp(-jnp.abs(z)))
        refs[3][...] = val[:, :refs[3].shape[1]]
    elif kind in ("rope_bf", "rope_f32_bf"):
        c, s1, s2 = refs[2][...], refs[3][...], refs[4][...]
        for j in range(acc.shape[1] // LANES):
            sl = slice(j * LANES, (j + 1) * LANES)
            r = _rope_slab(acc[:, sl], c, s1, s2)
            if kind == "rope_bf":
                refs[5][:, sl] = (r * scale).astype(BF16)
            else:
                refs[5][:, sl] = r
                refs[6][:, sl] = r.astype(BF16)
    else:
        raise ValueError(kind)


def _proj(xn, w, kind, *, scale=1.0, tables=None, bias=None, out_cols=None):
    n, d = xn.shape
    c = w.shape[1]
    tm = _tile(n, 1024)
    tn = _tile(c, 1024)
    grid = (c // tn, n // tm)
    in_specs = [pl.BlockSpec((tm, d), lambda j, i: (i, 0)),
                pl.BlockSpec((d, tn), lambda j, i: (0, j))]
    args = [xn, w]
    blk = pl.BlockSpec((tm, tn), lambda j, i: (i, j))
    if kind == "scale_bf" or kind == "sigmoid_bf":
        out_shape = [jax.ShapeDtypeStruct((n, c), BF16)]
        out_specs = [blk]
    elif kind == "f32_bf":
        out_shape = [jax.ShapeDtypeStruct((n, c), F32), jax.ShapeDtypeStruct((n, c), BF16)]
        out_specs = [blk, blk]
    elif kind == "logsig":
        in_specs.append(pl.BlockSpec((1, tn), lambda j, i: (0, j)))
        args.append(bias)
        out_shape = [jax.ShapeDtypeStruct((n, out_cols), F32)]
        out_specs = [pl.BlockSpec((tm, out_cols), lambda j, i: (i, 0))]
    else:
        nt = tables[0].shape[0] // tm
        for t in tables:
            in_specs.append(pl.BlockSpec((tm, LANES), lambda j, i, nt=nt: (i % nt, 0)))
            args.append(t)
        if kind == "rope_bf":
            out_shape = [jax.ShapeDtypeStruct((n, c), BF16)]
            out_specs = [blk]
        else:
            out_shape = [jax.ShapeDtypeStruct((n, c), F32), jax.ShapeDtypeStruct((n, c), BF16)]
            out_specs = [blk, blk]
    return pl.pallas_call(
        functools.partial(_proj_kernel, kind=kind, scale=scale),
        out_shape=out_shape,
        grid=grid,
        in_specs=in_specs,
        out_specs=out_specs,
        compiler_params=_cparams(("parallel", "parallel")),
        name="proj_" + kind,
    )(*args)


def _cumsum_kernel(x_ref, o_ref):
    x = x_ref[...]
    r = x.shape[0]
    li = lax.broadcasted_iota(jnp.int32, (LANES, LANES), 0)
    lj = lax.broadcasted_iota(jnp.int32, (LANES, LANES), 1)
    upper = (li <= lj).astype(F32)
    within = jnp.dot(x, upper, preferred_element_type=F32, precision=lax.Precision.HIGHEST)
    tot = jnp.broadcast_to(within[:, LANES - 1:LANES], (r, LANES))
    ri = lax.broadcasted_iota(jnp.int32, (r, r), 0)
    rj = lax.broadcasted_iota(jnp.int32, (r, r), 1)
    strict = (rj < ri).astype(F32)
    off = jnp.dot(strict, tot, preferred_element_type=F32, precision=lax.Precision.HIGHEST)
    o_ref[...] = within + off


def _cumsum(x):
    g, s = x.shape
    r = s // LANES
    out = pl.pallas_call(
        _cumsum_kernel,
        out_shape=jax.ShapeDtypeStruct((g, r, LANES), F32),
        grid=(g,),
        in_specs=[pl.BlockSpec((None, r, LANES), lambda i: (i, 0, 0))],
        out_specs=pl.BlockSpec((None, r, LANES), lambda i: (i, 0, 0)),
        compiler_params=_cparams(("parallel",)),
        name="cumsum",
    )(x.reshape(g, r, LANES))
    return out.reshape(g, s)


def _qk(q, k):
    return lax.dot_general(q, k, (((1,), (1,)), ((), ())), preferred_element_type=F32)


def _lane_chunks(s):
    return [s[:, j * LANES:(j + 1) * LANES] for j in range(s.shape[1] // LANES)]


def _running_max(sj, m_s):
    smax = sj[0]
    for x in sj[1:]:
        smax = jnp.maximum(smax, x)
    m_prev = m_s[...]
    m_new = jnp.maximum(m_prev, jnp.max(smax, axis=-1, keepdims=True))
    m_s[...] = m_new
    return m_new, jnp.exp2(m_prev - m_new)


def _causal_tiles(t):
    tq = _tile(t, 1024)
    return tq, tq


def _fox_kernel(q_ref, k_ref, v_ref, ca_ref, o_ref, m_s, acc_s, s_s, *, tq, tk):
    qi = pl.program_id(2)
    lane = lax.broadcasted_iota(jnp.int32, (tq, LANES), 1)
    q_aug = jnp.concatenate([q_ref[...], jnp.where(lane < 3, 1.0, 0.0).astype(BF16)], axis=1)
    ones = jnp.ones((tk, LANES), BF16)
    m_s[...] = jnp.full(m_s.shape, NEG, F32)
    acc_s[...] = jnp.zeros(acc_s.shape, F32)

    def scores(kj):
        off = pl.multiple_of(kj * tk, tk)
        k_aug = jnp.concatenate([k_ref[pl.ds(off, tk), :], ca_ref[pl.ds(off, tk), :]], axis=1)
        return _qk(q_aug, k_aug)

    def consume(s, kj):
        off = pl.multiple_of(kj * tk, tk)
        v_aug = jnp.concatenate([v_ref[pl.ds(off, tk), :], ones], axis=1)
        sj = _lane_chunks(s)
        m_new, alpha = _running_max(sj, m_s)
        p = jnp.concatenate([jnp.exp2(x - m_new).astype(BF16) for x in sj], axis=1)
        pv = jnp.dot(p, v_aug, preferred_element_type=F32)
        acc_s[...] = jnp.concatenate([alpha, alpha], axis=1) * acc_s[...] + pv

    nfull = (qi * tq) // tk
    s_s[...] = scores(0)

    def body(kj, carry):
        s = s_s[...]
        s_next = scores(kj + 1)
        consume(s, kj)
        s_s[...] = s_next
        return carry

    lax.fori_loop(0, nfull, body, 0)
    row = qi * tq + lax.broadcasted_iota(jnp.int32, (tq, tk), 0)
    col = lax.broadcasted_iota(jnp.int32, (tq, tk), 1)
    for d in range(tq // tk):
        kj = nfull + d
        s = s_s[...] if d == 0 else scores(kj)
        consume(jnp.where(col + kj * tk <= row, s, NEG), kj)
    acc = acc_s[...]
    o_ref[...] = (acc[:, :HEAD_DIM] / acc[:, HEAD_DIM:]).astype(o_ref.dtype)


def _fox_prompt(q, k, v, ca, b, t):
    n, w = q.shape
    h = w // HEAD_DIM
    tq, tk = _causal_tiles(t)
    nq = t // tq
    return pl.pallas_call(
        functools.partial(_fox_kernel, tq=tq, tk=tk),
        out_shape=jax.ShapeDtypeStruct((n, w), BF16),
        grid=(b, h, nq),
        in_specs=[pl.BlockSpec((tq, HEAD_DIM), lambda bi, hi, qi: (bi * nq + qi, hi)),
                  pl.BlockSpec((t, HEAD_DIM), lambda bi, hi, qi: (bi, hi)),
                  pl.BlockSpec((t, HEAD_DIM), lambda bi, hi, qi: (bi, hi)),
                  pl.BlockSpec((None, None, t, LANES), lambda bi, hi, qi: (bi, hi, 0, 0))],
        out_specs=pl.BlockSpec((tq, HEAD_DIM), lambda bi, hi, qi: (bi * nq + qi, hi)),
        scratch_shapes=[pltpu.VMEM((tq, LANES), F32), pltpu.VMEM((tq, 2 * HEAD_DIM), F32),
                        pltpu.VMEM((tq, tk), F32)],
        compiler_params=_cparams(("parallel", "parallel", "arbitrary")),
        name="fox_prompt",
    )(q, k, v, ca)


def _diff_finish(acc1, l1, acc2, l2, lam, gain, out_scale):
    o = acc1 / l1 - lam * (acc2 / l2)
    o = o * lax.rsqrt(jnp.mean(o * o, axis=-1, keepdims=True) + SUBLN_EPS) * gain
    return o * out_scale


def _diff_kernel(lam_ref, q_ref, k_ref, v_ref, g_ref, o_ref, m1, l1, a1, s1, m2, l2, a2, s2, *, tq, tk, out_scale):
    qi = pl.program_id(2)
    state = ((0, m1, l1, a1, s1), (HEAD_DIM, m2, l2, a2, s2))
    for _, m_s, l_s, a_s, _ in state:
        m_s[...] = jnp.full(m_s.shape, NEG, F32)
        l_s[...] = jnp.zeros(l_s.shape, F32)
        a_s[...] = jnp.zeros(a_s.shape, F32)

    def scores(kj, lo):
        off = pl.multiple_of(kj * tk, tk)
        return _qk(q_ref[:, lo:lo + HEAD_DIM], k_ref[pl.ds(off, tk), lo:lo + HEAD_DIM])

    def consume(s, v, m_s, l_s, a_s):
        sj = _lane_chunks(s)
        m_new, alpha = _running_max(sj, m_s)
        pj = [jnp.exp2(x - m_new) for x in sj]
        psum = pj[0]
        for x in pj[1:]:
            psum = psum + x
        p = jnp.concatenate([x.astype(BF16) for x in pj], axis=1)
        l_s[...] = alpha * l_s[...] + psum
        a_s[...] = jnp.concatenate([alpha, alpha], axis=1) * a_s[...] + jnp.dot(p, v, preferred_element_type=F32)

    nfull = (qi * tq) // tk
    for lo, _, _, _, s_s in state:
        s_s[...] = scores(0, lo)

    def body(kj, carry):
        off = pl.multiple_of(kj * tk, tk)
        v = v_ref[pl.ds(off, tk), :]
        for lo, m_s, l_s, a_s, s_s in state:
            s = s_s[...]
            s_next = scores(kj + 1, lo)
            consume(s, v, m_s, l_s, a_s)
            s_s[...] = s_next
        return carry

    lax.fori_loop(0, nfull, body, 0)
    row = qi * tq + lax.broadcasted_iota(jnp.int32, (tq, tk), 0)
    col = lax.broadcasted_iota(jnp.int32, (tq, tk), 1)
    for d in range(tq // tk):
        kj = nfull + d
        off = pl.multiple_of(kj * tk, tk)
        v = v_ref[pl.ds(off, tk), :]
        keep = ((col + kj * tk) >> CHUNK_SHIFT) <= (row >> CHUNK_SHIFT)
        for lo, m_s, l_s, a_s, s_s in state:
            s = s_s[...] if d == 0 else scores(kj, lo)
            consume(jnp.where(keep, s, NEG), v, m_s, l_s, a_s)
    o = _diff_finish(a1[...], jnp.sum(l1[...], axis=-1, keepdims=True),
                     a2[...], jnp.sum(l2[...], axis=-1, keepdims=True), lam_ref[0], g_ref[...], out_scale)
    o_ref[...] = o.astype(o_ref.dtype)


def _diff_prompt(lam, q, k, v, gain, b, t, out_scale):
    n, w = q.shape
    hw = 2 * HEAD_DIM
    h = w // hw
    tq, tk = _causal_tiles(t)
    assert tq % CHUNK == 0
    nq = t // tq
    per_map = [pltpu.VMEM((tq, LANES), F32), pltpu.VMEM((tq, LANES), F32), pltpu.VMEM((tq, hw), F32),
               pltpu.VMEM((tq, tk), F32)]
    return pl.pallas_call(
        functools.partial(_diff_kernel, tq=tq, tk=tk, out_scale=out_scale),
        out_shape=jax.ShapeDtypeStruct((n, w), BF16),
        grid_spec=pltpu.PrefetchScalarGridSpec(
            num_scalar_prefetch=1,
            grid=(b, h, nq),
            in_specs=[pl.BlockSpec((tq, hw), lambda bi, hi, qi, lam: (bi * nq + qi, hi)),
                      pl.BlockSpec((t, hw), lambda bi, hi, qi, lam: (bi, hi)),
                      pl.BlockSpec((t, hw), lambda bi, hi, qi, lam: (bi, hi)),
                      pl.BlockSpec((1, hw), lambda bi, hi, qi, lam: (0, 0))],
            out_specs=pl.BlockSpec((tq, hw), lambda bi, hi, qi, lam: (bi * nq + qi, hi)),
            scratch_shapes=per_map + per_map),
        compiler_params=_cparams(("parallel", "parallel", "arbitrary")),
        name="diff_prompt",
    )(lam, q, k, v, gain)


def _fox_decode_kernel(q_ref, kc_ref, vc_ref, kn_ref, vn_ref, c_ref, o_ref, *, past, tq):
    q = q_ref[...]
    s_c = _qk(q, kc_ref[...].astype(BF16)) - c_ref[:, :past] * LOG2E
    c_new = c_ref[:, past:past + LANES]
    s_n = _qk(q, kn_ref[...]) - c_new[:, :tq] * LOG2E
    row = lax.broadcasted_iota(jnp.int32, (tq, tq), 0)
    col = lax.broadcasted_iota(jnp.int32, (tq, tq), 1)
    s_n = jnp.where(col <= row, s_n, NEG)
    m = jnp.maximum(jnp.max(s_c, axis=-1, keepdims=True), jnp.max(s_n, axis=-1, keepdims=True))
    p_c = jnp.exp2(s_c - m)
    p_n = jnp.exp2(s_n - m)
    l = jnp.sum(p_c, axis=-1, keepdims=True) + jnp.sum(p_n, axis=-1, keepdims=True)
    o = (jnp.dot(p_c.astype(BF16), vc_ref[...].astype(BF16), preferred_element_type=F32)
         + jnp.dot(p_n.astype(BF16), vn_ref[...], preferred_element_type=F32))
    o_ref[...] = (o / l).astype(o_ref.dtype)


def _fox_decode(q, kc, vc, kn, vn, c, b, t, past):
    n, w = q.shape
    h = w // HEAD_DIM
    assert past % LANES == 0 and t <= LANES
    spad = c.shape[-1]
    new = pl.BlockSpec((t, HEAD_DIM), lambda bi, hi: (bi, hi))
    old = pl.BlockSpec((past, HEAD_DIM), lambda bi, hi: (bi, hi))
    return pl.pallas_call(
        functools.partial(_fox_decode_kernel, past=past, tq=t),
        out_shape=jax.ShapeDtypeStruct((n, w), BF16),
        grid=(b, h),
        in_specs=[new, old, old, new, new,
                  pl.BlockSpec((None, None, 1, spad), lambda bi, hi: (bi, hi, 0, 0))],
        out_specs=new,
        compiler_params=_cparams(("parallel", "parallel")),
        name="fox_decode",
    )(q, kc, vc, kn, vn, c)


def _diff_decode_kernel(lam_ref, q_ref, kc_ref, vc_ref, kn_ref, vn_ref, g_ref, o_ref, *, past, tq, out_scale):
    vc = vc_ref[...].astype(BF16)
    vn = vn_ref[...]
    row = past + lax.broadcasted_iota(jnp.int32, (tq, tq), 0)
    col = past + lax.broadcasted_iota(jnp.int32, (tq, tq), 1)
    keep = (col >> CHUNK_SHIFT) <= (row >> CHUNK_SHIFT)
    accs, ls = [], []
    for lo in (0, HEAD_DIM):
        q = q_ref[:, lo:lo + HEAD_DIM]
        s_c = _qk(q, kc_ref[:, lo:lo + HEAD_DIM].astype(BF16))
        s_n = jnp.where(keep, _qk(q, kn_ref[:, lo:lo + HEAD_DIM]), NEG)
        m = jnp.maximum(jnp.max(s_c, axis=-1, keepdims=True), jnp.max(s_n, axis=-1, keepdims=True))
        p_c = jnp.exp2(s_c - m)
        p_n = jnp.exp2(s_n - m)
        ls.append(jnp.sum(p_c, axis=-1, keepdims=True) + jnp.sum(p_n, axis=-1, keepdims=True))
        accs.append(jnp.dot(p_c.astype(BF16), vc, preferred_element_type=F32)
                    + jnp.dot(p_n.astype(BF16), vn, preferred_element_type=F32))
    o = _diff_finish(accs[0], ls[0], accs[1], ls[1], lam_ref[0], g_ref[...], out_scale)
    o_ref[...] = o.astype(o_ref.dtype)


def _diff_decode(lam, q, kc, vc, kn, vn, gain, b, t, past, out_scale):
    n, w = q.shape
    hw = 2 * HEAD_DIM
    h = w // hw
    assert (past - 1) // CHUNK <= past // CHUNK
    new = pl.BlockSpec((t, hw), lambda bi, hi, lam: (bi, hi))
    old = pl.BlockSpec((past, hw), lambda bi, hi, lam: (bi, hi))
    return pl.pallas_call(
        functools.partial(_diff_decode_kernel, past=past, tq=t, out_scale=out_scale),
        out_shape=jax.ShapeDtypeStruct((n, w), BF16),
        grid_spec=pltpu.PrefetchScalarGridSpec(
            num_scalar_prefetch=1,
            grid=(b, h),
            in_specs=[new, old, old, new, new,
                      pl.BlockSpec((1, hw), lambda bi, hi, lam: (0, 0))],
            out_specs=new),
        compiler_params=_cparams(("parallel", "parallel")),
        name="diff_decode",
    )(lam, q, kc, vc, kn, vn, gain)


def _merge_kernel(of_ref, od_ref, gf_ref, gd_ref, wof_ref, wod_ref, o_ref):
    yf = jnp.dot(of_ref[...], wof_ref[...], preferred_element_type=F32)
    yd = jnp.dot(od_ref[...], wod_ref[...], preferred_element_type=F32)
    o_ref[...] = (gf_ref[...].astype(F32) * yf + gd_ref[...].astype(F32) * yd).astype(o_ref.dtype)


def _merge(o_fox, o_diff, gates, w_o_fox, w_o_diff):
    n = o_fox.shape[0]
    d = w_o_fox.shape[1]
    tm = _tile(n, 512)
    row = lambda c: pl.BlockSpec((tm, c), lambda i: (i, 0))
    full = lambda a: pl.BlockSpec(a.shape, lambda i: (0, 0))
    return pl.pallas_call(
        _merge_kernel,
        out_shape=jax.ShapeDtypeStruct((n, d), BF16),
        grid=(n // tm,),
        in_specs=[row(o_fox.shape[1]), row(o_diff.shape[1]),
                  pl.BlockSpec((tm, d), lambda i: (i, 0)), pl.BlockSpec((tm, d), lambda i: (i, 1)),
                  full(w_o_fox), full(w_o_diff)],
        out_specs=row(d),
        compiler_params=_cparams(("parallel",)),
        name="merge",
    )(o_fox, o_diff, gates, gates, w_o_fox, w_o_diff)


def _mix_kernel(m_ref, x_ref, wout_ref, nrm_ref, wr_ref, br_ref, h_ref, xn_ref, ridx_ref, rw_ref, *, n_experts):
    h = x_ref[...] + jnp.dot(m_ref[...], wout_ref[...], preferred_element_type=F32)
    h_ref[...] = h
    xf = h * lax.rsqrt(jnp.mean(h * h, axis=-1, keepdims=True) + NORM_EPS) * nrm_ref[...]
    xn_ref[...] = xf
    xh = xf.astype(BF16)
    xl = (xf - xh.astype(F32)).astype(BF16)
    wr = wr_ref[...]
    both = jnp.dot(xh, wr, preferred_element_type=F32)
    logits = (both[:, :LANES] + both[:, LANES:] + jnp.dot(xl, wr[:, :LANES], preferred_element_type=F32)
              + br_ref[...])
    lane = lax.broadcasted_iota(jnp.int32, logits.shape, 1).astype(F32)
    cur = jnp.where(lane < n_experts, logits, -jnp.inf)
    vals, idxs = [], []
    for _ in range(TOP_K):
        mk = jnp.max(cur, axis=-1, keepdims=True)
        ik = jnp.min(jnp.where(cur == mk, lane, float(LANES)), axis=-1, keepdims=True)
        cur = jnp.where(lane == ik, -jnp.inf, cur)
        vals.append(mk)
        idxs.append(ik)
    es = [jnp.exp(vk - vals[0]) for vk in vals]
    den = es[0] + es[1] + es[2] + es[3]
    ridx = jnp.zeros(logits.shape, F32)
    rw = jnp.zeros(logits.shape, F32)
    for k in range(TOP_K):
        ridx = jnp.where(lane == float(k), idxs[k], ridx)
        rw = jnp.where(lane == float(k), es[k] / den, rw)
    ridx_ref[...] = ridx.astype(jnp.int32)
    rw_ref[...] = rw


def _mix(merged, x, w_out, norm_ffn, w_router, b_router):
    n, d = x.shape
    e = w_router.shape[1]
    tm = _tile(n, 512)
    w_pad = _pad_cols(w_router, LANES)
    w_hi = _top_bits(w_pad)
    wr = jnp.concatenate([w_hi, _top_bits(w_pad - w_hi)], axis=1).astype(BF16)
    br = _pad_cols(b_router.reshape(1, e), LANES)
    row = lambda c: pl.BlockSpec((tm, c), lambda i: (i, 0))
    full = lambda a: pl.BlockSpec(a.shape, lambda i: (0, 0), pipeline_mode=pl.Buffered(1))
    args = [merged, x, w_out, norm_ffn.reshape(1, d), wr, br]
    in_specs = [row(d), row(d), full(w_out), full(args[3]), full(wr), full(br)]
    return pl.pallas_call(
        functools.partial(_mix_kernel, n_experts=e),
        out_shape=[jax.ShapeDtypeStruct((n, d), F32), jax.ShapeDtypeStruct((n, d), F32),
                   jax.ShapeDtypeStruct((n, LANES), jnp.int32), jax.ShapeDtypeStruct((n, LANES), F32)],
        grid=(n // tm,),
        in_specs=in_specs,
        out_specs=[row(d), row(d), row(LANES), row(LANES)],
        compiler_params=_cparams(("parallel",)),
        name="mix",
    )(*args)


def _moe_kernel(be_ref, nact_ref, tok_hbm, slot_hbm, x_hbm, wg_ref, bg_ref, wu_ref, bu_ref, wd_ref,
                bd_ref, out_hbm, xbuf, xb, tok_s, slot_s, ostage, gsem, ssem, isem, *, tb, nft, n_real):
    blk = pl.program_id(0)
    ft = pl.program_id(1)
    nact = nact_ref[0]
    active = blk < nact
    cur = lax.rem(blk, 2)
    nxt = 1 - cur
    per_step = tb // nft

    def tok_copy(b, s):
        return pltpu.make_async_copy(tok_hbm.at[pl.ds(b, 1)], tok_s.at[pl.ds(s, 1)], isem.at[0])

    def slot_copy(b, s):
        return pltpu.make_async_copy(slot_hbm.at[pl.ds(b, 1)], slot_s.at[pl.ds(s, 1)], isem.at[1])

    def gather_row(s, r):
        t = tok_s[s, r]
        pltpu.make_async_copy(x_hbm.at[pl.ds(t, 1)], xbuf.at[s, pl.ds(r, 1)], gsem.at[s]).start()

    def scatter_row(s, r):
        d = slot_s[s, r]
        pltpu.make_async_copy(ostage.at[s, pl.ds(r, 1)], out_hbm.at[pl.ds(d, 1)], ssem.at[s]).start()

    def wait_gather(s):
        pltpu.make_async_copy(xbuf.at[s], xbuf.at[s], gsem.at[s]).wait()

    def wait_scatter(s):
        pltpu.make_async_copy(ostage.at[s], ostage.at[s], ssem.at[s]).wait()

    def loop_rows(fn, s):
        def body(r, carry):
            fn(s, r)
            return carry
        lax.fori_loop(0, tb, body, 0)

    @pl.when((blk == 0) & (ft == 0))
    def _():
        ostage[0] = jnp.zeros((tb, ostage.shape[2]), F32)
        fill = pltpu.make_async_copy(ostage.at[0], out_hbm.at[pl.ds(n_real, tb)], ssem.at[0])
        fill.start()
        fill.wait()
        for b in range(2):
            c = tok_copy(b, b)
            c.start()
            c.wait()
        loop_rows(gather_row, 0)

    @pl.when((ft == 0) & (blk >= 1) & (blk <= nact))
    def _():
        tok_copy(blk + 1, nxt).wait()
        slot_copy(blk - 1, nxt).wait()

    @pl.when((ft == 0) & (blk <= nact))
    def _():
        wait_gather(cur)

    @pl.when((ft == 0) & (blk >= 2) & (blk - 2 < nact))
    def _():
        wait_scatter(cur)

    @pl.when((ft == 0) & active)
    def _():
        tok_copy(blk + 2, cur).start()
        slot_copy(blk, cur).start()
        xb[...] = xbuf[cur].astype(BF16)
        ostage[cur] = jnp.broadcast_to(bd_ref[...], ostage.shape[1:])

    @pl.when((ft == 0) & (blk == nact) & (blk >= 1))
    def _():
        loop_rows(scatter_row, nxt)

    def compute(with_scatter):
        base = ft * per_step
        for j in range(per_step):
            gather_row(nxt, base + j)
        if with_scatter:
            for j in range(per_step):
                scatter_row(nxt, base + j)
        x = xb[...]
        g = jnp.dot(x, wg_ref[...], preferred_element_type=F32) + bg_ref[...]
        u = jnp.dot(x, wu_ref[...], preferred_element_type=F32) + bu_ref[...]
        g = jnp.minimum(g, SWIGLU_LIMIT)
        u = jnp.clip(u, -SWIGLU_LIMIT, SWIGLU_LIMIT)
        act = (u + 1.0) * (g * jax.nn.sigmoid(SWIGLU_ALPHA * g))
        ostage[cur] += jnp.dot(act.astype(BF16), wd_ref[...], preferred_element_type=F32)

    @pl.when(active & (blk == 0))
    def _():
        compute(False)

    @pl.when(active & (blk > 0))
    def _():
        compute(True)


def _moe(block_exp, nact, row_tok, row_slot, xn, w_gate, b_gate, w_up, b_up, w_down, b_down, n_real):
    nb, tb = row_tok.shape
    e, nft, d, tf = w_gate.shape
    f = nft * tf
    assert nft >= 2 and tb % nft == 0 and w_down.shape == (e, f, d)

    def ftile(b, j, be, na):
        return jnp.where(b < na[0], j, nft - 1)

    any_spec = pl.BlockSpec(memory_space=pl.ANY)
    in_specs = [any_spec, any_spec, any_spec,
                pl.BlockSpec((None, None, d, tf), lambda b, j, be, na: (be[b], ftile(b, j, be, na), 0, 0)),
                pl.BlockSpec((None, 1, tf), lambda b, j, be, na: (be[b], 0, ftile(b, j, be, na))),
                pl.BlockSpec((None, None, d, tf), lambda b, j, be, na: (be[b], ftile(b, j, be, na), 0, 0)),
                pl.BlockSpec((None, 1, tf), lambda b, j, be, na: (be[b], 0, ftile(b, j, be, na))),
                pl.BlockSpec((None, tf, d), lambda b, j, be, na: (be[b], ftile(b, j, be, na), 0)),
                pl.BlockSpec((None, 1, d), lambda b, j, be, na: (be[b], 0, 0))]
    return pl.pallas_call(
        functools.partial(_moe_kernel, tb=tb, nft=nft, n_real=n_real),
        out_shape=jax.ShapeDtypeStruct((n_real + tb, d), F32),
        grid_spec=pltpu.PrefetchScalarGridSpec(
            num_scalar_prefetch=2,
            grid=(nb, nft),
            in_specs=in_specs,
            out_specs=any_spec,
            scratch_shapes=[pltpu.VMEM((2, tb, d), F32), pltpu.VMEM((tb, d), BF16),
                            pltpu.SMEM((2, tb), jnp.int32), pltpu.SMEM((2, tb), jnp.int32),
                            pltpu.VMEM((2, tb, d), F32),
                            pltpu.SemaphoreType.DMA((2,)), pltpu.SemaphoreType.DMA((2,)),
                            pltpu.SemaphoreType.DMA((2,))]),
        compiler_params=_cparams(("arbitrary", "arbitrary")),
        name="moe",
    )(block_exp, nact, row_tok, row_slot, xn, w_gate, b_gate.reshape(e, 1, f), w_up,
      b_up.reshape(e, 1, f), w_down, b_down.reshape(e, 1, d))


def _routing(top_idx, n_experts, tb):
    n = top_idx.shape[0]
    a = n * TOP_K
    expert = top_idx.reshape(-1)
    order = jnp.argsort(expert).astype(jnp.int32)
    counts = jnp.sum(expert[:, None] == jnp.arange(n_experts, dtype=jnp.int32)[None, :], axis=0, dtype=jnp.int32)
    padded = (counts + tb - 1) // tb * tb
    start = jnp.cumsum(counts) - counts
    pend = jnp.cumsum(padded)
    pstart = pend - padded
    nb = -(-a // tb) + n_experts + 1
    row0 = jnp.arange(nb, dtype=jnp.int32) * tb
    block_exp = jnp.minimum(jnp.sum(pend[None, :] <= row0[:, None], axis=1), n_experts - 1).astype(jnp.int32)
    nvalid = jnp.clip(pstart[block_exp] + counts[block_exp] - row0, 0, tb)
    src0 = jnp.clip(start[block_exp] + row0 - pstart[block_exp], 0, a)
    order_pad = jnp.concatenate([order, jnp.zeros((tb,), jnp.int32)])
    r = jnp.arange(tb, dtype=jnp.int32)[None, :]
    win = order_pad[src0[:, None] + r]
    valid = r < nvalid[:, None]
    tok = win // TOP_K
    row_tok = jnp.where(valid, tok, 0)
    row_slot = jnp.where(valid, (win % TOP_K) * n + tok, a + r)
    nact = (pend[-1] // tb).reshape(1).astype(jnp.int32)
    return block_exp, nact, row_tok.astype(jnp.int32), row_slot.astype(jnp.int32)


def _ple_kernel(h_ref, s0_ref, s1_ref, s2_ref, s3_ref, rw_ref, pe_ref, nple_ref, wg_ref, wp_ref, nfin_ref,
                o_ref, *, final):
    h = h_ref[...]
    rw = rw_ref[...]
    for k, s_ref in enumerate((s0_ref, s1_ref, s2_ref, s3_ref)):
        h = h + rw[:, k:k + 1] * s_ref[...]
    xn = h * lax.rsqrt(jnp.mean(h * h, axis=-1, keepdims=True) + NORM_EPS) * nple_ref[...]
    gate = jax.nn.sigmoid(jnp.dot(xn.astype(BF16), wg_ref[...], preferred_element_type=F32))
    h = h + gate * jnp.dot(pe_ref[...].astype(BF16), wp_ref[...], preferred_element_type=F32)
    if final:
        h = h * lax.rsqrt(jnp.mean(h * h, axis=-1, keepdims=True) + NORM_EPS) * nfin_ref[...]
    o_ref[...] = h


def _ple(h, slots, n_all, tok0, rw, pe, norm_ple, w_ple_gate, w_ple_proj, norm_final, final):
    n, d = h.shape
    tm = _tile(n, 256)
    assert TOP_K == 4 and n_all % tm == 0 and tok0 % tm == 0
    row = lambda c: pl.BlockSpec((tm, c), lambda i: (i, 0))
    full = lambda a: pl.BlockSpec(a.shape, lambda i: (0, 0))
    slot = lambda k: pl.BlockSpec((tm, d), lambda i, t0=(k * n_all + tok0) // tm: (t0 + i, 0))
    g1 = norm_ple.reshape(1, d)
    g2 = norm_final.reshape(1, d)
    return pl.pallas_call(
        functools.partial(_ple_kernel, final=final),
        out_shape=jax.ShapeDtypeStruct((n, d), F32),
        grid=(n // tm,),
        in_specs=[row(d), slot(0), slot(1), slot(2), slot(3), row(LANES), row(pe.shape[1]),
                  full(g1), full(w_ple_gate), full(w_ple_proj), full(g2)],
        out_specs=row(d),
        compiler_params=_cparams(("parallel",)),
        name="ple",
    )(h, slots, slots, slots, slots, rw, pe, g1, w_ple_gate, w_ple_proj, g2)


def _rope_tables(pos):
    half = ROPE_DIM // 2
    inv_freq = ROPE_THETA ** (-2.0 * jnp.arange(half, dtype=F32) / ROPE_DIM)
    ang = pos.astype(F32)[:, None] * inv_freq
    cos, sin = jnp.cos(ang), jnp.sin(ang)
    t = pos.shape[0]
    one = jnp.ones((t, LANES - ROPE_DIM), F32)
    zero = jnp.zeros((t, LANES - ROPE_DIM), F32)
    zh = jnp.zeros((t, half), F32)
    c = jnp.concatenate([cos, cos, one], axis=1)
    s1 = jnp.concatenate([zh, sin, zero], axis=1)
    s2 = jnp.concatenate([-sin, zh, zero], axis=1)
    return c, s1, s2


def _pad_cols(w, mult):
    c = w.shape[1]
    cp = -(-c // mult) * mult
    return jnp.pad(w, ((0, 0), (0, cp - c)))


def _layer_weights(w_in, b_forget, d):
    h_f = d // (2 * HEAD_DIM)
    fw = h_f * HEAD_DIM
    dw = fw
    o = 0
    seg = {}
    for name, width in (("fq", fw), ("fk", fw), ("fv", fw), ("fl", h_f), ("dq", dw), ("dk", dw), ("dv", dw),
                        ("gates", 2 * d)):
        seg[name] = w_in[:, o:o + width]
        o += width
    assert o == w_in.shape[1]
    out = {k: v.astype(BF16) for k, v in seg.items() if k != "fl"}
    out["fl"] = _pad_cols(seg["fl"], LANES).astype(BF16)
    out["fl_bias"] = _pad_cols(b_forget.reshape(1, h_f), LANES)
    return out, h_f


def _project(x, norm_mix, pw, tables, h_f):
    qscale = LOG2E * HEAD_DIM ** -0.5
    xn = _rmsnorm(x, norm_mix, NORM_EPS)
    fq, = _proj(xn, pw["fq"], "scale_bf", scale=qscale)
    fk, fk_b = _proj(xn, pw["fk"], "f32_bf")
    fv, fv_b = _proj(xn, pw["fv"], "f32_bf")
    logf, = _proj(xn, pw["fl"], "logsig", bias=pw["fl_bias"], out_cols=h_f)
    dq, = _proj(xn, pw["dq"], "rope_bf", scale=qscale, tables=tables)
    dk, dk_b = _proj(xn, pw["dk"], "rope_f32_bf", tables=tables)
    dv, dv_b = _proj(xn, pw["dv"], "f32_bf")
    gates, = _proj(xn, pw["gates"], "sigmoid_bf")
    return dict(fq=fq, fk=fk, fk_b=fk_b, fv=fv, fv_b=fv_b, logf=logf, dq=dq, dk=dk, dk_b=dk_b,
                dv=dv, dv_b=dv_b, gates=gates)


def _cum_logf(logf_bth):
    b, s, h = logf_bth.shape
    spad = -(-s // 1024) * 1024
    x = jnp.pad(jnp.swapaxes(logf_bth, 1, 2), ((0, 0), (0, 0), (0, spad - s)))
    return _cumsum(x.reshape(b * h, spad)).reshape(b, h, 1, spad)


def _column_tiles(w):
    e, d, f = w.shape
    tf = _tile(f, 512)
    return jnp.swapaxes(w.reshape(e, d, f // tf, tf), 1, 2).astype(BF16)


def _decay_columns(c):
    x = -LOG2E * c
    hi = _top_bits(x)
    mid = _top_bits(x - hi)
    lo = _top_bits(x - hi - mid)
    parts = jnp.stack([hi, mid, lo], axis=-1).astype(BF16)
    return jnp.pad(parts, ((0, 0), (0, 0), (0, 0), (0, LANES - 3)))


def kernel(x_prompt, x_sample, cache_fox_k, cache_fox_v, cache_fox_logf, cache_diff_k, cache_diff_v,
           p_prompt, p_sample, norm_mix, w_in, b_forget, lambda_q1, lambda_k1, lambda_q2, lambda_k2,
           diff_subln, w_o_fox, w_o_diff, w_out, norm_ffn, w_router, b_router, w_gate, b_gate,
           w_up, b_up, w_down, b_down, norm_ple, w_ple_gate, w_ple_proj, norm_final):
    depth = w_in.shape[0]
    bp, tp, d = x_prompt.shape
    bs, ts, _ = x_sample.shape
    past = cache_fox_k.shape[2]
    n_p, n_s = bp * tp, bs * ts
    n_experts = w_router.shape[-1]
    tb = 512

    h_p = x_prompt.reshape(n_p, d)
    h_s = x_sample.reshape(n_s, d)
    tab_p = _rope_tables(jnp.arange(tp, dtype=jnp.int32))
    tab_s = tuple(jnp.tile(t, (bs, 1)) for t in _rope_tables(past + jnp.arange(ts, dtype=jnp.int32)))
    st_p, st_s = [], []
    for i in range(depth):
        lam_init = 0.8 - 0.6 * math.exp(-0.3 * i)
        lam = (jnp.exp(jnp.sum(lambda_q1[i].astype(F32) * lambda_k1[i].astype(F32)))
               - jnp.exp(jnp.sum(lambda_q2[i].astype(F32) * lambda_k2[i].astype(F32)))
               + lam_init).reshape(1).astype(F32)
        out_scale = 1.0 - lam_init
        pw, h_f = _layer_weights(w_in[i], b_forget[i], d)
        h_d = h_f // 2
        subln = diff_subln[i].reshape(1, 2 * HEAD_DIM)
        wof, wod, wo = w_o_fox[i].astype(BF16), w_o_diff[i].astype(BF16), w_out[i].astype(BF16)
        wg, wu, wd = _column_tiles(w_gate[i]), _column_tiles(w_up[i]), w_down[i].astype(BF16)
        wpg, wpp = w_ple_gate[i].astype(BF16), w_ple_proj[i].astype(BF16)

        pr = _project(h_p, norm_mix[i], pw, tab_p, h_f)
        c_p = _cum_logf(pr["logf"].reshape(bp, tp, h_f))
        o_fox_p = _fox_prompt(pr["fq"], pr["fk_b"], pr["fv_b"], _decay_columns(c_p[:, :, 0, :tp]), bp, tp)
        o_diff_p = _diff_prompt(lam, pr["dq"], pr["dk_b"], pr["dv_b"], subln, bp, tp, out_scale)
        h1_p, xn_p, ridx_p, rw_p = _mix(_merge(o_fox_p, o_diff_p, pr["gates"], wof, wod), h_p, wo,
                                        norm_ffn[i], w_router[i], b_router[i])

        sr = _project(h_s, norm_mix[i], pw, tab_s, h_f)
        logf_all = jnp.concatenate([cache_fox_logf[i].astype(F32), sr["logf"].reshape(bs, ts, h_f)], axis=1)
        c_s = _cum_logf(logf_all)
        o_fox_s = _fox_decode(sr["fq"], cache_fox_k[i].reshape(bs * past, -1), cache_fox_v[i].reshape(bs * past, -1),
                              sr["fk_b"], sr["fv_b"], c_s, bs, ts, past)
        o_diff_s = _diff_decode(lam, sr["dq"], cache_diff_k[i].reshape(bs * past, -1),
                                cache_diff_v[i].reshape(bs * past, -1), sr["dk_b"], sr["dv_b"], subln,
                                bs, ts, past, out_scale)
        h1_s, xn_s, ridx_s, rw_s = _mix(_merge(o_fox_s, o_diff_s, sr["gates"], wof, wod), h_s, wo,
                                        norm_ffn[i], w_router[i], b_router[i])

        xn_all = jnp.concatenate([xn_p, xn_s], axis=0)
        top_idx = jnp.concatenate([ridx_p[:, :TOP_K], ridx_s[:, :TOP_K]], axis=0)
        block_exp, nact, row_tok, row_slot = _routing(top_idx, n_experts, tb)
        n_all = n_p + n_s
        slots = _moe(block_exp, nact, row_tok, row_slot, xn_all, wg, b_gate[i], wu, b_up[i], wd,
                     b_down[i], n_all * TOP_K)

        last = i == depth - 1
        h_p = _ple(h1_p, slots, n_all, 0, rw_p, p_prompt[i].reshape(n_p, -1), norm_ple[i], wpg, wpp,
                   norm_final, last)
        h_s = _ple(h1_s, slots, n_all, n_p, rw_s, p_sample[i].reshape(n_s, -1), norm_ple[i], wpg, wpp,
                   norm_final, last)

        st_p.append((pr["fk"].reshape(bp, tp, h_f, HEAD_DIM), pr["fv"].reshape(bp, tp, h_f, HEAD_DIM),
                     pr["logf"].reshape(bp, tp, h_f), pr["dk"].reshape(bp, tp, h_d, 2 * HEAD_DIM),
                     pr["dv"].reshape(bp, tp, h_d, 2 * HEAD_DIM)))
        st_s.append((sr["fk"].reshape(bs, ts, h_f, HEAD_DIM), sr["fv"].reshape(bs, ts, h_f, HEAD_DIM),
                     sr["logf"].reshape(bs, ts, h_f), sr["dk"].reshape(bs, ts, h_d, 2 * HEAD_DIM),
                     sr["dv"].reshape(bs, ts, h_d, 2 * HEAD_DIM)))

    y_prompt = h_p.reshape(bp, tp, d)
    y_sample = h_s.reshape(bs, ts, d)
    outs_p = [jnp.stack([s[j] for s in st_p]) for j in range(5)]
    outs_s = [jnp.stack([s[j] for s in st_s]) for j in range(5)]
    return (y_prompt, y_sample, *outs_p, *outs_s)
```

```python
import functools
import math

import jax
import jax.numpy as jnp
from jax import lax
from jax.experimental import pallas as pl
from jax.experimental.pallas import tpu as pltpu

F32 = jnp.float32
BF16 = jnp.bfloat16

HEAD_DIM = 128
CHUNK = 64
CHUNK_SHIFT = 6
assert 1 << CHUNK_SHIFT == CHUNK
ROPE_DIM = HEAD_DIM // 4
ROPE_THETA = 500000.0
TOP_K = 4
SWIGLU_LIMIT = 7.0
SWIGLU_ALPHA = 1.702
NORM_EPS = 1e-6
SUBLN_EPS = 1e-5
LOG2E = 1.4426950408889634
NEG = -1e30
LANES = 128
VMEM_LIMIT = 56 * 1024 * 1024


def _cparams(sem):
    return pltpu.CompilerParams(dimension_semantics=sem, vmem_limit_bytes=VMEM_LIMIT)


def _top_bits(v):
    bits = lax.bitcast_convert_type(v, jnp.uint32) & jnp.uint32(0xFFFF0000)
    return lax.bitcast_convert_type(bits, F32)


def _pack_halves(x):
    half = x.shape[1] // 2

    def rounded(v):
        bits = lax.bitcast_convert_type(v, jnp.uint32)
        return bits + jnp.uint32(0x7FFF) + ((bits >> 16) & jnp.uint32(1))

    return (rounded(x[:, half:]) & jnp.uint32(0xFFFF0000)) | (rounded(x[:, :half]) >> 16)


def _unpack_halves(w):
    lo = lax.bitcast_convert_type(w << 16, F32)
    hi = lax.bitcast_convert_type(w & jnp.uint32(0xFFFF0000), F32)
    return lo, hi


def _tile(n, pref):
    t = min(n, pref)
    assert n % t == 0, (n, pref)
    return t


def _rmsnorm_kernel(x_ref, g_ref, o_ref, *, eps):
    x = x_ref[...]
    y = x * lax.rsqrt(jnp.mean(x * x, axis=-1, keepdims=True) + eps) * g_ref[...]
    o_ref[...] = y.astype(o_ref.dtype)


def _rmsnorm(x, gain, eps):
    n, d = x.shape
    tm = _tile(n, 1024)
    return pl.pallas_call(
        functools.partial(_rmsnorm_kernel, eps=eps),
        out_shape=jax.ShapeDtypeStruct((n, d), BF16),
        grid=(n // tm,),
        in_specs=[pl.BlockSpec((tm, d), lambda i: (i, 0)),
                  pl.BlockSpec((1, d), lambda i: (0, 0))],
        out_specs=pl.BlockSpec((tm, d), lambda i: (i, 0)),
        compiler_params=_cparams(("parallel",)),
        name="rmsnorm",
    )(x, gain.reshape(1, d))


def _rope_slab(x, c, s1, s2):
    return x * c + pltpu.roll(x, ROPE_DIM // 2, 1) * s1 + pltpu.roll(x, LANES - ROPE_DIM // 2, 1) * s2


def _proj_kernel(*refs, kind, scale):
    xn_ref, w_ref = refs[0], refs[1]
    acc = jnp.dot(xn_ref[...], w_ref[...], preferred_element_type=F32)
    if kind == "scale_bf":
        refs[2][...] = (acc * scale).astype(BF16)
    elif kind == "f32_bf":
        refs[2][...] = acc
        refs[3][...] = acc.astype(BF16)
    elif kind == "sigmoid_bf":
        refs[2][...] = jax.nn.sigmoid(acc).astype(BF16)
    elif kind == "logsig":
        z = acc + refs[2][...]
        val = jnp.minimum(z, 0.0) - jnp.log1p(jnp.exp(-jnp.abs(z)))
        refs[3][...] = val[:, :refs[3].shape[1]]
    elif kind in ("rope_bf", "rope_f32_bf"):
        c, s1, s2 = refs[2][...], refs[3][...], refs[4][...]
        for j in range(acc.shape[1] // LANES):
            sl = slice(j * LANES, (j + 1) * LANES)
            r = _rope_slab(acc[:, sl], c, s1, s2)
            if kind == "rope_bf":
                refs[5][:, sl] = (r * scale).astype(BF16)
            else:
                refs[5][:, sl] = r
                refs[6][:, sl] = r.astype(BF16)
    else:
        raise ValueError(kind)


def _proj(xn, w, kind, *, scale=1.0, tables=None, bias=None, out_cols=None):
    n, d = xn.shape
    c = w.shape[1]
    tm = _tile(n, 1024)
    tn = _tile(c, 1024)
    grid = (c // tn, n // tm)
    in_specs = [pl.BlockSpec((tm, d), lambda j, i: (i, 0)),
                pl.BlockSpec((d, tn), lambda j, i: (0, j))]
    args = [xn, w]
    blk = pl.BlockSpec((tm, tn), lambda j, i: (i, j))
    if kind == "scale_bf" or kind == "sigmoid_bf":
        out_shape = [jax.ShapeDtypeStruct((n, c), BF16)]
        out_specs = [blk]
    elif kind == "f32_bf":
        out_shape = [jax.ShapeDtypeStruct((n, c), F32), jax.ShapeDtypeStruct((n, c), BF16)]
        out_specs = [blk, blk]
    elif kind == "logsig":
        in_specs.append(pl.BlockSpec((1, tn), lambda j, i: (0, j)))
        args.append(bias)
        out_shape = [jax.ShapeDtypeStruct((n, out_cols), F32)]
        out_specs = [pl.BlockSpec((tm, out_cols), lambda j, i: (i, 0))]
    else:
        nt = tables[0].shape[0] // tm
        for t in tables:
            in_specs.append(pl.BlockSpec((tm, LANES), lambda j, i, nt=nt: (i % nt, 0)))
            args.append(t)
        if kind == "rope_bf":
            out_shape = [jax.ShapeDtypeStruct((n, c), BF16)]
            out_specs = [blk]
        else:
            out_shape = [jax.ShapeDtypeStruct((n, c), F32), jax.ShapeDtypeStruct((n, c), BF16)]
            out_specs = [blk, blk]
    return pl.pallas_call(
        functools.partial(_proj_kernel, kind=kind, scale=scale),
        out_shape=out_shape,
        grid=grid,
        in_specs=in_specs,
        out_specs=out_specs,
        compiler_params=_cparams(("parallel", "parallel")),
        name="proj_" + kind,
    )(*args)


def _cumsum_kernel(x_ref, o_ref):
    x = x_ref[...]
    r = x.shape[0]
    li = lax.broadcasted_iota(jnp.int32, (LANES, LANES), 0)
    lj = lax.broadcasted_iota(jnp.int32, (LANES, LANES), 1)
    upper = (li <= lj).astype(F32)
    within = jnp.dot(x, upper, preferred_element_type=F32, precision=lax.Precision.HIGHEST)
    tot = jnp.broadcast_to(within[:, LANES - 1:LANES], (r, LANES))
    ri = lax.broadcasted_iota(jnp.int32, (r, r), 0)
    rj = lax.broadcasted_iota(jnp.int32, (r, r), 1)
    strict = (rj < ri).astype(F32)
    off = jnp.dot(strict, tot, preferred_element_type=F32, precision=lax.Precision.HIGHEST)
    o_ref[...] = within + off


def _cumsum(x):
    g, s = x.shape
    r = s // LANES
    out = pl.pallas_call(
        _cumsum_kernel,
        out_shape=jax.ShapeDtypeStruct((g, r, LANES), F32),
        grid=(g,),
        in_specs=[pl.BlockSpec((None, r, LANES), lambda i: (i, 0, 0))],
        out_specs=pl.BlockSpec((None, r, LANES), lambda i: (i, 0, 0)),
        compiler_params=_cparams(("parallel",)),
        name="cumsum",
    )(x.reshape(g, r, LANES))
    return out.reshape(g, s)


def _qk(q, k):
    return lax.dot_general(q, k, (((1,), (1,)), ((), ())), preferred_element_type=F32)


def _lane_chunks(s):
    return [s[:, j * LANES:(j + 1) * LANES] for j in range(s.shape[1] // LANES)]


def _running_max(sj, m_s):
    smax = sj[0]
    for x in sj[1:]:
        smax = jnp.maximum(smax, x)
    m_prev = m_s[...]
    m_new = jnp.maximum(m_prev, jnp.max(smax, axis=-1, keepdims=True))
    m_s[...] = m_new
    return m_new, jnp.exp2(m_prev - m_new)


def _causal_tiles(t):
    tq = _tile(t, 1024)
    return tq, tq


def _fox_kernel(q_ref, k_ref, v_ref, ca_ref, o_ref, m_s, acc_s, s_s, *, tq, tk):
    qi = pl.program_id(2)
    lane = lax.broadcasted_iota(jnp.int32, (tq, LANES), 1)
    q_aug = jnp.concatenate([q_ref[...], jnp.where(lane < 3, 1.0, 0.0).astype(BF16)], axis=1)
    ones = jnp.ones((tk, LANES), BF16)
    m_s[...] = jnp.full(m_s.shape, NEG, F32)
    acc_s[...] = jnp.zeros(acc_s.shape, F32)

    def scores(kj):
        off = pl.multiple_of(kj * tk, tk)
        k_aug = jnp.concatenate([k_ref[pl.ds(off, tk), :], ca_ref[pl.ds(off, tk), :]], axis=1)
        return _qk(q_aug, k_aug)

    def consume(s, kj):
        off = pl.multiple_of(kj * tk, tk)
        v_aug = jnp.concatenate([v_ref[pl.ds(off, tk), :], ones], axis=1)
        sj = _lane_chunks(s)
        m_new, alpha = _running_max(sj, m_s)
        p = jnp.concatenate([jnp.exp2(x - m_new).astype(BF16) for x in sj], axis=1)
        pv = jnp.dot(p, v_aug, preferred_element_type=F32)
        acc_s[...] = jnp.concatenate([alpha, alpha], axis=1) * acc_s[...] + pv

    nfull = (qi * tq) // tk
    s_s[...] = scores(0)

    def body(kj, carry):
        s = s_s[...]
        s_next = scores(kj + 1)
        consume(s, kj)
        s_s[...] = s_next
        return carry

    lax.fori_loop(0, nfull, body, 0)
    row = qi * tq + lax.broadcasted_iota(jnp.int32, (tq, tk), 0)
    col = lax.broadcasted_iota(jnp.int32, (tq, tk), 1)
    for d in range(tq // tk):
        kj = nfull + d
        s = s_s[...] if d == 0 else scores(kj)
        consume(jnp.where(col + kj * tk <= row, s, NEG), kj)
    acc = acc_s[...]
    o_ref[...] = (acc[:, :HEAD_DIM] / acc[:, HEAD_DIM:]).astype(o_ref.dtype)


def _fox_prompt(q, k, v, ca, b, t):
    n, w = q.shape
    h = w // HEAD_DIM
    tq, tk = _causal_tiles(t)
    nq = t // tq
    return pl.pallas_call(
        functools.partial(_fox_kernel, tq=tq, tk=tk),
        out_shape=jax.ShapeDtypeStruct((n, w), BF16),
        grid=(b, h, nq),
        in_specs=[pl.BlockSpec((tq, HEAD_DIM), lambda bi, hi, qi: (bi * nq + qi, hi)),
                  pl.BlockSpec((t, HEAD_DIM), lambda bi, hi, qi: (bi, hi)),
                  pl.BlockSpec((t, HEAD_DIM), lambda bi, hi, qi: (bi, hi)),
                  pl.BlockSpec((None, None, t, LANES), lambda bi, hi, qi: (bi, hi, 0, 0))],
        out_specs=pl.BlockSpec((tq, HEAD_DIM), lambda bi, hi, qi: (bi * nq + qi, hi)),
        scratch_shapes=[pltpu.VMEM((tq, LANES), F32), pltpu.VMEM((tq, 2 * HEAD_DIM), F32),
                        pltpu.VMEM((tq, tk), F32)],
        compiler_params=_cparams(("parallel", "parallel", "arbitrary")),
        name="fox_prompt",
    )(q, k, v, ca)


def _diff_finish(acc1, l1, acc2, l2, lam, gain, out_scale):
    o = acc1 / l1 - lam * (acc2 / l2)
    o = o * lax.rsqrt(jnp.mean(o * o, axis=-1, keepdims=True) + SUBLN_EPS) * gain
    return o * out_scale


def _diff_kernel(lam_ref, q_ref, k_ref, v_ref, g_ref, o_ref, m1, l1, a1, s1, m2, l2, a2, s2, *, tq, tk, out_scale):
    qi = pl.program_id(2)
    state = ((0, m1, l1, a1, s1), (HEAD_DIM, m2, l2, a2, s2))
    for _, m_s, l_s, a_s, _ in state:
        m_s[...] = jnp.full(m_s.shape, NEG, F32)
        l_s[...] = jnp.zeros(l_s.shape, F32)
        a_s[...] = jnp.zeros(a_s.shape, F32)

    def scores(kj, lo):
        off = pl.multiple_of(kj * tk, tk)
        return _qk(q_ref[:, lo:lo + HEAD_DIM], k_ref[pl.ds(off, tk), lo:lo + HEAD_DIM])

    def consume(s, v, m_s, l_s, a_s):
        sj = _lane_chunks(s)
        m_new, alpha = _running_max(sj, m_s)
        pj = [jnp.exp2(x - m_new) for x in sj]
        psum = pj[0]
        for x in pj[1:]:
            psum = psum + x
        p = jnp.concatenate([x.astype(BF16) for x in pj], axis=1)
        l_s[...] = alpha * l_s[...] + psum
        a_s[...] = jnp.concatenate([alpha, alpha], axis=1) * a_s[...] + jnp.dot(p, v, preferred_element_type=F32)

    nfull = (qi * tq) // tk
    for lo, _, _, _, s_s in state:
        s_s[...] = scores(0, lo)

    def body(kj, carry):
        off = pl.multiple_of(kj * tk, tk)
        v = v_ref[pl.ds(off, tk), :]
        for lo, m_s, l_s, a_s, s_s in state:
            s = s_s[...]
            s_next = scores(kj + 1, lo)
            consume(s, v, m_s, l_s, a_s)
            s_s[...] = s_next
        return carry

    lax.fori_loop(0, nfull, body, 0)
    row = qi * tq + lax.broadcasted_iota(jnp.int32, (tq, tk), 0)
    col = lax.broadcasted_iota(jnp.int32, (tq, tk), 1)
    for d in range(tq // tk):
        kj = nfull + d
        off = pl.multiple_of(kj * tk, tk)
        v = v_ref[pl.ds(off, tk), :]
        keep = ((col + kj * tk) >> CHUNK_SHIFT) <= (row >> CHUNK_SHIFT)
        for lo, m_s, l_s, a_s, s_s in state:
            s = s_s[...] if d == 0 else scores(kj, lo)
            consume(jnp.where(keep, s, NEG), v, m_s, l_s, a_s)
    o = _diff_finish(a1[...], jnp.sum(l1[...], axis=-1, keepdims=True),
                     a2[...], jnp.sum(l2[...], axis=-1, keepdims=True), lam_ref[0], g_ref[...], out_scale)
    o_ref[...] = o.astype(o_ref.dtype)


def _diff_prompt(lam, q, k, v, gain, b, t, out_scale):
    n, w = q.shape
    hw = 2 * HEAD_DIM
    h = w // hw
    tq, tk = _causal_tiles(t)
    assert tq % CHUNK == 0
    nq = t // tq
    per_map = [pltpu.VMEM((tq, LANES), F32), pltpu.VMEM((tq, LANES), F32), pltpu.VMEM((tq, hw), F32),
               pltpu.VMEM((tq, tk), F32)]
    return pl.pallas_call(
        functools.partial(_diff_kernel, tq=tq, tk=tk, out_scale=out_scale),
        out_shape=jax.ShapeDtypeStruct((n, w), BF16),
        grid_spec=pltpu.PrefetchScalarGridSpec(
            num_scalar_prefetch=1,
            grid=(b, h, nq),
            in_specs=[pl.BlockSpec((tq, hw), lambda bi, hi, qi, lam: (bi * nq + qi, hi)),
                      pl.BlockSpec((t, hw), lambda bi, hi, qi, lam: (bi, hi)),
                      pl.BlockSpec((t, hw), lambda bi, hi, qi, lam: (bi, hi)),
                      pl.BlockSpec((1, hw), lambda bi, hi, qi, lam: (0, 0))],
            out_specs=pl.BlockSpec((tq, hw), lambda bi, hi, qi, lam: (bi * nq + qi, hi)),
            scratch_shapes=per_map + per_map),
        compiler_params=_cparams(("parallel", "parallel", "arbitrary")),
        name="diff_prompt",
    )(lam, q, k, v, gain)


def _fox_decode_kernel(q_ref, kc_ref, vc_ref, kn_ref, vn_ref, c_ref, o_ref, *, past, tq):
    q = q_ref[...]
    s_c = _qk(q, kc_ref[...].astype(BF16)) - c_ref[:, :past] * LOG2E
    c_new = c_ref[:, past:past + LANES]
    s_n = _qk(q, kn_ref[...]) - c_new[:, :tq] * LOG2E
    row = lax.broadcasted_iota(jnp.int32, (tq, tq), 0)
    col = lax.broadcasted_iota(jnp.int32, (tq, tq), 1)
    s_n = jnp.where(col <= row, s_n, NEG)
    m = jnp.maximum(jnp.max(s_c, axis=-1, keepdims=True), jnp.max(s_n, axis=-1, keepdims=True))
    p_c = jnp.exp2(s_c - m)
    p_n = jnp.exp2(s_n - m)
    l = jnp.sum(p_c, axis=-1, keepdims=True) + jnp.sum(p_n, axis=-1, keepdims=True)
    o = (jnp.dot(p_c.astype(BF16), vc_ref[...].astype(BF16), preferred_element_type=F32)
         + jnp.dot(p_n.astype(BF16), vn_ref[...], preferred_element_type=F32))
    o_ref[...] = (o / l).astype(o_ref.dtype)


def _fox_decode(q, kc, vc, kn, vn, c, b, t, past):
    n, w = q.shape
    h = w // HEAD_DIM
    assert past % LANES == 0 and t <= LANES
    spad = c.shape[-1]
    new = pl.BlockSpec((t, HEAD_DIM), lambda bi, hi: (bi, hi))
    old = pl.BlockSpec((past, HEAD_DIM), lambda bi, hi: (bi, hi))
    return pl.pallas_call(
        functools.partial(_fox_decode_kernel, past=past, tq=t),
        out_shape=jax.ShapeDtypeStruct((n, w), BF16),
        grid=(b, h),
        in_specs=[new, old, old, new, new,
                  pl.BlockSpec((None, None, 1, spad), lambda bi, hi: (bi, hi, 0, 0))],
        out_specs=new,
        compiler_params=_cparams(("parallel", "parallel")),
        name="fox_decode",
    )(q, kc, vc, kn, vn, c)


def _diff_decode_kernel(lam_ref, q_ref, kc_ref, vc_ref, kn_ref, vn_ref, g_ref, o_ref, *, past, tq, out_scale):
    vc = vc_ref[...].astype(BF16)
    vn = vn_ref[...]
    row = past + lax.broadcasted_iota(jnp.int32, (tq, tq), 0)
    col = past + lax.broadcasted_iota(jnp.int32, (tq, tq), 1)
    keep = (col >> CHUNK_SHIFT) <= (row >> CHUNK_SHIFT)
    accs, ls = [], []
    for lo in (0, HEAD_DIM):
        q = q_ref[:, lo:lo + HEAD_DIM]
        s_c = _qk(q, kc_ref[:, lo:lo + HEAD_DIM].astype(BF16))
        s_n = jnp.where(keep, _qk(q, kn_ref[:, lo:lo + HEAD_DIM]), NEG)
        m = jnp.maximum(jnp.max(s_c, axis=-1, keepdims=True), jnp.max(s_n, axis=-1, keepdims=True))
        p_c = jnp.exp2(s_c - m)
        p_n = jnp.exp2(s_n - m)
        ls.append(jnp.sum(p_c, axis=-1, keepdims=True) + jnp.sum(p_n, axis=-1, keepdims=True))
        accs.append(jnp.dot(p_c.astype(BF16), vc, preferred_element_type=F32)
                    + jnp.dot(p_n.astype(BF16), vn, preferred_element_type=F32))
    o = _diff_finish(accs[0], ls[0], accs[1], ls[1], lam_ref[0], g_ref[...], out_scale)
    o_ref[...] = o.astype(o_ref.dtype)


def _diff_decode(lam, q, kc, vc, kn, vn, gain, b, t, past, out_scale):
    n, w = q.shape
    hw = 2 * HEAD_DIM
    h = w // hw
    assert (past - 1) // CHUNK <= past // CHUNK
    new = pl.BlockSpec((t, hw), lambda bi, hi, lam: (bi, hi))
    old = pl.BlockSpec((past, hw), lambda bi, hi, lam: (bi, hi))
    return pl.pallas_call(
        functools.partial(_diff_decode_kernel, past=past, tq=t, out_scale=out_scale),
        out_shape=jax.ShapeDtypeStruct((n, w), BF16),
        grid_spec=pltpu.PrefetchScalarGridSpec(
            num_scalar_prefetch=1,
            grid=(b, h),
            in_specs=[new, old, old, new, new,
                      pl.BlockSpec((1, hw), lambda bi, hi, lam: (0, 0))],
            out_specs=new),
        compiler_params=_cparams(("parallel", "parallel")),
        name="diff_decode",
    )(lam, q, kc, vc, kn, vn, gain)


def _merge_kernel(of_ref, od_ref, gf_ref, gd_ref, wof_ref, wod_ref, o_ref):
    yf = jnp.dot(of_ref[...], wof_ref[...], preferred_element_type=F32)
    yd = jnp.dot(od_ref[...], wod_ref[...], preferred_element_type=F32)
    o_ref[...] = (gf_ref[...].astype(F32) * yf + gd_ref[...].astype(F32) * yd).astype(o_ref.dtype)


def _merge(o_fox, o_diff, gates, w_o_fox, w_o_diff):
    n = o_fox.shape[0]
    d = w_o_fox.shape[1]
    tm = _tile(n, 512)
    row = lambda c: pl.BlockSpec((tm, c), lambda i: (i, 0))
    full = lambda a: pl.BlockSpec(a.shape, lambda i: (0, 0))
    return pl.pallas_call(
        _merge_kernel,
        out_shape=jax.ShapeDtypeStruct((n, d), BF16),
        grid=(n // tm,),
        in_specs=[row(o_fox.shape[1]), row(o_diff.shape[1]),
                  pl.BlockSpec((tm, d), lambda i: (i, 0)), pl.BlockSpec((tm, d), lambda i: (i, 1)),
                  full(w_o_fox), full(w_o_diff)],
        out_specs=row(d),
        compiler_params=_cparams(("parallel",)),
        name="merge",
    )(o_fox, o_diff, gates, gates, w_o_fox, w_o_diff)


def _mix_kernel(m_ref, x_ref, wout_ref, nrm_ref, wr_ref, br_ref, h_ref, xn_ref, ridx_ref, rw_ref, *, n_experts):
    h = x_ref[...] + jnp.dot(m_ref[...], wout_ref[...], preferred_element_type=F32)
    h_ref[...] = h
    xf = h * lax.rsqrt(jnp.mean(h * h, axis=-1, keepdims=True) + NORM_EPS) * nrm_ref[...]
    xn_ref[...] = _pack_halves(xf)
    xh = xf.astype(BF16)
    xl = (xf - xh.astype(F32)).astype(BF16)
    wr = wr_ref[...]
    both = jnp.dot(xh, wr, preferred_element_type=F32)
    logits = (both[:, :LANES] + both[:, LANES:] + jnp.dot(xl, wr[:, :LANES], preferred_element_type=F32)
              + br_ref[...])
    lane = lax.broadcasted_iota(jnp.int32, logits.shape, 1).astype(F32)
    cur = jnp.where(lane < n_experts, logits, -jnp.inf)
    vals, idxs = [], []
    for _ in range(TOP_K):
        mk = jnp.max(cur, axis=-1, keepdims=True)
        ik = jnp.min(jnp.where(cur == mk, lane, float(LANES)), axis=-1, keepdims=True)
        cur = jnp.where(lane == ik, -jnp.inf, cur)
        vals.append(mk)
        idxs.append(ik)
    es = [jnp.exp(vk - vals[0]) for vk in vals]
    den = es[0] + es[1] + es[2] + es[3]
    ridx = jnp.zeros(logits.shape, F32)
    rw = jnp.zeros(logits.shape, F32)
    for k in range(TOP_K):
        ridx = jnp.where(lane == float(k), idxs[k], ridx)
        rw = jnp.where(lane == float(k), es[k] / den, rw)
    ridx_ref[...] = ridx.astype(jnp.int32)
    rw_ref[...] = rw


def _mix(merged, x, w_out, norm_ffn, w_router, b_router):
    n, d = x.shape
    e = w_router.shape[1]
    tm = _tile(n, 512)
    w_pad = _pad_cols(w_router, LANES)
    w_hi = _top_bits(w_pad)
    wr = jnp.concatenate([w_hi, _top_bits(w_pad - w_hi)], axis=1).astype(BF16)
    br = _pad_cols(b_router.reshape(1, e), LANES)
    row = lambda c: pl.BlockSpec((tm, c), lambda i: (i, 0))
    full = lambda a: pl.BlockSpec(a.shape, lambda i: (0, 0), pipeline_mode=pl.Buffered(1))
    args = [merged, x, w_out, norm_ffn.reshape(1, d), wr, br]
    in_specs = [row(d), row(d), full(w_out), full(args[3]), full(wr), full(br)]
    return pl.pallas_call(
        functools.partial(_mix_kernel, n_experts=e),
        out_shape=[jax.ShapeDtypeStruct((n, d), F32), jax.ShapeDtypeStruct((n, d // 2), jnp.uint32),
                   jax.ShapeDtypeStruct((n, LANES), jnp.int32), jax.ShapeDtypeStruct((n, LANES), F32)],
        grid=(n // tm,),
        in_specs=in_specs,
        out_specs=[row(d), row(d // 2), row(LANES), row(LANES)],
        compiler_params=_cparams(("parallel",)),
        name="mix",
    )(*args)


def _moe_kernel(be_ref, nact_ref, tok_hbm, slot_hbm, x_hbm, wg_ref, bg_ref, wu_ref, bu_ref, wd_ref,
                bd_ref, out_hbm, xbuf, xb, tok_s, slot_s, acc, ostage, gsem, ssem, isem, *, tb, nft, n_real):
    blk = pl.program_id(0)
    ft = pl.program_id(1)
    nact = nact_ref[0]
    active = blk < nact
    cur = lax.rem(blk, 2)
    nxt = 1 - cur
    per_step = tb // nft

    def tok_copy(b, s):
        return pltpu.make_async_copy(tok_hbm.at[pl.ds(b, 1)], tok_s.at[pl.ds(s, 1)], isem.at[0])

    def slot_copy(b, s):
        return pltpu.make_async_copy(slot_hbm.at[pl.ds(b, 1)], slot_s.at[pl.ds(s, 1)], isem.at[1])

    def gather_row(s, r):
        t = tok_s[s, r]
        pltpu.make_async_copy(x_hbm.at[pl.ds(t, 1)], xbuf.at[s, pl.ds(r, 1)], gsem.at[s]).start()

    def scatter_row(s, r):
        d = slot_s[s, r]
        pltpu.make_async_copy(ostage.at[s, pl.ds(r, 1)], out_hbm.at[pl.ds(d, 1)], ssem.at[s]).start(priority=1)

    def wait_gather(s):
        pltpu.make_async_copy(xbuf.at[s], xbuf.at[s], gsem.at[s]).wait()

    def wait_scatter(s):
        pltpu.make_async_copy(ostage.at[s], ostage.at[s], ssem.at[s]).wait()

    def loop_rows(fn, s):
        def body(r, carry):
            fn(s, r)
            return carry
        lax.fori_loop(0, tb, body, 0)

    @pl.when((blk == 0) & (ft == 0))
    def _():
        ostage[0] = jnp.zeros(ostage.shape[1:], ostage.dtype)
        fill = pltpu.make_async_copy(ostage.at[0], out_hbm.at[pl.ds(n_real, tb)], ssem.at[0])
        fill.start()
        fill.wait()
        for b in range(2):
            c = tok_copy(b, b)
            c.start()
            c.wait()
        loop_rows(gather_row, 0)

    @pl.when((ft == 0) & (blk >= 1) & (blk <= nact))
    def _():
        tok_copy(blk + 1, nxt).wait()
        slot_copy(blk - 1, nxt).wait()

    @pl.when((ft == 0) & (blk <= nact))
    def _():
        wait_gather(cur)

    @pl.when((ft == 0) & (blk >= 2) & (blk - 2 < nact))
    def _():
        wait_scatter(cur)

    @pl.when((ft == 0) & active)
    def _():
        tok_copy(blk + 2, cur).start()
        slot_copy(blk, cur).start()
        lo, hi = _unpack_halves(xbuf[cur])
        xb[...] = jnp.concatenate([lo.astype(BF16), hi.astype(BF16)], axis=1)
        acc[...] = jnp.broadcast_to(bd_ref[...], acc.shape)

    @pl.when((ft == 0) & (blk == nact) & (blk >= 1))
    def _():
        loop_rows(scatter_row, nxt)

    def compute(with_scatter):
        base = ft * per_step
        for j in range(per_step):
            gather_row(nxt, base + j)
        if with_scatter:
            for j in range(per_step):
                scatter_row(nxt, base + j)
        x = xb[...]
        g = jnp.dot(x, wg_ref[...], preferred_element_type=F32) + bg_ref[...]
        u = jnp.dot(x, wu_ref[...], preferred_element_type=F32) + bu_ref[...]
        g = jnp.minimum(g, SWIGLU_LIMIT)
        u = jnp.clip(u, -SWIGLU_LIMIT, SWIGLU_LIMIT)
        act = (u + 1.0) * (g * jax.nn.sigmoid(SWIGLU_ALPHA * g))
        acc[...] += jnp.dot(act.astype(BF16), wd_ref[...], preferred_element_type=F32)

    @pl.when(active & (blk == 0))
    def _():
        compute(False)

    @pl.when(active & (blk > 0))
    def _():
        compute(True)

    @pl.when(active & (ft == nft - 1))
    def _():
        ostage[cur] = _pack_halves(acc[...])


def _moe(block_exp, nact, row_tok, row_slot, xn, w_gate, b_gate, w_up, b_up, w_down, b_down, n_real):
    nb, tb = row_tok.shape
    e, d, f = w_gate.shape
    tf = _tile(f, 512)
    nft = f // tf
    d2 = d // 2
    assert nft >= 2 and tb % nft == 0 and xn.shape[1] == d2

    def ftile(b, j, be, na):
        return jnp.where(b < na[0], j, nft - 1)

    any_spec = pl.BlockSpec(memory_space=pl.ANY)
    in_specs = [any_spec, any_spec, any_spec,
                pl.BlockSpec((None, d, tf), lambda b, j, be, na: (be[b], 0, ftile(b, j, be, na))),
                pl.BlockSpec((None, 1, tf), lambda b, j, be, na: (be[b], 0, ftile(b, j, be, na))),
                pl.BlockSpec((None, d, tf), lambda b, j, be, na: (be[b], 0, ftile(b, j, be, na))),
                pl.BlockSpec((None, 1, tf), lambda b, j, be, na: (be[b], 0, ftile(b, j, be, na))),
                pl.BlockSpec((None, tf, d), lambda b, j, be, na: (be[b], ftile(b, j, be, na), 0)),
                pl.BlockSpec((None, 1, d), lambda b, j, be, na: (be[b], 0, 0))]
    return pl.pallas_call(
        functools.partial(_moe_kernel, tb=tb, nft=nft, n_real=n_real),
        out_shape=jax.ShapeDtypeStruct((n_real + tb, d2), jnp.uint32),
        grid_spec=pltpu.PrefetchScalarGridSpec(
            num_scalar_prefetch=2,
            grid=(nb, nft),
            in_specs=in_specs,
            out_specs=any_spec,
            scratch_shapes=[pltpu.VMEM((2, tb, d2), jnp.uint32), pltpu.VMEM((tb, d), BF16),
                            pltpu.SMEM((2, tb), jnp.int32), pltpu.SMEM((2, tb), jnp.int32),
                            pltpu.VMEM((tb, d), F32), pltpu.VMEM((2, tb, d2), jnp.uint32),
                            pltpu.SemaphoreType.DMA((2,)), pltpu.SemaphoreType.DMA((2,)),
                            pltpu.SemaphoreType.DMA((2,))]),
        compiler_params=_cparams(("arbitrary", "arbitrary")),
        name="moe",
    )(block_exp, nact, row_tok, row_slot, xn, w_gate, b_gate.reshape(e, 1, f), w_up,
      b_up.reshape(e, 1, f), w_down, b_down.reshape(e, 1, d))


def _routing(top_idx, n_experts, tb):
    n = top_idx.shape[0]
    a = n * TOP_K
    expert = top_idx.reshape(-1)
    order = jnp.argsort(expert).astype(jnp.int32)
    counts = jnp.sum(expert[:, None] == jnp.arange(n_experts, dtype=jnp.int32)[None, :], axis=0, dtype=jnp.int32)
    padded = (counts + tb - 1) // tb * tb
    start = jnp.cumsum(counts) - counts
    pend = jnp.cumsum(padded)
    pstart = pend - padded
    nb = -(-a // tb) + n_experts + 1
    row0 = jnp.arange(nb, dtype=jnp.int32) * tb
    block_exp = jnp.minimum(jnp.sum(pend[None, :] <= row0[:, None], axis=1), n_experts - 1).astype(jnp.int32)
    nvalid = jnp.clip(pstart[block_exp] + counts[block_exp] - row0, 0, tb)
    src0 = jnp.clip(start[block_exp] + row0 - pstart[block_exp], 0, a)
    order_pad = jnp.concatenate([order, jnp.zeros((tb,), jnp.int32)])
    r = jnp.arange(tb, dtype=jnp.int32)[None, :]
    win = order_pad[src0[:, None] + r]
    valid = r < nvalid[:, None]
    tok = win // TOP_K
    row_tok = jnp.where(valid, tok, 0)
    row_slot = jnp.where(valid, (win % TOP_K) * n + tok, a + r)
    nact = (pend[-1] // tb).reshape(1).astype(jnp.int32)
    return block_exp, nact, row_tok.astype(jnp.int32), row_slot.astype(jnp.int32)


def _ple_kernel(h_ref, s0_ref, s1_ref, s2_ref, s3_ref, rw_ref, pe_ref, nple_ref, wg_ref, wp_ref, nfin_ref,
                o_ref, *, final):
    h = h_ref[...]
    rw = rw_ref[...]
    for k, s_ref in enumerate((s0_ref, s1_ref, s2_ref, s3_ref)):
        h = h + rw[:, k:k + 1] * jnp.concatenate(_unpack_halves(s_ref[...]), axis=1)
    xn = h * lax.rsqrt(jnp.mean(h * h, axis=-1, keepdims=True) + NORM_EPS) * nple_ref[...]
    gate = jax.nn.sigmoid(jnp.dot(xn.astype(BF16), wg_ref[...], preferred_element_type=F32))
    h = h + gate * jnp.dot(pe_ref[...].astype(BF16), wp_ref[...], preferred_element_type=F32)
    if final:
        h = h * lax.rsqrt(jnp.mean(h * h, axis=-1, keepdims=True) + NORM_EPS) * nfin_ref[...]
    o_ref[...] = h


def _ple(h, slots, n_all, tok0, rw, pe, norm_ple, w_ple_gate, w_ple_proj, norm_final, final):
    n, d = h.shape
    tm = _tile(n, 256)
    assert TOP_K == 4 and n_all % tm == 0 and tok0 % tm == 0
    row = lambda c: pl.BlockSpec((tm, c), lambda i: (i, 0))
    full = lambda a: pl.BlockSpec(a.shape, lambda i: (0, 0))
    slot = lambda k: pl.BlockSpec((tm, d // 2), lambda i, t0=(k * n_all + tok0) // tm: (t0 + i, 0))
    g1 = norm_ple.reshape(1, d)
    g2 = norm_final.reshape(1, d)
    return pl.pallas_call(
        functools.partial(_ple_kernel, final=final),
        out_shape=jax.ShapeDtypeStruct((n, d), F32),
        grid=(n // tm,),
        in_specs=[row(d), slot(0), slot(1), slot(2), slot(3), row(LANES), row(pe.shape[1]),
                  full(g1), full(w_ple_gate), full(w_ple_proj), full(g2)],
        out_specs=row(d),
        compiler_params=_cparams(("parallel",)),
        name="ple",
    )(h, slots, slots, slots, slots, rw, pe, g1, w_ple_gate, w_ple_proj, g2)


def _rope_tables(pos):
    half = ROPE_DIM // 2
    inv_freq = ROPE_THETA ** (-2.0 * jnp.arange(half, dtype=F32) / ROPE_DIM)
    ang = pos.astype(F32)[:, None] * inv_freq
    cos, sin = jnp.cos(ang), jnp.sin(ang)
    t = pos.shape[0]
    one = jnp.ones((t, LANES - ROPE_DIM), F32)
    zero = jnp.zeros((t, LANES - ROPE_DIM), F32)
    zh = jnp.zeros((t, half), F32)
    c = jnp.concatenate([cos, cos, one], axis=1)
    s1 = jnp.concatenate([zh, sin, zero], axis=1)
    s2 = jnp.concatenate([-sin, zh, zero], axis=1)
    return c, s1, s2


def _pad_cols(w, mult):
    c = w.shape[1]
    cp = -(-c // mult) * mult
    return jnp.pad(w, ((0, 0), (0, cp - c)))


def _layer_weights(w_in, b_forget, d):
    h_f = d // (2 * HEAD_DIM)
    fw = h_f * HEAD_DIM
    dw = fw
    o = 0
    seg = {}
    for name, width in (("fq", fw), ("fk", fw), ("fv", fw), ("fl", h_f), ("dq", dw), ("dk", dw), ("dv", dw),
                        ("gates", 2 * d)):
        seg[name] = w_in[:, o:o + width]
        o += width
    assert o == w_in.shape[1]
    out = {k: v.astype(BF16) for k, v in seg.items() if k != "fl"}
    out["fl"] = _pad_cols(seg["fl"], LANES).astype(BF16)
    out["fl_bias"] = _pad_cols(b_forget.reshape(1, h_f), LANES)
    return out, h_f


def _project(x, norm_mix, pw, tables, h_f):
    qscale = LOG2E * HEAD_DIM ** -0.5
    xn = _rmsnorm(x, norm_mix, NORM_EPS)
    fq, = _proj(xn, pw["fq"], "scale_bf", scale=qscale)
    fk, fk_b = _proj(xn, pw["fk"], "f32_bf")
    fv, fv_b = _proj(xn, pw["fv"], "f32_bf")
    logf, = _proj(xn, pw["fl"], "logsig", bias=pw["fl_bias"], out_cols=h_f)
    dq, = _proj(xn, pw["dq"], "rope_bf", scale=qscale, tables=tables)
    dk, dk_b = _proj(xn, pw["dk"], "rope_f32_bf", tables=tables)
    dv, dv_b = _proj(xn, pw["dv"], "f32_bf")
    gates, = _proj(xn, pw["gates"], "sigmoid_bf")
    return dict(fq=fq, fk=fk, fk_b=fk_b, fv=fv, fv_b=fv_b, logf=logf, dq=dq, dk=dk, dk_b=dk_b,
                dv=dv, dv_b=dv_b, gates=gates)


def _cum_logf(logf_bth):
    b, s, h = logf_bth.shape
    spad = -(-s // 1024) * 1024
    x = jnp.pad(jnp.swapaxes(logf_bth, 1, 2), ((0, 0), (0, 0), (0, spad - s)))
    return _cumsum(x.reshape(b * h, spad)).reshape(b, h, 1, spad)


def _decay_columns(c):
    x = -LOG2E * c
    hi = _top_bits(x)
    mid = _top_bits(x - hi)
    lo = _top_bits(x - hi - mid)
    parts = jnp.stack([hi, mid, lo], axis=-1).astype(BF16)
    return jnp.pad(parts, ((0, 0), (0, 0), (0, 0), (0, LANES - 3)))


def kernel(x_prompt, x_sample, cache_fox_k, cache_fox_v, cache_fox_logf, cache_diff_k, cache_diff_v,
           p_prompt, p_sample, norm_mix, w_in, b_forget, lambda_q1, lambda_k1, lambda_q2, lambda_k2,
           diff_subln, w_o_fox, w_o_diff, w_out, norm_ffn, w_router, b_router, w_gate, b_gate,
           w_up, b_up, w_down, b_down, norm_ple, w_ple_gate, w_ple_proj, norm_final):
    depth = w_in.shape[0]
    bp, tp, d = x_prompt.shape
    bs, ts, _ = x_sample.shape
    past = cache_fox_k.shape[2]
    n_p, n_s = bp * tp, bs * ts
    n_experts = w_router.shape[-1]
    tb = 512

    h_p = x_prompt.reshape(n_p, d)
    h_s = x_sample.reshape(n_s, d)
    tab_p = _rope_tables(jnp.arange(tp, dtype=jnp.int32))
    tab_s = tuple(jnp.tile(t, (bs, 1)) for t in _rope_tables(past + jnp.arange(ts, dtype=jnp.int32)))
    st_p, st_s = [], []
    for i in range(depth):
        lam_init = 0.8 - 0.6 * math.exp(-0.3 * i)
        lam = (jnp.exp(jnp.sum(lambda_q1[i].astype(F32) * lambda_k1[i].astype(F32)))
               - jnp.exp(jnp.sum(lambda_q2[i].astype(F32) * lambda_k2[i].astype(F32)))
               + lam_init).reshape(1).astype(F32)
        out_scale = 1.0 - lam_init
        pw, h_f = _layer_weights(w_in[i], b_forget[i], d)
        h_d = h_f // 2
        subln = diff_subln[i].reshape(1, 2 * HEAD_DIM)
        wof, wod, wo = w_o_fox[i].astype(BF16), w_o_diff[i].astype(BF16), w_out[i].astype(BF16)
        wg, wu, wd = w_gate[i].astype(BF16), w_up[i].astype(BF16), w_down[i].astype(BF16)
        wpg, wpp = w_ple_gate[i].astype(BF16), w_ple_proj[i].astype(BF16)

        pr = _project(h_p, norm_mix[i], pw, tab_p, h_f)
        c_p = _cum_logf(pr["logf"].reshape(bp, tp, h_f))
        o_fox_p = _fox_prompt(pr["fq"], pr["fk_b"], pr["fv_b"], _decay_columns(c_p[:, :, 0, :tp]), bp, tp)
        o_diff_p = _diff_prompt(lam, pr["dq"], pr["dk_b"], pr["dv_b"], subln, bp, tp, out_scale)
        h1_p, xn_p, ridx_p, rw_p = _mix(_merge(o_fox_p, o_diff_p, pr["gates"], wof, wod), h_p, wo,
                                        norm_ffn[i], w_router[i], b_router[i])

        sr = _project(h_s, norm_mix[i], pw, tab_s, h_f)
        logf_all = jnp.concatenate([cache_fox_logf[i].astype(F32), sr["logf"].reshape(bs, ts, h_f)], axis=1)
        c_s = _cum_logf(logf_all)
        o_fox_s = _fox_decode(sr["fq"], cache_fox_k[i].reshape(bs * past, -1), cache_fox_v[i].reshape(bs * past, -1),
                              sr["fk_b"], sr["fv_b"], c_s, bs, ts, past)
        o_diff_s = _diff_decode(lam, sr["dq"], cache_diff_k[i].reshape(bs * past, -1),
                                cache_diff_v[i].reshape(bs * past, -1), sr["dk_b"], sr["dv_b"], subln,
                                bs, ts, past, out_scale)
        h1_s, xn_s, ridx_s, rw_s = _mix(_merge(o_fox_s, o_diff_s, sr["gates"], wof, wod), h_s, wo,
                                        norm_ffn[i], w_router[i], b_router[i])

        xn_all = jnp.concatenate([xn_p, xn_s], axis=0)
        top_idx = jnp.concatenate([ridx_p[:, :TOP_K], ridx_s[:, :TOP_K]], axis=0)
        block_exp, nact, row_tok, row_slot = _routing(top_idx, n_experts, tb)
        n_all = n_p + n_s
        slots = _moe(block_exp, nact, row_tok, row_slot, xn_all, wg, b_gate[i], wu, b_up[i], wd,
                     b_down[i], n_all * TOP_K)

        last = i == depth - 1
        h_p = _ple(h1_p, slots, n_all, 0, rw_p, p_prompt[i].reshape(n_p, -1), norm_ple[i], wpg, wpp,
                   norm_final, last)
        h_s = _ple(h1_s, slots, n_all, n_p, rw_s, p_sample[i].reshape(n_s, -1), norm_ple[i], wpg, wpp,
                   norm_final, last)

        st_p.append((pr["fk"].reshape(bp, tp, h_f, HEAD_DIM), pr["fv"].reshape(bp, tp, h_f, HEAD_DIM),
                     pr["logf"].reshape(bp, tp, h_f), pr["dk"].reshape(bp, tp, h_d, 2 * HEAD_DIM),
                     pr["dv"].reshape(bp, tp, h_d, 2 * HEAD_DIM)))
        st_s.append((sr["fk"].reshape(bs, ts, h_f, HEAD_DIM), sr["fv"].reshape(bs, ts, h_f, HEAD_DIM),
                     sr["logf"].reshape(bs, ts, h_f), sr["dk"].reshape(bs, ts, h_d, 2 * HEAD_DIM),
                     sr["dv"].reshape(bs, ts, h_d, 2 * HEAD_DIM)))

    y_prompt = h_p.reshape(bp, tp, d)
    y_sample = h_s.reshape(bs, ts, d)
    outs_p = [jnp.stack([s[j] for s in st_p]) for j in range(5)]
    outs_s = [jnp.stack([s[j] for s in st_s]) for j in range(5)]
    return (y_prompt, y_sample, *outs_p, *outs_s)
```

```python
import functools
import math

import jax
import jax.numpy as jnp
from jax import lax
from jax.experimental import pallas as pl
from jax.experimental.pallas import tpu as pltpu

F32 = jnp.float32
BF16 = jnp.bfloat16

HEAD_DIM = 128
CHUNK = 64
CHUNK_SHIFT = 6
assert 1 << CHUNK_SHIFT == CHUNK
ROPE_DIM = HEAD_DIM // 4
ROPE_THETA = 500000.0
TOP_K = 4
SWIGLU_LIMIT = 7.0
SWIGLU_ALPHA = 1.702
NORM_EPS = 1e-6
SUBLN_EPS = 1e-5
LOG2E = 1.4426950408889634
NEG = -1e30
LANES = 128
VMEM_LIMIT = 56 * 1024 * 1024


def _cparams(sem):
    return pltpu.CompilerParams(dimension_semantics=sem, vmem_limit_bytes=VMEM_LIMIT)


def _top_bits(v):
    bits = lax.bitcast_convert_type(v, jnp.uint32) & jnp.uint32(0xFFFF0000)
    return lax.bitcast_convert_type(bits, F32)


def _pack_halves(x):
    half = x.shape[1] // 2

    def rounded(v):
        bits = lax.bitcast_convert_type(v, jnp.uint32)
        return bits + jnp.uint32(0x7FFF) + ((bits >> 16) & jnp.uint32(1))

    return (rounded(x[:, half:]) & jnp.uint32(0xFFFF0000)) | (rounded(x[:, :half]) >> 16)


def _unpack_halves(w):
    lo = lax.bitcast_convert_type(w << 16, F32)
    hi = lax.bitcast_convert_type(w & jnp.uint32(0xFFFF0000), F32)
    return lo, hi


def _tile(n, pref):
    t = min(n, pref)
    assert n % t == 0, (n, pref)
    return t


def _rmsnorm_kernel(x_ref, g_ref, o_ref, *, eps):
    x = x_ref[...]
    y = x * lax.rsqrt(jnp.mean(x * x, axis=-1, keepdims=True) + eps) * g_ref[...]
    o_ref[...] = y.astype(o_ref.dtype)


def _rmsnorm(x, gain, eps):
    n, d = x.shape
    tm = _tile(n, 1024)
    return pl.pallas_call(
        functools.partial(_rmsnorm_kernel, eps=eps),
        out_shape=jax.ShapeDtypeStruct((n, d), BF16),
        grid=(n // tm,),
        in_specs=[pl.BlockSpec((tm, d), lambda i: (i, 0)),
                  pl.BlockSpec((1, d), lambda i: (0, 0))],
        out_specs=pl.BlockSpec((tm, d), lambda i: (i, 0)),
        compiler_params=_cparams(("parallel",)),
        name="rmsnorm",
    )(x, gain.reshape(1, d))


def _rope_slab(x, c, s1, s2):
    return x * c + pltpu.roll(x, ROPE_DIM // 2, 1) * s1 + pltpu.roll(x, LANES - ROPE_DIM // 2, 1) * s2


def _proj_kernel(*refs, kind, scale):
    xn_ref, w_ref = refs[0], refs[1]
    acc = jnp.dot(xn_ref[...], w_ref[...], preferred_element_type=F32)
    if kind == "scale_bf":
        refs[2][...] = (acc * scale).astype(BF16)
    elif kind == "f32_bf":
        hd = refs[2].shape[2]
        for h in range(refs[2].shape[1]):
            refs[2][:, h, :] = acc[:, h * hd:(h + 1) * hd]
        refs[3][...] = acc.astype(BF16)
    elif kind == "sigmoid_bf":
        refs[2][...] = jax.nn.sigmoid(acc).astype(BF16)
    elif kind == "logsig":
        z = acc + refs[2][...]
        val = jnp.minimum(z, 0.0) - jnp.log1p(jnp.exp(-jnp.abs(z)))
        refs[3][...] = val[:, :refs[3].shape[1]]
    elif kind in ("rope_bf", "rope_f32_bf"):
        c, s1, s2 = refs[2][...], refs[3][...], refs[4][...]
        for j in range(acc.shape[1] // LANES):
            sl = slice(j * LANES, (j + 1) * LANES)
            r = _rope_slab(acc[:, sl], c, s1, s2)
            if kind == "rope_bf":
                refs[5][:, sl] = (r * scale).astype(BF16)
            else:
                per_head = refs[5].shape[2] // LANES
                refs[5][:, j // per_head, (j % per_head) * LANES:(j % per_head + 1) * LANES] = r
                refs[6][:, sl] = r.astype(BF16)
    else:
        raise ValueError(kind)


def _proj(xn, w, kind, *, scale=1.0, tables=None, bias=None, out_cols=None, head_dim=None):
    n, d = xn.shape
    c = w.shape[1]
    tm = _tile(n, 1024)
    tn = _tile(c, 1024)
    grid = (c // tn, n // tm)
    in_specs = [pl.BlockSpec((tm, d), lambda j, i: (i, 0)),
                pl.BlockSpec((d, tn), lambda j, i: (0, j))]
    args = [xn, w]
    blk = pl.BlockSpec((tm, tn), lambda j, i: (i, j))
    if head_dim is not None:
        assert tn == c and c % head_dim == 0 and head_dim % LANES == 0
        heads_shape = jax.ShapeDtypeStruct((n, c // head_dim, head_dim), F32)
        heads_blk = pl.BlockSpec((tm, c // head_dim, head_dim), lambda j, i: (i, 0, 0))
    if kind == "scale_bf" or kind == "sigmoid_bf":
        out_shape = [jax.ShapeDtypeStruct((n, c), BF16)]
        out_specs = [blk]
    elif kind == "f32_bf":
        out_shape = [heads_shape, jax.ShapeDtypeStruct((n, c), BF16)]
        out_specs = [heads_blk, blk]
    elif kind == "logsig":
        in_specs.append(pl.BlockSpec((1, tn), lambda j, i: (0, j)))
        args.append(bias)
        out_shape = [jax.ShapeDtypeStruct((n, out_cols), F32)]
        out_specs = [pl.BlockSpec((tm, out_cols), lambda j, i: (i, 0))]
    else:
        nt = tables[0].shape[0] // tm
        for t in tables:
            in_specs.append(pl.BlockSpec((tm, LANES), lambda j, i, nt=nt: (i % nt, 0)))
            args.append(t)
        if kind == "rope_bf":
            out_shape = [jax.ShapeDtypeStruct((n, c), BF16)]
            out_specs = [blk]
        else:
            out_shape = [heads_shape, jax.ShapeDtypeStruct((n, c), BF16)]
            out_specs = [heads_blk, blk]
    return pl.pallas_call(
        functools.partial(_proj_kernel, kind=kind, scale=scale),
        out_shape=out_shape,
        grid=grid,
        in_specs=in_specs,
        out_specs=out_specs,
        compiler_params=_cparams(("parallel", "parallel")),
        name="proj_" + kind,
    )(*args)


def _cumsum_kernel(x_ref, o_ref):
    x = x_ref[...]
    r = x.shape[0]
    li = lax.broadcasted_iota(jnp.int32, (LANES, LANES), 0)
    lj = lax.broadcasted_iota(jnp.int32, (LANES, LANES), 1)
    upper = (li <= lj).astype(F32)
    within = jnp.dot(x, upper, preferred_element_type=F32, precision=lax.Precision.HIGHEST)
    tot = jnp.broadcast_to(within[:, LANES - 1:LANES], (r, LANES))
    ri = lax.broadcasted_iota(jnp.int32, (r, r), 0)
    rj = lax.broadcasted_iota(jnp.int32, (r, r), 1)
    strict = (rj < ri).astype(F32)
    off = jnp.dot(strict, tot, preferred_element_type=F32, precision=lax.Precision.HIGHEST)
    o_ref[...] = within + off


def _cumsum(x):
    g, s = x.shape
    r = s // LANES
    out = pl.pallas_call(
        _cumsum_kernel,
        out_shape=jax.ShapeDtypeStruct((g, r, LANES), F32),
        grid=(g,),
        in_specs=[pl.BlockSpec((None, r, LANES), lambda i: (i, 0, 0))],
        out_specs=pl.BlockSpec((None, r, LANES), lambda i: (i, 0, 0)),
        compiler_params=_cparams(("parallel",)),
        name="cumsum",
    )(x.reshape(g, r, LANES))
    return out.reshape(g, s)


def _qk(q, k):
    return lax.dot_general(q, k, (((1,), (1,)), ((), ())), preferred_element_type=F32)


def _lane_chunks(s):
    return [s[:, j * LANES:(j + 1) * LANES] for j in range(s.shape[1] // LANES)]


def _running_max(sj, m_s):
    smax = sj[0]
    for x in sj[1:]:
        smax = jnp.maximum(smax, x)
    m_prev = m_s[...]
    m_new = jnp.maximum(m_prev, jnp.max(smax, axis=-1, keepdims=True))
    m_s[...] = m_new
    return m_new, jnp.exp2(m_prev - m_new)


def _causal_tiles(t):
    tq = _tile(t, 1024)
    return tq, tq


def _fox_kernel(q_ref, k_ref, v_ref, ca_ref, o_ref, m_s, acc_s, s_s, *, tq, tk):
    qi = pl.program_id(2)
    lane = lax.broadcasted_iota(jnp.int32, (tq, LANES), 1)
    q_aug = jnp.concatenate([q_ref[...], jnp.where(lane < 3, 1.0, 0.0).astype(BF16)], axis=1)
    ones = jnp.ones((tk, LANES), BF16)
    m_s[...] = jnp.full(m_s.shape, NEG, F32)
    acc_s[...] = jnp.zeros(acc_s.shape, F32)

    def scores(kj):
        off = pl.multiple_of(kj * tk, tk)
        k_aug = jnp.concatenate([k_ref[pl.ds(off, tk), :], ca_ref[pl.ds(off, tk), :]], axis=1)
        return _qk(q_aug, k_aug)

    def consume(s, kj):
        off = pl.multiple_of(kj * tk, tk)
        v_aug = jnp.concatenate([v_ref[pl.ds(off, tk), :], ones], axis=1)
        sj = _lane_chunks(s)
        m_new, alpha = _running_max(sj, m_s)
        p = jnp.concatenate([jnp.exp2(x - m_new).astype(BF16) for x in sj], axis=1)
        pv = jnp.dot(p, v_aug, preferred_element_type=F32)
        acc_s[...] = jnp.concatenate([alpha, alpha], axis=1) * acc_s[...] + pv

    nfull = (qi * tq) // tk
    s_s[...] = scores(0)

    def body(kj, carry):
        s = s_s[...]
        s_next = scores(kj + 1)
        consume(s, kj)
        s_s[...] = s_next
        return carry

    lax.fori_loop(0, nfull, body, 0)
    row = qi * tq + lax.broadcasted_iota(jnp.int32, (tq, tk), 0)
    col = lax.broadcasted_iota(jnp.int32, (tq, tk), 1)
    for d in range(tq // tk):
        kj = nfull + d
        s = s_s[...] if d == 0 else scores(kj)
        consume(jnp.where(col + kj * tk <= row, s, NEG), kj)
    acc = acc_s[...]
    o_ref[...] = (acc[:, :HEAD_DIM] / acc[:, HEAD_DIM:]).astype(o_ref.dtype)


def _fox_prompt(q, k, v, ca, b, t):
    n, w = q.shape
    h = w // HEAD_DIM
    tq, tk = _causal_tiles(t)
    nq = t // tq
    return pl.pallas_call(
        functools.partial(_fox_kernel, tq=tq, tk=tk),
        out_shape=jax.ShapeDtypeStruct((n, w), BF16),
        grid=(b, h, nq),
        in_specs=[pl.BlockSpec((tq, HEAD_DIM), lambda bi, hi, qi: (bi * nq + qi, hi)),
                  pl.BlockSpec((t, HEAD_DIM), lambda bi, hi, qi: (bi, hi)),
                  pl.BlockSpec((t, HEAD_DIM), lambda bi, hi, qi: (bi, hi)),
                  pl.BlockSpec((None, None, t, LANES), lambda bi, hi, qi: (bi, hi, 0, 0))],
        out_specs=pl.BlockSpec((tq, HEAD_DIM), lambda bi, hi, qi: (bi * nq + qi, hi)),
        scratch_shapes=[pltpu.VMEM((tq, LANES), F32), pltpu.VMEM((tq, 2 * HEAD_DIM), F32),
                        pltpu.VMEM((tq, tk), F32)],
        compiler_params=_cparams(("parallel", "parallel", "arbitrary")),
        name="fox_prompt",
    )(q, k, v, ca)


def _diff_finish(acc1, l1, acc2, l2, lam, gain, out_scale):
    o = acc1 / l1 - lam * (acc2 / l2)
    o = o * lax.rsqrt(jnp.mean(o * o, axis=-1, keepdims=True) + SUBLN_EPS) * gain
    return o * out_scale


def _diff_kernel(lam_ref, q_ref, k_ref, v_ref, g_ref, o_ref, m1, l1, a1, s1, m2, l2, a2, s2, *, tq, tk, out_scale):
    qi = pl.program_id(2)
    state = ((0, m1, l1, a1, s1), (HEAD_DIM, m2, l2, a2, s2))
    for _, m_s, l_s, a_s, _ in state:
        m_s[...] = jnp.full(m_s.shape, NEG, F32)
        l_s[...] = jnp.zeros(l_s.shape, F32)
        a_s[...] = jnp.zeros(a_s.shape, F32)

    def scores(kj, lo):
        off = pl.multiple_of(kj * tk, tk)
        return _qk(q_ref[:, lo:lo + HEAD_DIM], k_ref[pl.ds(off, tk), lo:lo + HEAD_DIM])

    def consume(s, v, m_s, l_s, a_s):
        sj = _lane_chunks(s)
        m_new, alpha = _running_max(sj, m_s)
        pj = [jnp.exp2(x - m_new) for x in sj]
        psum = pj[0]
        for x in pj[1:]:
            psum = psum + x
        p = jnp.concatenate([x.astype(BF16) for x in pj], axis=1)
        l_s[...] = alpha * l_s[...] + psum
        a_s[...] = jnp.concatenate([alpha, alpha], axis=1) * a_s[...] + jnp.dot(p, v, preferred_element_type=F32)

    nfull = (qi * tq) // tk
    for lo, _, _, _, s_s in state:
        s_s[...] = scores(0, lo)

    def body(kj, carry):
        off = pl.multiple_of(kj * tk, tk)
        v = v_ref[pl.ds(off, tk), :]
        for lo, m_s, l_s, a_s, s_s in state:
            s = s_s[...]
            s_next = scores(kj + 1, lo)
            consume(s, v, m_s, l_s, a_s)
            s_s[...] = s_next
        return carry

    lax.fori_loop(0, nfull, body, 0)
    row = qi * tq + lax.broadcasted_iota(jnp.int32, (tq, tk), 0)
    col = lax.broadcasted_iota(jnp.int32, (tq, tk), 1)
    for d in range(tq // tk):
        kj = nfull + d
        off = pl.multiple_of(kj * tk, tk)
        v = v_ref[pl.ds(off, tk), :]
        keep = ((col + kj * tk) >> CHUNK_SHIFT) <= (row >> CHUNK_SHIFT)
        for lo, m_s, l_s, a_s, s_s in state:
            s = s_s[...] if d == 0 else scores(kj, lo)
            consume(jnp.where(keep, s, NEG), v, m_s, l_s, a_s)
    o = _diff_finish(a1[...], jnp.sum(l1[...], axis=-1, keepdims=True),
                     a2[...], jnp.sum(l2[...], axis=-1, keepdims=True), lam_ref[0], g_ref[...], out_scale)
    o_ref[...] = o.astype(o_ref.dtype)


def _diff_prompt(lam, q, k, v, gain, b, t, out_scale):
    n, w = q.shape
    hw = 2 * HEAD_DIM
    h = w // hw
    tq, tk = _causal_tiles(t)
    assert tq % CHUNK == 0
    nq = t // tq
    per_map = [pltpu.VMEM((tq, LANES), F32), pltpu.VMEM((tq, LANES), F32), pltpu.VMEM((tq, hw), F32),
               pltpu.VMEM((tq, tk), F32)]
    return pl.pallas_call(
        functools.partial(_diff_kernel, tq=tq, tk=tk, out_scale=out_scale),
        out_shape=jax.ShapeDtypeStruct((n, w), BF16),
        grid_spec=pltpu.PrefetchScalarGridSpec(
            num_scalar_prefetch=1,
            grid=(b, h, nq),
            in_specs=[pl.BlockSpec((tq, hw), lambda bi, hi, qi, lam: (bi * nq + qi, hi)),
                      pl.BlockSpec((t, hw), lambda bi, hi, qi, lam: (bi, hi)),
                      pl.BlockSpec((t, hw), lambda bi, hi, qi, lam: (bi, hi)),
                      pl.BlockSpec((1, hw), lambda bi, hi, qi, lam: (0, 0))],
            out_specs=pl.BlockSpec((tq, hw), lambda bi, hi, qi, lam: (bi * nq + qi, hi)),
            scratch_shapes=per_map + per_map),
        compiler_params=_cparams(("parallel", "parallel", "arbitrary")),
        name="diff_prompt",
    )(lam, q, k, v, gain)


def _fox_decode_kernel(q_ref, kc_ref, vc_ref, kn_ref, vn_ref, c_ref, o_ref, *, past, tq):
    q = q_ref[...]
    s_c = _qk(q, kc_ref[...].astype(BF16)) - c_ref[:, :past] * LOG2E
    c_new = c_ref[:, past:past + LANES]
    s_n = _qk(q, kn_ref[...]) - c_new[:, :tq] * LOG2E
    row = lax.broadcasted_iota(jnp.int32, (tq, tq), 0)
    col = lax.broadcasted_iota(jnp.int32, (tq, tq), 1)
    s_n = jnp.where(col <= row, s_n, NEG)
    m = jnp.maximum(jnp.max(s_c, axis=-1, keepdims=True), jnp.max(s_n, axis=-1, keepdims=True))
    p_c = jnp.exp2(s_c - m)
    p_n = jnp.exp2(s_n - m)
    l = jnp.sum(p_c, axis=-1, keepdims=True) + jnp.sum(p_n, axis=-1, keepdims=True)
    o = (jnp.dot(p_c.astype(BF16), vc_ref[...].astype(BF16), preferred_element_type=F32)
         + jnp.dot(p_n.astype(BF16), vn_ref[...], preferred_element_type=F32))
    o_ref[...] = (o / l).astype(o_ref.dtype)


def _fox_decode(q, kc, vc, kn, vn, c, b, t, past):
    n, w = q.shape
    h = w // HEAD_DIM
    assert past % LANES == 0 and t <= LANES
    spad = c.shape[-1]
    new = pl.BlockSpec((t, HEAD_DIM), lambda bi, hi: (bi, hi))
    old = pl.BlockSpec((past, HEAD_DIM), lambda bi, hi: (bi, hi))
    return pl.pallas_call(
        functools.partial(_fox_decode_kernel, past=past, tq=t),
        out_shape=jax.ShapeDtypeStruct((n, w), BF16),
        grid=(b, h),
        in_specs=[new, old, old, new, new,
                  pl.BlockSpec((None, None, 1, spad), lambda bi, hi: (bi, hi, 0, 0))],
        out_specs=new,
        compiler_params=_cparams(("parallel", "parallel")),
        name="fox_decode",
    )(q, kc, vc, kn, vn, c)


def _diff_decode_kernel(lam_ref, q_ref, kc_ref, vc_ref, kn_ref, vn_ref, g_ref, o_ref, *, past, tq, out_scale):
    vc = vc_ref[...].astype(BF16)
    vn = vn_ref[...]
    row = past + lax.broadcasted_iota(jnp.int32, (tq, tq), 0)
    col = past + lax.broadcasted_iota(jnp.int32, (tq, tq), 1)
    keep = (col >> CHUNK_SHIFT) <= (row >> CHUNK_SHIFT)
    accs, ls = [], []
    for lo in (0, HEAD_DIM):
        q = q_ref[:, lo:lo + HEAD_DIM]
        s_c = _qk(q, kc_ref[:, lo:lo + HEAD_DIM].astype(BF16))
        s_n = jnp.where(keep, _qk(q, kn_ref[:, lo:lo + HEAD_DIM]), NEG)
        m = jnp.maximum(jnp.max(s_c, axis=-1, keepdims=True), jnp.max(s_n, axis=-1, keepdims=True))
        p_c = jnp.exp2(s_c - m)
        p_n = jnp.exp2(s_n - m)
        ls.append(jnp.sum(p_c, axis=-1, keepdims=True) + jnp.sum(p_n, axis=-1, keepdims=True))
        accs.append(jnp.dot(p_c.astype(BF16), vc, preferred_element_type=F32)
                    + jnp.dot(p_n.astype(BF16), vn, preferred_element_type=F32))
    o = _diff_finish(accs[0], ls[0], accs[1], ls[1], lam_ref[0], g_ref[...], out_scale)
    o_ref[...] = o.astype(o_ref.dtype)


def _diff_decode(lam, q, kc, vc, kn, vn, gain, b, t, past, out_scale):
    n, w = q.shape
    hw = 2 * HEAD_DIM
    h = w // hw
    assert (past - 1) // CHUNK <= past // CHUNK
    new = pl.BlockSpec((t, hw), lambda bi, hi, lam: (bi, hi))
    old = pl.BlockSpec((past, hw), lambda bi, hi, lam: (bi, hi))
    return pl.pallas_call(
        functools.partial(_diff_decode_kernel, past=past, tq=t, out_scale=out_scale),
        out_shape=jax.ShapeDtypeStruct((n, w), BF16),
        grid_spec=pltpu.PrefetchScalarGridSpec(
            num_scalar_prefetch=1,
            grid=(b, h),
            in_specs=[new, old, old, new, new,
                      pl.BlockSpec((1, hw), lambda bi, hi, lam: (0, 0))],
            out_specs=new),
        compiler_params=_cparams(("parallel", "parallel")),
        name="diff_decode",
    )(lam, q, kc, vc, kn, vn, gain)


def _merge_kernel(of_ref, od_ref, gf_ref, gd_ref, wof_ref, wod_ref, o_ref):
    yf = jnp.dot(of_ref[...], wof_ref[...], preferred_element_type=F32)
    yd = jnp.dot(od_ref[...], wod_ref[...], preferred_element_type=F32)
    o_ref[...] = (gf_ref[...].astype(F32) * yf + gd_ref[...].astype(F32) * yd).astype(o_ref.dtype)


def _merge(o_fox, o_diff, gates, w_o_fox, w_o_diff):
    n = o_fox.shape[0]
    d = w_o_fox.shape[1]
    tm = _tile(n, 512)
    row = lambda c: pl.BlockSpec((tm, c), lambda i: (i, 0))
    full = lambda a: pl.BlockSpec(a.shape, lambda i: (0, 0))
    return pl.pallas_call(
        _merge_kernel,
        out_shape=jax.ShapeDtypeStruct((n, d), BF16),
        grid=(n // tm,),
        in_specs=[row(o_fox.shape[1]), row(o_diff.shape[1]),
                  pl.BlockSpec((tm, d), lambda i: (i, 0)), pl.BlockSpec((tm, d), lambda i: (i, 1)),
                  full(w_o_fox), full(w_o_diff)],
        out_specs=row(d),
        compiler_params=_cparams(("parallel",)),
        name="merge",
    )(o_fox, o_diff, gates, gates, w_o_fox, w_o_diff)


def _mix_kernel(m_ref, x_ref, wout_ref, nrm_ref, wr_ref, br_ref, h_ref, xn_ref, ridx_ref, rw_ref, *, n_experts):
    h = x_ref[...] + jnp.dot(m_ref[...], wout_ref[...], preferred_element_type=F32)
    h_ref[...] = h
    xf = h * lax.rsqrt(jnp.mean(h * h, axis=-1, keepdims=True) + NORM_EPS) * nrm_ref[...]
    xn_ref[...] = _pack_halves(xf)
    xh = xf.astype(BF16)
    xl = (xf - xh.astype(F32)).astype(BF16)
    wr = wr_ref[...]
    both = jnp.dot(xh, wr, preferred_element_type=F32)
    logits = (both[:, :LANES] + both[:, LANES:] + jnp.dot(xl, wr[:, :LANES], preferred_element_type=F32)
              + br_ref[...])
    lane = lax.broadcasted_iota(jnp.int32, logits.shape, 1).astype(F32)
    cur = jnp.where(lane < n_experts, logits, -jnp.inf)
    vals, idxs = [], []
    for _ in range(TOP_K):
        mk = jnp.max(cur, axis=-1, keepdims=True)
        ik = jnp.min(jnp.where(cur == mk, lane, float(LANES)), axis=-1, keepdims=True)
        cur = jnp.where(lane == ik, -jnp.inf, cur)
        vals.append(mk)
        idxs.append(ik)
    es = [jnp.exp(vk - vals[0]) for vk in vals]
    den = es[0] + es[1] + es[2] + es[3]
    ridx = jnp.zeros(logits.shape, F32)
    rw = jnp.zeros(logits.shape, F32)
    for k in range(TOP_K):
        ridx = jnp.where(lane == float(k), idxs[k], ridx)
        rw = jnp.where(lane == float(k), es[k] / den, rw)
    ridx_ref[...] = ridx.astype(jnp.int32)
    rw_ref[...] = rw


def _mix(merged, x, w_out, norm_ffn, w_router, b_router):
    n, d = x.shape
    e = w_router.shape[1]
    tm = _tile(n, 512)
    w_pad = _pad_cols(w_router, LANES)
    w_hi = _top_bits(w_pad)
    wr = jnp.concatenate([w_hi, _top_bits(w_pad - w_hi)], axis=1).astype(BF16)
    br = _pad_cols(b_router.reshape(1, e), LANES)
    row = lambda c: pl.BlockSpec((tm, c), lambda i: (i, 0))
    full = lambda a: pl.BlockSpec(a.shape, lambda i: (0, 0), pipeline_mode=pl.Buffered(1))
    args = [merged, x, w_out, norm_ffn.reshape(1, d), wr, br]
    in_specs = [row(d), row(d), full(w_out), full(args[3]), full(wr), full(br)]
    return pl.pallas_call(
        functools.partial(_mix_kernel, n_experts=e),
        out_shape=[jax.ShapeDtypeStruct((n, d), F32), jax.ShapeDtypeStruct((n, d // 2), jnp.uint32),
                   jax.ShapeDtypeStruct((n, LANES), jnp.int32), jax.ShapeDtypeStruct((n, LANES), F32)],
        grid=(n // tm,),
        in_specs=in_specs,
        out_specs=[row(d), row(d // 2), row(LANES), row(LANES)],
        compiler_params=_cparams(("parallel",)),
        name="mix",
    )(*args)


def _moe_kernel(be_ref, nact_ref, nv_ref, tok_hbm, slot_hbm, x_hbm, wg_ref, bg_ref, wu_ref, bu_ref, wd_ref,
                bd_ref, out_hbm, xbuf, xb, tok_s, slot_s, acc, ostage, gsem, ssem, isem, *, tb, nft, n_real):
    blk = pl.program_id(0)
    ft = pl.program_id(1)
    nact = nact_ref[0]
    active = blk < nact
    cur = lax.rem(blk, 2)
    nxt = 1 - cur
    per_step = tb // nft

    def tok_copy(b, s):
        return pltpu.make_async_copy(tok_hbm.at[pl.ds(b, 1)], tok_s.at[pl.ds(s, 1)], isem.at[0])

    def slot_copy(b, s):
        return pltpu.make_async_copy(slot_hbm.at[pl.ds(b, 1)], slot_s.at[pl.ds(s, 1)], isem.at[1])

    def gather_row(s, r):
        t = tok_s[s, r]
        pltpu.make_async_copy(x_hbm.at[pl.ds(t, 1)], xbuf.at[s, pl.ds(r, 1)], gsem.at[s]).start()

    def scatter_row(s, r):
        d = slot_s[s, r]
        pltpu.make_async_copy(ostage.at[s, pl.ds(r, 1)], out_hbm.at[pl.ds(d, 1)], ssem.at[s]).start(priority=1)

    def wait_gather(s):
        pltpu.make_async_copy(xbuf.at[s], xbuf.at[s], gsem.at[s]).wait()

    def wait_scatter(s):
        pltpu.make_async_copy(ostage.at[s], ostage.at[s], ssem.at[s]).wait()

    def loop_rows(fn, s):
        def body(r, carry):
            fn(s, r)
            return carry
        lax.fori_loop(0, tb, body, 0)

    @pl.when((blk == 0) & (ft == 0))
    def _():
        ostage[1] = jnp.zeros(ostage.shape[1:], ostage.dtype)

        def spare(r, carry):
            slot_s[1, r] = n_real + r
            return carry
        lax.fori_loop(0, tb, spare, 0)
        for b in range(2):
            c = tok_copy(b, b)
            c.start()
            c.wait()
        loop_rows(gather_row, 0)

    @pl.when((ft == 0) & (blk >= 1) & (blk <= nact))
    def _():
        tok_copy(blk + 1, nxt).wait()
        slot_copy(blk - 1, nxt).wait()

    @pl.when((ft == 0) & (blk <= nact))
    def _():
        wait_gather(cur)

    @pl.when((ft == 0) & (blk >= 1) & (blk - 2 < nact))
    def _():
        wait_scatter(cur)

    @pl.when((ft == 0) & active)
    def _():
        tok_copy(blk + 2, cur).start()
        slot_copy(blk, cur).start()
        lo, hi = _unpack_halves(xbuf[cur])
        xb[...] = jnp.concatenate([lo.astype(BF16), hi.astype(BF16)], axis=1)
        acc[...] = jnp.broadcast_to(bd_ref[...], acc.shape)

    @pl.when((ft == 0) & (blk == nact) & (blk >= 1))
    def _():
        loop_rows(scatter_row, nxt)

    def compute(rows):
        base = ft * per_step
        for j in range(per_step):
            gather_row(nxt, base + j)
        for j in range(per_step):
            scatter_row(nxt, base + j)
        x = xb[:rows]
        g = jnp.dot(x, wg_ref[...], preferred_element_type=F32) + bg_ref[...]
        u = jnp.dot(x, wu_ref[...], preferred_element_type=F32) + bu_ref[...]
        g = jnp.minimum(g, SWIGLU_LIMIT)
        u = jnp.clip(u, -SWIGLU_LIMIT, SWIGLU_LIMIT)
        act = (u + 1.0) * (g * jax.nn.sigmoid(SWIGLU_ALPHA * g))
        acc[:rows] += jnp.dot(act.astype(BF16), wd_ref[...], preferred_element_type=F32)

    half_full = nv_ref[blk] <= tb // 2

    @pl.when(active & half_full)
    def _():
        compute(tb // 2)

    @pl.when(active & jnp.logical_not(half_full))
    def _():
        compute(tb)

    @pl.when(active & (ft == nft - 1))
    def _():
        ostage[cur] = _pack_halves(acc[...])


def _moe(block_exp, nact, nvalid, row_tok, row_slot, xn, w_gate, b_gate, w_up, b_up, w_down, b_down, n_real):
    nb, tb = row_tok.shape
    e, d, f = w_gate.shape
    tf = _tile(f, 1024)
    nft = f // tf
    d2 = d // 2
    assert nft >= 2 and tb % nft == 0 and xn.shape[1] == d2

    def ftile(b, j, be, na, nv):
        return jnp.where(b < na[0], j, nft - 1)

    any_spec = pl.BlockSpec(memory_space=pl.ANY)
    in_specs = [any_spec, any_spec, any_spec,
                pl.BlockSpec((None, d, tf), lambda b, j, be, na, nv: (be[b], 0, ftile(b, j, be, na, nv))),
                pl.BlockSpec((None, 1, tf), lambda b, j, be, na, nv: (be[b], 0, ftile(b, j, be, na, nv))),
                pl.BlockSpec((None, d, tf), lambda b, j, be, na, nv: (be[b], 0, ftile(b, j, be, na, nv))),
                pl.BlockSpec((None, 1, tf), lambda b, j, be, na, nv: (be[b], 0, ftile(b, j, be, na, nv))),
                pl.BlockSpec((None, tf, d), lambda b, j, be, na, nv: (be[b], ftile(b, j, be, na, nv), 0)),
                pl.BlockSpec((None, 1, d), lambda b, j, be, na, nv: (be[b], 0, 0))]
    return pl.pallas_call(
        functools.partial(_moe_kernel, tb=tb, nft=nft, n_real=n_real),
        out_shape=jax.ShapeDtypeStruct((n_real + tb, d2), jnp.uint32),
        grid_spec=pltpu.PrefetchScalarGridSpec(
            num_scalar_prefetch=3,
            grid=(nb, nft),
            in_specs=in_specs,
            out_specs=any_spec,
            scratch_shapes=[pltpu.VMEM((2, tb, d2), jnp.uint32), pltpu.VMEM((tb, d), BF16),
                            pltpu.SMEM((2, tb), jnp.int32), pltpu.SMEM((2, tb), jnp.int32),
                            pltpu.VMEM((tb, d), F32), pltpu.VMEM((2, tb, d2), jnp.uint32),
                            pltpu.SemaphoreType.DMA((2,)), pltpu.SemaphoreType.DMA((2,)),
                            pltpu.SemaphoreType.DMA((2,))]),
        compiler_params=_cparams(("arbitrary", "arbitrary")),
        name="moe",
    )(block_exp, nact, nvalid, row_tok, row_slot, xn, w_gate, b_gate.reshape(e, 1, f), w_up,
      b_up.reshape(e, 1, f), w_down, b_down.reshape(e, 1, d))


def _routing(top_idx, n_experts, tb):
    n = top_idx.shape[0]
    a = n * TOP_K
    expert = top_idx.reshape(-1)
    order = jnp.argsort(expert).astype(jnp.int32)
    counts = jnp.sum(expert[:, None] == jnp.arange(n_experts, dtype=jnp.int32)[None, :], axis=0, dtype=jnp.int32)
    padded = (counts + tb - 1) // tb * tb
    start = jnp.cumsum(counts) - counts
    pend = jnp.cumsum(padded)
    pstart = pend - padded
    nb = -(-a // tb) + n_experts + 1
    row0 = jnp.arange(nb, dtype=jnp.int32) * tb
    block_exp = jnp.minimum(jnp.sum(pend[None, :] <= row0[:, None], axis=1), n_experts - 1).astype(jnp.int32)
    nvalid = jnp.clip(pstart[block_exp] + counts[block_exp] - row0, 0, tb)
    src0 = jnp.clip(start[block_exp] + row0 - pstart[block_exp], 0, a)
    order_pad = jnp.concatenate([order, jnp.zeros((tb,), jnp.int32)])
    r = jnp.arange(tb, dtype=jnp.int32)[None, :]
    win = order_pad[src0[:, None] + r]
    valid = r < nvalid[:, None]
    tok = win // TOP_K
    row_tok = jnp.where(valid, tok, 0)
    row_slot = jnp.where(valid, (win % TOP_K) * n + tok, a + r)
    nact = (pend[-1] // tb).reshape(1).astype(jnp.int32)
    return block_exp, nact, nvalid.astype(jnp.int32), row_tok.astype(jnp.int32), row_slot.astype(jnp.int32)


def _ple_kernel(h_ref, s0_ref, s1_ref, s2_ref, s3_ref, rw_ref, pe_ref, nple_ref, wg_ref, wp_ref, nfin_ref,
                o_ref, *, final):
    h = h_ref[...]
    rw = rw_ref[...]
    for k, s_ref in enumerate((s0_ref, s1_ref, s2_ref, s3_ref)):
        h = h + rw[:, k:k + 1] * jnp.concatenate(_unpack_halves(s_ref[...]), axis=1)
    xn = h * lax.rsqrt(jnp.mean(h * h, axis=-1, keepdims=True) + NORM_EPS) * nple_ref[...]
    gate = jax.nn.sigmoid(jnp.dot(xn.astype(BF16), wg_ref[...], preferred_element_type=F32))
    h = h + gate * jnp.dot(pe_ref[...].astype(BF16), wp_ref[...], preferred_element_type=F32)
    if final:
        h = h * lax.rsqrt(jnp.mean(h * h, axis=-1, keepdims=True) + NORM_EPS) * nfin_ref[...]
    o_ref[...] = h


def _ple(h, slots, n_all, tok0, rw, pe, norm_ple, w_ple_gate, w_ple_proj, norm_final, final):
    n, d = h.shape
    tm = _tile(n, 256)
    assert TOP_K == 4 and n_all % tm == 0 and tok0 % tm == 0
    row = lambda c: pl.BlockSpec((tm, c), lambda i: (i, 0))
    full = lambda a: pl.BlockSpec(a.shape, lambda i: (0, 0))
    slot = lambda k: pl.BlockSpec((tm, d // 2), lambda i, t0=(k * n_all + tok0) // tm: (t0 + i, 0))
    g1 = norm_ple.reshape(1, d)
    g2 = norm_final.reshape(1, d)
    return pl.pallas_call(
        functools.partial(_ple_kernel, final=final),
        out_shape=jax.ShapeDtypeStruct((n, d), F32),
        grid=(n // tm,),
        in_specs=[row(d), slot(0), slot(1), slot(2), slot(3), row(LANES), row(pe.shape[1]),
                  full(g1), full(w_ple_gate), full(w_ple_proj), full(g2)],
        out_specs=row(d),
        compiler_params=_cparams(("parallel",)),
        name="ple",
    )(h, slots, slots, slots, slots, rw, pe, g1, w_ple_gate, w_ple_proj, g2)


def _rope_tables(pos):
    half = ROPE_DIM // 2
    inv_freq = ROPE_THETA ** (-2.0 * jnp.arange(half, dtype=F32) / ROPE_DIM)
    ang = pos.astype(F32)[:, None] * inv_freq
    cos, sin = jnp.cos(ang), jnp.sin(ang)
    t = pos.shape[0]
    one = jnp.ones((t, LANES - ROPE_DIM), F32)
    zero = jnp.zeros((t, LANES - ROPE_DIM), F32)
    zh = jnp.zeros((t, half), F32)
    c = jnp.concatenate([cos, cos, one], axis=1)
    s1 = jnp.concatenate([zh, sin, zero], axis=1)
    s2 = jnp.concatenate([-sin, zh, zero], axis=1)
    return c, s1, s2


def _pad_cols(w, mult):
    c = w.shape[1]
    cp = -(-c // mult) * mult
    return jnp.pad(w, ((0, 0), (0, cp - c)))


def _layer_weights(w_in, b_forget, d):
    h_f = d // (2 * HEAD_DIM)
    fw = h_f * HEAD_DIM
    dw = fw
    o = 0
    seg = {}
    for name, width in (("fq", fw), ("fk", fw), ("fv", fw), ("fl", h_f), ("dq", dw), ("dk", dw), ("dv", dw),
                        ("gates", 2 * d)):
        seg[name] = w_in[:, o:o + width]
        o += width
    assert o == w_in.shape[1]
    out = {k: v.astype(BF16) for k, v in seg.items() if k != "fl"}
    out["fl"] = _pad_cols(seg["fl"], LANES).astype(BF16)
    out["fl_bias"] = _pad_cols(b_forget.reshape(1, h_f), LANES)
    return out, h_f


def _project(x, norm_mix, pw, tables, h_f):
    qscale = LOG2E * HEAD_DIM ** -0.5
    xn = _rmsnorm(x, norm_mix, NORM_EPS)
    fq, = _proj(xn, pw["fq"], "scale_bf", scale=qscale)
    fk, fk_b = _proj(xn, pw["fk"], "f32_bf", head_dim=HEAD_DIM)
    fv, fv_b = _proj(xn, pw["fv"], "f32_bf", head_dim=HEAD_DIM)
    logf, = _proj(xn, pw["fl"], "logsig", bias=pw["fl_bias"], out_cols=h_f)
    dq, = _proj(xn, pw["dq"], "rope_bf", scale=qscale, tables=tables)
    dk, dk_b = _proj(xn, pw["dk"], "rope_f32_bf", tables=tables, head_dim=2 * HEAD_DIM)
    dv, dv_b = _proj(xn, pw["dv"], "f32_bf", head_dim=2 * HEAD_DIM)
    gates, = _proj(xn, pw["gates"], "sigmoid_bf")
    return dict(fq=fq, fk=fk, fk_b=fk_b, fv=fv, fv_b=fv_b, logf=logf, dq=dq, dk=dk, dk_b=dk_b,
                dv=dv, dv_b=dv_b, gates=gates)


def _cum_logf(logf_bth):
    b, s, h = logf_bth.shape
    spad = -(-s // 1024) * 1024
    x = jnp.pad(jnp.swapaxes(logf_bth, 1, 2), ((0, 0), (0, 0), (0, spad - s)))
    return _cumsum(x.reshape(b * h, spad)).reshape(b, h, 1, spad)


def _decay_columns(c):
    x = -LOG2E * c
    hi = _top_bits(x)
    mid = _top_bits(x - hi)
    lo = _top_bits(x - hi - mid)
    parts = jnp.stack([hi, mid, lo], axis=-1).astype(BF16)
    return jnp.pad(parts, ((0, 0), (0, 0), (0, 0), (0, LANES - 3)))


def kernel(x_prompt, x_sample, cache_fox_k, cache_fox_v, cache_fox_logf, cache_diff_k, cache_diff_v,
           p_prompt, p_sample, norm_mix, w_in, b_forget, lambda_q1, lambda_k1, lambda_q2, lambda_k2,
           diff_subln, w_o_fox, w_o_diff, w_out, norm_ffn, w_router, b_router, w_gate, b_gate,
           w_up, b_up, w_down, b_down, norm_ple, w_ple_gate, w_ple_proj, norm_final):
    depth = w_in.shape[0]
    bp, tp, d = x_prompt.shape
    bs, ts, _ = x_sample.shape
    past = cache_fox_k.shape[2]
    n_p, n_s = bp * tp, bs * ts
    n_experts = w_router.shape[-1]
    tb = 512

    h_p = x_prompt.reshape(n_p, d)
    h_s = x_sample.reshape(n_s, d)
    tab_p = _rope_tables(jnp.arange(tp, dtype=jnp.int32))
    tab_s = tuple(jnp.tile(t, (bs, 1)) for t in _rope_tables(past + jnp.arange(ts, dtype=jnp.int32)))
    st_p, st_s = [], []
    for i in range(depth):
        lam_init = 0.8 - 0.6 * math.exp(-0.3 * i)
        lam = (jnp.exp(jnp.sum(lambda_q1[i].astype(F32) * lambda_k1[i].astype(F32)))
               - jnp.exp(jnp.sum(lambda_q2[i].astype(F32) * lambda_k2[i].astype(F32)))
               + lam_init).reshape(1).astype(F32)
        out_scale = 1.0 - lam_init
        pw, h_f = _layer_weights(w_in[i], b_forget[i], d)
        h_d = h_f // 2
        subln = diff_subln[i].reshape(1, 2 * HEAD_DIM)
        wof, wod, wo = w_o_fox[i].astype(BF16), w_o_diff[i].astype(BF16), w_out[i].astype(BF16)
        wg, wu, wd = w_gate[i].astype(BF16), w_up[i].astype(BF16), w_down[i].astype(BF16)
        wpg, wpp = w_ple_gate[i].astype(BF16), w_ple_proj[i].astype(BF16)

        pr = _project(h_p, norm_mix[i], pw, tab_p, h_f)
        c_p = _cum_logf(pr["logf"].reshape(bp, tp, h_f))
        o_fox_p = _fox_prompt(pr["fq"], pr["fk_b"], pr["fv_b"], _decay_columns(c_p[:, :, 0, :tp]), bp, tp)
        o_diff_p = _diff_prompt(lam, pr["dq"], pr["dk_b"], pr["dv_b"], subln, bp, tp, out_scale)
        h1_p, xn_p, ridx_p, rw_p = _mix(_merge(o_fox_p, o_diff_p, pr["gates"], wof, wod), h_p, wo,
                                        norm_ffn[i], w_router[i], b_router[i])

        sr = _project(h_s, norm_mix[i], pw, tab_s, h_f)
        logf_all = jnp.concatenate([cache_fox_logf[i].astype(F32), sr["logf"].reshape(bs, ts, h_f)], axis=1)
        c_s = _cum_logf(logf_all)
        o_fox_s = _fox_decode(sr["fq"], cache_fox_k[i].reshape(bs * past, -1), cache_fox_v[i].reshape(bs * past, -1),
                              sr["fk_b"], sr["fv_b"], c_s, bs, ts, past)
        o_diff_s = _diff_decode(lam, sr["dq"], cache_diff_k[i].reshape(bs * past, -1),
                                cache_diff_v[i].reshape(bs * past, -1), sr["dk_b"], sr["dv_b"], subln,
                                bs, ts, past, out_scale)
        h1_s, xn_s, ridx_s, rw_s = _mix(_merge(o_fox_s, o_diff_s, sr["gates"], wof, wod), h_s, wo,
                                        norm_ffn[i], w_router[i], b_router[i])

        xn_all = jnp.concatenate([xn_p, xn_s], axis=0)
        top_idx = jnp.concatenate([ridx_p[:, :TOP_K], ridx_s[:, :TOP_K]], axis=0)
        block_exp, nact, nvalid, row_tok, row_slot = _routing(top_idx, n_experts, tb)
        n_all = n_p + n_s
        slots = _moe(block_exp, nact, nvalid, row_tok, row_slot, xn_all, wg, b_gate[i], wu, b_up[i], wd,
                     b_down[i], n_all * TOP_K)

        last = i == depth - 1
        h_p = _ple(h1_p, slots, n_all, 0, rw_p, p_prompt[i].reshape(n_p, -1), norm_ple[i], wpg, wpp,
                   norm_final, last)
        h_s = _ple(h1_s, slots, n_all, n_p, rw_s, p_sample[i].reshape(n_s, -1), norm_ple[i], wpg, wpp,
                   norm_final, last)

        st_p.append((pr["fk"].reshape(bp, tp, h_f, HEAD_DIM), pr["fv"].reshape(bp, tp, h_f, HEAD_DIM),
                     pr["logf"].reshape(bp, tp, h_f), pr["dk"].reshape(bp, tp, h_d, 2 * HEAD_DIM),
                     pr["dv"].reshape(bp, tp, h_d, 2 * HEAD_DIM)))
        st_s.append((sr["fk"].reshape(bs, ts, h_f, HEAD_DIM), sr["fv"].reshape(bs, ts, h_f, HEAD_DIM),
                     sr["logf"].reshape(bs, ts, h_f), sr["dk"].reshape(bs, ts, h_d, 2 * HEAD_DIM),
                     sr["dv"].reshape(bs, ts, h_d, 2 * HEAD_DIM)))

    y_prompt = h_p.reshape(bp, tp, d)
    y_sample = h_s.reshape(bs, ts, d)
    outs_p = [jnp.stack([s[j] for s in st_p]) for j in range(5)]
    outs_s = [jnp.stack([s[j] for s in st_s]) for j in range(5)]
    return (y_prompt, y_sample, *outs_p, *outs_s)
```

```python
import functools
import math

import jax
import jax.numpy as jnp
from jax import lax
from jax.experimental import pallas as pl
from jax.experimental.pallas import tpu as pltpu

F32 = jnp.float32
BF16 = jnp.bfloat16

HEAD_DIM = 128
CHUNK = 64
CHUNK_SHIFT = 6
assert 1 << CHUNK_SHIFT == CHUNK
ROPE_DIM = HEAD_DIM // 4
ROPE_THETA = 500000.0
TOP_K = 4
SWIGLU_LIMIT = 7.0
SWIGLU_ALPHA = 1.702
NORM_EPS = 1e-6
SUBLN_EPS = 1e-5
LOG2E = 1.4426950408889634
NEG = -1e30
LANES = 128
VMEM_LIMIT = 56 * 1024 * 1024


def _cparams(sem):
    return pltpu.CompilerParams(dimension_semantics=sem, vmem_limit_bytes=VMEM_LIMIT)


def _top_bits(v):
    bits = lax.bitcast_convert_type(v, jnp.uint32) & jnp.uint32(0xFFFF0000)
    return lax.bitcast_convert_type(bits, F32)


def _pack_halves(x):
    half = x.shape[1] // 2

    def rounded(v):
        bits = lax.bitcast_convert_type(v, jnp.uint32)
        return bits + jnp.uint32(0x7FFF) + ((bits >> 16) & jnp.uint32(1))

    return (rounded(x[:, half:]) & jnp.uint32(0xFFFF0000)) | (rounded(x[:, :half]) >> 16)


def _unpack_halves(w):
    lo = lax.bitcast_convert_type(w << 16, F32)
    hi = lax.bitcast_convert_type(w & jnp.uint32(0xFFFF0000), F32)
    return lo, hi


def _tile(n, pref):
    t = min(n, pref)
    assert n % t == 0, (n, pref)
    return t


def _rope_slab(x, c, s1, s2):
    return x * c + pltpu.roll(x, ROPE_DIM // 2, 1) * s1 + pltpu.roll(x, LANES - ROPE_DIM // 2, 1) * s2


def _proj_kernel(*refs, kind, scale):
    xn_ref, w_ref = refs[0], refs[1]
    if kind == "norm_scale_bf":
        x = xn_ref[...]
        xn = (x * lax.rsqrt(jnp.mean(x * x, axis=-1, keepdims=True) + NORM_EPS) * refs[2][...]).astype(BF16)
        refs[3][...] = xn
        refs[4][...] = (jnp.dot(xn, w_ref[...], preferred_element_type=F32) * scale).astype(BF16)
        return
    acc = jnp.dot(xn_ref[...], w_ref[...], preferred_element_type=F32)
    if kind == "scale_bf":
        refs[2][...] = (acc * scale).astype(BF16)
    elif kind == "f32_bf":
        hd = refs[2].shape[2]
        for h in range(refs[2].shape[1]):
            refs[2][:, h, :] = acc[:, h * hd:(h + 1) * hd]
        refs[3][...] = acc.astype(BF16)
    elif kind == "sigmoid_bf":
        refs[2][...] = jax.nn.sigmoid(acc).astype(BF16)
    elif kind == "logsig":
        z = acc + refs[2][...]
        val = jnp.minimum(z, 0.0) - jnp.log1p(jnp.exp(-jnp.abs(z)))
        refs[3][...] = val[:, :refs[3].shape[1]]
    elif kind in ("rope_bf", "rope_f32_bf"):
        c, s1, s2 = refs[2][...], refs[3][...], refs[4][...]
        for j in range(acc.shape[1] // LANES):
            sl = slice(j * LANES, (j + 1) * LANES)
            r = _rope_slab(acc[:, sl], c, s1, s2)
            if kind == "rope_bf":
                refs[5][:, sl] = (r * scale).astype(BF16)
            else:
                per_head = refs[5].shape[2] // LANES
                refs[5][:, j // per_head, (j % per_head) * LANES:(j % per_head + 1) * LANES] = r
                refs[6][:, sl] = r.astype(BF16)
    else:
        raise ValueError(kind)


def _proj(xn, w, kind, *, scale=1.0, tables=None, bias=None, out_cols=None, head_dim=None):
    n, d = xn.shape
    c = w.shape[1]
    tm = _tile(n, 1024)
    tn = _tile(c, 1024)
    grid = (c // tn, n // tm)
    in_specs = [pl.BlockSpec((tm, d), lambda j, i: (i, 0)),
                pl.BlockSpec((d, tn), lambda j, i: (0, j))]
    args = [xn, w]
    blk = pl.BlockSpec((tm, tn), lambda j, i: (i, j))
    if head_dim is not None:
        assert tn == c and c % head_dim == 0 and head_dim % LANES == 0
        heads_shape = jax.ShapeDtypeStruct((n, c // head_dim, head_dim), F32)
        heads_blk = pl.BlockSpec((tm, c // head_dim, head_dim), lambda j, i: (i, 0, 0))
    if kind == "scale_bf" or kind == "sigmoid_bf":
        out_shape = [jax.ShapeDtypeStruct((n, c), BF16)]
        out_specs = [blk]
    elif kind == "norm_scale_bf":
        assert tn == c
        in_specs.append(pl.BlockSpec((1, d), lambda j, i: (0, 0)))
        args.append(bias)
        out_shape = [jax.ShapeDtypeStruct((n, d), BF16), jax.ShapeDtypeStruct((n, c), BF16)]
        out_specs = [pl.BlockSpec((tm, d), lambda j, i: (i, 0)), blk]
    elif kind == "f32_bf":
        out_shape = [heads_shape, jax.ShapeDtypeStruct((n, c), BF16)]
        out_specs = [heads_blk, blk]
    elif kind == "logsig":
        in_specs.append(pl.BlockSpec((1, tn), lambda j, i: (0, j)))
        args.append(bias)
        out_shape = [jax.ShapeDtypeStruct((n, out_cols), F32)]
        out_specs = [pl.BlockSpec((tm, out_cols), lambda j, i: (i, 0))]
    else:
        nt = tables[0].shape[0] // tm
        for t in tables:
            in_specs.append(pl.BlockSpec((tm, LANES), lambda j, i, nt=nt: (i % nt, 0)))
            args.append(t)
        if kind == "rope_bf":
            out_shape = [jax.ShapeDtypeStruct((n, c), BF16)]
            out_specs = [blk]
        else:
            out_shape = [heads_shape, jax.ShapeDtypeStruct((n, c), BF16)]
            out_specs = [heads_blk, blk]
    return pl.pallas_call(
        functools.partial(_proj_kernel, kind=kind, scale=scale),
        out_shape=out_shape,
        grid=grid,
        in_specs=in_specs,
        out_specs=out_specs,
        compiler_params=_cparams(("parallel", "parallel")),
        name="proj_" + kind,
    )(*args)


def _cumsum_kernel(x_ref, o_ref):
    x = x_ref[...]
    r = x.shape[0]
    li = lax.broadcasted_iota(jnp.int32, (LANES, LANES), 0)
    lj = lax.broadcasted_iota(jnp.int32, (LANES, LANES), 1)
    upper = (li <= lj).astype(F32)
    within = jnp.dot(x, upper, preferred_element_type=F32, precision=lax.Precision.HIGHEST)
    tot = jnp.broadcast_to(within[:, LANES - 1:LANES], (r, LANES))
    ri = lax.broadcasted_iota(jnp.int32, (r, r), 0)
    rj = lax.broadcasted_iota(jnp.int32, (r, r), 1)
    strict = (rj < ri).astype(F32)
    off = jnp.dot(strict, tot, preferred_element_type=F32, precision=lax.Precision.HIGHEST)
    o_ref[...] = within + off


def _cumsum(x):
    g, s = x.shape
    r = s // LANES
    out = pl.pallas_call(
        _cumsum_kernel,
        out_shape=jax.ShapeDtypeStruct((g, r, LANES), F32),
        grid=(g,),
        in_specs=[pl.BlockSpec((None, r, LANES), lambda i: (i, 0, 0))],
        out_specs=pl.BlockSpec((None, r, LANES), lambda i: (i, 0, 0)),
        compiler_params=_cparams(("parallel",)),
        name="cumsum",
    )(x.reshape(g, r, LANES))
    return out.reshape(g, s)


def _qk(q, k):
    return lax.dot_general(q, k, (((1,), (1,)), ((), ())), preferred_element_type=F32)


def _lane_chunks(s):
    return [s[:, j * LANES:(j + 1) * LANES] for j in range(s.shape[1] // LANES)]


def _running_max(sj, m_s):
    smax = sj[0]
    for x in sj[1:]:
        smax = jnp.maximum(smax, x)
    m_prev = m_s[...]
    m_new = jnp.maximum(m_prev, jnp.max(smax, axis=-1, keepdims=True))
    m_s[...] = m_new
    return m_new, jnp.exp2(m_prev - m_new)


def _causal_tiles(t):
    tq = _tile(t, 1024)
    return tq, tq


def _fox_kernel(q_ref, k_ref, v_ref, c_ref, o_ref, m_s, acc_s, s_s, *, tq, tk):
    qi = pl.program_id(2)
    q = q_ref[...]
    ones = jnp.ones((tk, LANES), BF16)
    m_s[...] = jnp.full(m_s.shape, NEG, F32)
    acc_s[...] = jnp.zeros(acc_s.shape, F32)

    def scores(kj):
        off = pl.multiple_of(kj * tk, tk)
        return _qk(q, k_ref[pl.ds(off, tk), :]) - c_ref[:, pl.ds(off, tk)] * LOG2E

    def consume(s, kj):
        off = pl.multiple_of(kj * tk, tk)
        v_aug = jnp.concatenate([v_ref[pl.ds(off, tk), :], ones], axis=1)
        sj = _lane_chunks(s)
        m_new, alpha = _running_max(sj, m_s)
        p = jnp.concatenate([jnp.exp2(x - m_new).astype(BF16) for x in sj], axis=1)
        pv = jnp.dot(p, v_aug, preferred_element_type=F32)
        acc_s[...] = jnp.concatenate([alpha, alpha], axis=1) * acc_s[...] + pv

    nfull = (qi * tq) // tk
    s_s[...] = scores(0)

    def body(kj, carry):
        s = s_s[...]
        s_next = scores(kj + 1)
        consume(s, kj)
        s_s[...] = s_next
        return carry

    lax.fori_loop(0, nfull, body, 0)
    row = qi * tq + lax.broadcasted_iota(jnp.int32, (tq, tk), 0)
    col = lax.broadcasted_iota(jnp.int32, (tq, tk), 1)
    for d in range(tq // tk):
        kj = nfull + d
        s = s_s[...] if d == 0 else scores(kj)
        consume(jnp.where(col + kj * tk <= row, s, NEG), kj)
    acc = acc_s[...]
    o_ref[...] = (acc[:, :HEAD_DIM] / acc[:, HEAD_DIM:]).astype(o_ref.dtype)


def _fox_prompt(q, k, v, c, b, t):
    n, w = q.shape
    h = w // HEAD_DIM
    tq, tk = _causal_tiles(t)
    nq = t // tq
    return pl.pallas_call(
        functools.partial(_fox_kernel, tq=tq, tk=tk),
        out_shape=jax.ShapeDtypeStruct((n, w), BF16),
        grid=(b, h, nq),
        in_specs=[pl.BlockSpec((tq, HEAD_DIM), lambda bi, hi, qi: (bi * nq + qi, hi)),
                  pl.BlockSpec((t, HEAD_DIM), lambda bi, hi, qi: (bi, hi)),
                  pl.BlockSpec((t, HEAD_DIM), lambda bi, hi, qi: (bi, hi)),
                  pl.BlockSpec((None, None, 1, c.shape[-1]), lambda bi, hi, qi: (bi, hi, 0, 0))],
        out_specs=pl.BlockSpec((tq, HEAD_DIM), lambda bi, hi, qi: (bi * nq + qi, hi)),
        scratch_shapes=[pltpu.VMEM((tq, LANES), F32), pltpu.VMEM((tq, 2 * HEAD_DIM), F32),
                        pltpu.VMEM((tq, tk), F32)],
        compiler_params=_cparams(("parallel", "parallel", "arbitrary")),
        name="fox_prompt",
    )(q, k, v, c)


def _diff_finish(acc1, l1, acc2, l2, lam, gain, out_scale):
    o = acc1 / l1 - lam * (acc2 / l2)
    o = o * lax.rsqrt(jnp.mean(o * o, axis=-1, keepdims=True) + SUBLN_EPS) * gain
    return o * out_scale


def _diff_kernel(lam_ref, q_ref, k_ref, v_ref, g_ref, o_ref, m1, l1, a1, s1, m2, l2, a2, s2, *, tq, tk, out_scale):
    qi = pl.program_id(2)
    state = ((0, m1, l1, a1, s1), (HEAD_DIM, m2, l2, a2, s2))
    for _, m_s, l_s, a_s, _ in state:
        m_s[...] = jnp.full(m_s.shape, NEG, F32)
        l_s[...] = jnp.zeros(l_s.shape, F32)
        a_s[...] = jnp.zeros(a_s.shape, F32)

    def scores(kj, lo):
        off = pl.multiple_of(kj * tk, tk)
        return _qk(q_ref[:, lo:lo + HEAD_DIM], k_ref[pl.ds(off, tk), lo:lo + HEAD_DIM])

    def consume(s, v, m_s, l_s, a_s):
        sj = _lane_chunks(s)
        m_new, alpha = _running_max(sj, m_s)
        pj = [jnp.exp2(x - m_new) for x in sj]
        psum = pj[0]
        for x in pj[1:]:
            psum = psum + x
        p = jnp.concatenate([x.astype(BF16) for x in pj], axis=1)
        l_s[...] = alpha * l_s[...] + psum
        a_s[...] = jnp.concatenate([alpha, alpha], axis=1) * a_s[...] + jnp.dot(p, v, preferred_element_type=F32)

    nfull = (qi * tq) // tk
    for lo, _, _, _, s_s in state:
        s_s[...] = scores(0, lo)

    def body(kj, carry):
        off = pl.multiple_of(kj * tk, tk)
        v = v_ref[pl.ds(off, tk), :]
        for lo, m_s, l_s, a_s, s_s in state:
            s = s_s[...]
            s_next = scores(kj + 1, lo)
            consume(s, v, m_s, l_s, a_s)
            s_s[...] = s_next
        return carry

    lax.fori_loop(0, nfull, body, 0)
    row = qi * tq + lax.broadcasted_iota(jnp.int32, (tq, tk), 0)
    col = lax.broadcasted_iota(jnp.int32, (tq, tk), 1)
    for d in range(tq // tk):
        kj = nfull + d
        off = pl.multiple_of(kj * tk, tk)
        v = v_ref[pl.ds(off, tk), :]
        keep = ((col + kj * tk) >> CHUNK_SHIFT) <= (row >> CHUNK_SHIFT)
        for lo, m_s, l_s, a_s, s_s in state:
            s = s_s[...] if d == 0 else scores(kj, lo)
            consume(jnp.where(keep, s, NEG), v, m_s, l_s, a_s)
    o = _diff_finish(a1[...], jnp.sum(l1[...], axis=-1, keepdims=True),
                     a2[...], jnp.sum(l2[...], axis=-1, keepdims=True), lam_ref[0], g_ref[...], out_scale)
    o_ref[...] = o.astype(o_ref.dtype)


def _diff_prompt(lam, q, k, v, gain, b, t, out_scale):
    n, w = q.shape
    hw = 2 * HEAD_DIM
    h = w // hw
    tq, tk = _causal_tiles(t)
    assert tq % CHUNK == 0
    nq = t // tq
    per_map = [pltpu.VMEM((tq, LANES), F32), pltpu.VMEM((tq, LANES), F32), pltpu.VMEM((tq, hw), F32),
               pltpu.VMEM((tq, tk), F32)]
    return pl.pallas_call(
        functools.partial(_diff_kernel, tq=tq, tk=tk, out_scale=out_scale),
        out_shape=jax.ShapeDtypeStruct((n, w), BF16),
        grid_spec=pltpu.PrefetchScalarGridSpec(
            num_scalar_prefetch=1,
            grid=(b, h, nq),
            in_specs=[pl.BlockSpec((tq, hw), lambda bi, hi, qi, lam: (bi * nq + qi, hi)),
                      pl.BlockSpec((t, hw), lambda bi, hi, qi, lam: (bi, hi)),
                      pl.BlockSpec((t, hw), lambda bi, hi, qi, lam: (bi, hi)),
                      pl.BlockSpec((1, hw), lambda bi, hi, qi, lam: (0, 0))],
            out_specs=pl.BlockSpec((tq, hw), lambda bi, hi, qi, lam: (bi * nq + qi, hi)),
            scratch_shapes=per_map + per_map),
        compiler_params=_cparams(("parallel", "parallel", "arbitrary")),
        name="diff_prompt",
    )(lam, q, k, v, gain)


def _fox_decode_kernel(q_ref, kc_ref, vc_ref, kn_ref, vn_ref, c_ref, o_ref, *, past, tq):
    q = q_ref[...]
    s_c = _qk(q, kc_ref[...].astype(BF16)) - c_ref[:, :past] * LOG2E
    c_new = c_ref[:, past:past + LANES]
    s_n = _qk(q, kn_ref[...]) - c_new[:, :tq] * LOG2E
    row = lax.broadcasted_iota(jnp.int32, (tq, tq), 0)
    col = lax.broadcasted_iota(jnp.int32, (tq, tq), 1)
    s_n = jnp.where(col <= row, s_n, NEG)
    m = jnp.maximum(jnp.max(s_c, axis=-1, keepdims=True), jnp.max(s_n, axis=-1, keepdims=True))
    p_c = jnp.exp2(s_c - m)
    p_n = jnp.exp2(s_n - m)
    l = jnp.sum(p_c, axis=-1, keepdims=True) + jnp.sum(p_n, axis=-1, keepdims=True)
    o = (jnp.dot(p_c.astype(BF16), vc_ref[...].astype(BF16), preferred_element_type=F32)
         + jnp.dot(p_n.astype(BF16), vn_ref[...], preferred_element_type=F32))
    o_ref[...] = (o / l).astype(o_ref.dtype)


def _fox_decode(q, kc, vc, kn, vn, c, b, t, past):
    n, w = q.shape
    h = w // HEAD_DIM
    assert past % LANES == 0 and t <= LANES
    spad = c.shape[-1]
    new = pl.BlockSpec((t, HEAD_DIM), lambda bi, hi: (bi, hi))
    old = pl.BlockSpec((past, HEAD_DIM), lambda bi, hi: (bi, hi))
    return pl.pallas_call(
        functools.partial(_fox_decode_kernel, past=past, tq=t),
        out_shape=jax.ShapeDtypeStruct((n, w), BF16),
        grid=(b, h),
        in_specs=[new, old, old, new, new,
                  pl.BlockSpec((None, None, 1, spad), lambda bi, hi: (bi, hi, 0, 0))],
        out_specs=new,
        compiler_params=_cparams(("parallel", "parallel")),
        name="fox_decode",
    )(q, kc, vc, kn, vn, c)


def _diff_decode_kernel(lam_ref, q_ref, kc_ref, vc_ref, kn_ref, vn_ref, g_ref, o_ref, *, past, tq, out_scale):
    vc = vc_ref[...].astype(BF16)
    vn = vn_ref[...]
    row = past + lax.broadcasted_iota(jnp.int32, (tq, tq), 0)
    col = past + lax.broadcasted_iota(jnp.int32, (tq, tq), 1)
    keep = (col >> CHUNK_SHIFT) <= (row >> CHUNK_SHIFT)
    accs, ls = [], []
    for lo in (0, HEAD_DIM):
        q = q_ref[:, lo:lo + HEAD_DIM]
        s_c = _qk(q, kc_ref[:, lo:lo + HEAD_DIM].astype(BF16))
        s_n = jnp.where(keep, _qk(q, kn_ref[:, lo:lo + HEAD_DIM]), NEG)
        m = jnp.maximum(jnp.max(s_c, axis=-1, keepdims=True), jnp.max(s_n, axis=-1, keepdims=True))
        p_c = jnp.exp2(s_c - m)
        p_n = jnp.exp2(s_n - m)
        ls.append(jnp.sum(p_c, axis=-1, keepdims=True) + jnp.sum(p_n, axis=-1, keepdims=True))
        accs.append(jnp.dot(p_c.astype(BF16), vc, preferred_element_type=F32)
                    + jnp.dot(p_n.astype(BF16), vn, preferred_element_type=F32))
    o = _diff_finish(accs[0], ls[0], accs[1], ls[1], lam_ref[0], g_ref[...], out_scale)
    o_ref[...] = o.astype(o_ref.dtype)


def _diff_decode(lam, q, kc, vc, kn, vn, gain, b, t, past, out_scale):
    n, w = q.shape
    hw = 2 * HEAD_DIM
    h = w // hw
    assert (past - 1) // CHUNK <= past // CHUNK
    new = pl.BlockSpec((t, hw), lambda bi, hi, lam: (bi, hi))
    old = pl.BlockSpec((past, hw), lambda bi, hi, lam: (bi, hi))
    return pl.pallas_call(
        functools.partial(_diff_decode_kernel, past=past, tq=t, out_scale=out_scale),
        out_shape=jax.ShapeDtypeStruct((n, w), BF16),
        grid_spec=pltpu.PrefetchScalarGridSpec(
            num_scalar_prefetch=1,
            grid=(b, h),
            in_specs=[new, old, old, new, new,
                      pl.BlockSpec((1, hw), lambda bi, hi, lam: (0, 0))],
            out_specs=new),
        compiler_params=_cparams(("parallel", "parallel")),
        name="diff_decode",
    )(lam, q, kc, vc, kn, vn, gain)


def _merge_kernel(of_ref, od_ref, gf_ref, gd_ref, wof_ref, wod_ref, o_ref):
    yf = jnp.dot(of_ref[...], wof_ref[...], preferred_element_type=F32)
    yd = jnp.dot(od_ref[...], wod_ref[...], preferred_element_type=F32)
    o_ref[...] = (gf_ref[...].astype(F32) * yf + gd_ref[...].astype(F32) * yd).astype(o_ref.dtype)


def _merge(o_fox, o_diff, gates, w_o_fox, w_o_diff):
    n = o_fox.shape[0]
    d = w_o_fox.shape[1]
    tm = _tile(n, 512)
    row = lambda c: pl.BlockSpec((tm, c), lambda i: (i, 0))
    full = lambda a: pl.BlockSpec(a.shape, lambda i: (0, 0))
    return pl.pallas_call(
        _merge_kernel,
        out_shape=jax.ShapeDtypeStruct((n, d), BF16),
        grid=(n // tm,),
        in_specs=[row(o_fox.shape[1]), row(o_diff.shape[1]),
                  pl.BlockSpec((tm, d), lambda i: (i, 0)), pl.BlockSpec((tm, d), lambda i: (i, 1)),
                  full(w_o_fox), full(w_o_diff)],
        out_specs=row(d),
        compiler_params=_cparams(("parallel",)),
        name="merge",
    )(o_fox, o_diff, gates, gates, w_o_fox, w_o_diff)


def _mix_kernel(m_ref, x_ref, wout_ref, nrm_ref, wr_ref, br_ref, h_ref, xn_ref, ridx_ref, rw_ref, *, n_experts):
    h = x_ref[...] + jnp.dot(m_ref[...], wout_ref[...], preferred_element_type=F32)
    h_ref[...] = h
    xf = h * lax.rsqrt(jnp.mean(h * h, axis=-1, keepdims=True) + NORM_EPS) * nrm_ref[...]
    xn_ref[...] = _pack_halves(xf)
    xh = xf.astype(BF16)
    xl = (xf - xh.astype(F32)).astype(BF16)
    wr = wr_ref[...]
    both = jnp.dot(xh, wr, preferred_element_type=F32)
    logits = (both[:, :LANES] + both[:, LANES:] + jnp.dot(xl, wr[:, :LANES], preferred_element_type=F32)
              + br_ref[...])
    lane = lax.broadcasted_iota(jnp.int32, logits.shape, 1).astype(F32)
    cur = jnp.where(lane < n_experts, logits, -jnp.inf)
    vals, idxs = [], []
    for _ in range(TOP_K):
        mk = jnp.max(cur, axis=-1, keepdims=True)
        ik = jnp.min(jnp.where(cur == mk, lane, float(LANES)), axis=-1, keepdims=True)
        cur = jnp.where(lane == ik, -jnp.inf, cur)
        vals.append(mk)
        idxs.append(ik)
    es = [jnp.exp(vk - vals[0]) for vk in vals]
    den = es[0] + es[1] + es[2] + es[3]
    ridx = jnp.zeros(logits.shape, F32)
    rw = jnp.zeros(logits.shape, F32)
    for k in range(TOP_K):
        ridx = jnp.where(lane == float(k), idxs[k], ridx)
        rw = jnp.where(lane == float(k), es[k] / den, rw)
    ridx_ref[...] = ridx.astype(jnp.int32)
    rw_ref[...] = rw


def _mix(merged, x, w_out, norm_ffn, w_router, b_router):
    n, d = x.shape
    e = w_router.shape[1]
    tm = _tile(n, 512)
    w_pad = _pad_cols(w_router, LANES)
    w_hi = _top_bits(w_pad)
    wr = jnp.concatenate([w_hi, _top_bits(w_pad - w_hi)], axis=1).astype(BF16)
    br = _pad_cols(b_router.reshape(1, e), LANES)
    row = lambda c: pl.BlockSpec((tm, c), lambda i: (i, 0))
    full = lambda a: pl.BlockSpec(a.shape, lambda i: (0, 0), pipeline_mode=pl.Buffered(1))
    args = [merged, x, w_out, norm_ffn.reshape(1, d), wr, br]
    in_specs = [row(d), row(d), full(w_out), full(args[3]), full(wr), full(br)]
    return pl.pallas_call(
        functools.partial(_mix_kernel, n_experts=e),
        out_shape=[jax.ShapeDtypeStruct((n, d), F32), jax.ShapeDtypeStruct((n, d // 2), jnp.uint32),
                   jax.ShapeDtypeStruct((n, LANES), jnp.int32), jax.ShapeDtypeStruct((n, LANES), F32)],
        grid=(n // tm,),
        in_specs=in_specs,
        out_specs=[row(d), row(d // 2), row(LANES), row(LANES)],
        compiler_params=_cparams(("parallel",)),
        name="mix",
    )(*args)


def _moe_kernel(be_ref, nact_ref, nv_ref, tok_hbm, slot_hbm, x_hbm, wg_ref, bg_ref, wu_ref, bu_ref, wd_ref,
                bd_ref, out_hbm, xbuf, xb, tok_s, slot_s, acc, ostage, gsem, ssem, isem, *, tb, nft, n_real):
    blk = pl.program_id(0)
    ft = pl.program_id(1)
    nact = nact_ref[0]
    active = blk < nact
    cur = lax.rem(blk, 2)
    nxt = 1 - cur
    per_step = tb // nft

    def tok_copy(b, s):
        return pltpu.make_async_copy(tok_hbm.at[pl.ds(b, 1)], tok_s.at[pl.ds(s, 1)], isem.at[0])

    def slot_copy(b, s):
        return pltpu.make_async_copy(slot_hbm.at[pl.ds(b, 1)], slot_s.at[pl.ds(s, 1)], isem.at[1])

    def gather_row(s, r):
        t = tok_s[s, r]
        pltpu.make_async_copy(x_hbm.at[pl.ds(t, 1)], xbuf.at[s, pl.ds(r, 1)], gsem.at[s]).start()

    def scatter_row(s, r):
        d = slot_s[s, r]
        pltpu.make_async_copy(ostage.at[s, pl.ds(r, 1)], out_hbm.at[pl.ds(d, 1)], ssem.at[s]).start(priority=1)

    def wait_gather(s):
        pltpu.make_async_copy(xbuf.at[s], xbuf.at[s], gsem.at[s]).wait()

    def wait_scatter(s):
        pltpu.make_async_copy(ostage.at[s], ostage.at[s], ssem.at[s]).wait()

    def loop_rows(fn, s):
        def body(r, carry):
            fn(s, r)
            return carry
        lax.fori_loop(0, tb, body, 0)

    @pl.when((blk == 0) & (ft == 0))
    def _():
        ostage[1] = jnp.zeros(ostage.shape[1:], ostage.dtype)

        def spare(r, carry):
            slot_s[1, r] = n_real + r
            return carry
        lax.fori_loop(0, tb, spare, 0)
        for b in range(2):
            c = tok_copy(b, b)
            c.start()
            c.wait()
        loop_rows(gather_row, 0)

    @pl.when((ft == 0) & (blk >= 1) & (blk <= nact))
    def _():
        tok_copy(blk + 1, nxt).wait()
        slot_copy(blk - 1, nxt).wait()

    @pl.when((ft == 0) & (blk <= nact))
    def _():
        wait_gather(cur)

    @pl.when((ft == 0) & (blk >= 1) & (blk - 2 < nact))
    def _():
        wait_scatter(cur)

    @pl.when((ft == 0) & active)
    def _():
        tok_copy(blk + 2, cur).start()
        slot_copy(blk, cur).start()
        lo, hi = _unpack_halves(xbuf[cur])
        xb[...] = jnp.concatenate([lo.astype(BF16), hi.astype(BF16)], axis=1)
        acc[...] = jnp.broadcast_to(bd_ref[...], acc.shape)

    @pl.when((ft == 0) & (blk == nact) & (blk >= 1))
    def _():
        loop_rows(scatter_row, nxt)

    def compute(rows):
        base = ft * per_step
        for j in range(per_step):
            gather_row(nxt, base + j)
        for j in range(per_step):
            scatter_row(nxt, base + j)
        x = xb[:rows]
        g = jnp.dot(x, wg_ref[...], preferred_element_type=F32) + bg_ref[...]
        u = jnp.dot(x, wu_ref[...], preferred_element_type=F32) + bu_ref[...]
        g = jnp.minimum(g, SWIGLU_LIMIT)
        u = jnp.clip(u, -SWIGLU_LIMIT, SWIGLU_LIMIT)
        act = (u + 1.0) * (g * jax.nn.sigmoid(SWIGLU_ALPHA * g))
        acc[:rows] += jnp.dot(act.astype(BF16), wd_ref[...], preferred_element_type=F32)

    half_full = nv_ref[blk] <= tb // 2

    @pl.when(active & half_full)
    def _():
        compute(tb // 2)

    @pl.when(active & jnp.logical_not(half_full))
    def _():
        compute(tb)

    @pl.when(active & (ft == nft - 1))
    def _():
        ostage[cur] = _pack_halves(acc[...])


def _moe(block_exp, nact, nvalid, row_tok, row_slot, xn, w_gate, b_gate, w_up, b_up, w_down, b_down, n_real):
    nb, tb = row_tok.shape
    e, d, f = w_gate.shape
    tf = _tile(f, 1024)
    nft = f // tf
    d2 = d // 2
    assert nft >= 2 and tb % nft == 0 and xn.shape[1] == d2

    def ftile(b, j, be, na, nv):
        return jnp.where(b < na[0], j, nft - 1)

    any_spec = pl.BlockSpec(memory_space=pl.ANY)
    in_specs = [any_spec, any_spec, any_spec,
                pl.BlockSpec((None, d, tf), lambda b, j, be, na, nv: (be[b], 0, ftile(b, j, be, na, nv))),
                pl.BlockSpec((None, 1, tf), lambda b, j, be, na, nv: (be[b], 0, ftile(b, j, be, na, nv))),
                pl.BlockSpec((None, d, tf), lambda b, j, be, na, nv: (be[b], 0, ftile(b, j, be, na, nv))),
                pl.BlockSpec((None, 1, tf), lambda b, j, be, na, nv: (be[b], 0, ftile(b, j, be, na, nv))),
                pl.BlockSpec((None, tf, d), lambda b, j, be, na, nv: (be[b], ftile(b, j, be, na, nv), 0)),
                pl.BlockSpec((None, 1, d), lambda b, j, be, na, nv: (be[b], 0, 0))]
    return pl.pallas_call(
        functools.partial(_moe_kernel, tb=tb, nft=nft, n_real=n_real),
        out_shape=jax.ShapeDtypeStruct((n_real + tb, d2), jnp.uint32),
        grid_spec=pltpu.PrefetchScalarGridSpec(
            num_scalar_prefetch=3,
            grid=(nb, nft),
            in_specs=in_specs,
            out_specs=any_spec,
            scratch_shapes=[pltpu.VMEM((2, tb, d2), jnp.uint32), pltpu.VMEM((tb, d), BF16),
                            pltpu.SMEM((2, tb), jnp.int32), pltpu.SMEM((2, tb), jnp.int32),
                            pltpu.VMEM((tb, d), F32), pltpu.VMEM((2, tb, d2), jnp.uint32),
                            pltpu.SemaphoreType.DMA((2,)), pltpu.SemaphoreType.DMA((2,)),
                            pltpu.SemaphoreType.DMA((2,))]),
        compiler_params=_cparams(("arbitrary", "arbitrary")),
        name="moe",
    )(block_exp, nact, nvalid, row_tok, row_slot, xn, w_gate, b_gate.reshape(e, 1, f), w_up,
      b_up.reshape(e, 1, f), w_down, b_down.reshape(e, 1, d))


def _routing(top_idx, n_experts, tb):
    n = top_idx.shape[0]
    a = n * TOP_K
    expert = top_idx.reshape(-1)
    order = jnp.argsort(expert).astype(jnp.int32)
    counts = jnp.sum(expert[:, None] == jnp.arange(n_experts, dtype=jnp.int32)[None, :], axis=0, dtype=jnp.int32)
    padded = (counts + tb - 1) // tb * tb
    start = jnp.cumsum(counts) - counts
    pend = jnp.cumsum(padded)
    pstart = pend - padded
    nb = -(-a // tb) + n_experts + 1
    row0 = jnp.arange(nb, dtype=jnp.int32) * tb
    block_exp = jnp.minimum(jnp.sum(pend[None, :] <= row0[:, None], axis=1), n_experts - 1).astype(jnp.int32)
    nvalid = jnp.clip(pstart[block_exp] + counts[block_exp] - row0, 0, tb)
    src0 = jnp.clip(start[block_exp] + row0 - pstart[block_exp], 0, a)
    order_pad = jnp.concatenate([order, jnp.zeros((tb,), jnp.int32)])
    r = jnp.arange(tb, dtype=jnp.int32)[None, :]
    win = order_pad[src0[:, None] + r]
    valid = r < nvalid[:, None]
    tok = win // TOP_K
    row_tok = jnp.where(valid, tok, 0)
    row_slot = jnp.where(valid, (win % TOP_K) * n + tok, a + r)
    nact = (pend[-1] // tb).reshape(1).astype(jnp.int32)
    return block_exp, nact, nvalid.astype(jnp.int32), row_tok.astype(jnp.int32), row_slot.astype(jnp.int32)


def _ple_kernel(h_ref, s0_ref, s1_ref, s2_ref, s3_ref, rw_ref, pe_ref, nple_ref, wg_ref, wp_ref, nfin_ref,
                o_ref, *, final):
    h = h_ref[...]
    rw = rw_ref[...]
    for k, s_ref in enumerate((s0_ref, s1_ref, s2_ref, s3_ref)):
        h = h + rw[:, k:k + 1] * jnp.concatenate(_unpack_halves(s_ref[...]), axis=1)
    xn = h * lax.rsqrt(jnp.mean(h * h, axis=-1, keepdims=True) + NORM_EPS) * nple_ref[...]
    gate = jax.nn.sigmoid(jnp.dot(xn.astype(BF16), wg_ref[...], preferred_element_type=F32))
    h = h + gate * jnp.dot(pe_ref[...].astype(BF16), wp_ref[...], preferred_element_type=F32)
    if final:
        h = h * lax.rsqrt(jnp.mean(h * h, axis=-1, keepdims=True) + NORM_EPS) * nfin_ref[...]
    o_ref[...] = h


def _ple(h, slots, n_all, tok0, rw, pe, norm_ple, w_ple_gate, w_ple_proj, norm_final, final):
    n, d = h.shape
    tm = _tile(n, 256)
    assert TOP_K == 4 and n_all % tm == 0 and tok0 % tm == 0
    row = lambda c: pl.BlockSpec((tm, c), lambda i: (i, 0))
    full = lambda a: pl.BlockSpec(a.shape, lambda i: (0, 0))
    slot = lambda k: pl.BlockSpec((tm, d // 2), lambda i, t0=(k * n_all + tok0) // tm: (t0 + i, 0))
    g1 = norm_ple.reshape(1, d)
    g2 = norm_final.reshape(1, d)
    return pl.pallas_call(
        functools.partial(_ple_kernel, final=final),
        out_shape=jax.ShapeDtypeStruct((n, d), F32),
        grid=(n // tm,),
        in_specs=[row(d), slot(0), slot(1), slot(2), slot(3), row(LANES), row(pe.shape[1]),
                  full(g1), full(w_ple_gate), full(w_ple_proj), full(g2)],
        out_specs=row(d),
        compiler_params=_cparams(("parallel",)),
        name="ple",
    )(h, slots, slots, slots, slots, rw, pe, g1, w_ple_gate, w_ple_proj, g2)


def _rope_tables(pos):
    half = ROPE_DIM // 2
    inv_freq = ROPE_THETA ** (-2.0 * jnp.arange(half, dtype=F32) / ROPE_DIM)
    ang = pos.astype(F32)[:, None] * inv_freq
    cos, sin = jnp.cos(ang), jnp.sin(ang)
    t = pos.shape[0]
    one = jnp.ones((t, LANES - ROPE_DIM), F32)
    zero = jnp.zeros((t, LANES - ROPE_DIM), F32)
    zh = jnp.zeros((t, half), F32)
    c = jnp.concatenate([cos, cos, one], axis=1)
    s1 = jnp.concatenate([zh, sin, zero], axis=1)
    s2 = jnp.concatenate([-sin, zh, zero], axis=1)
    return c, s1, s2


def _pad_cols(w, mult):
    c = w.shape[1]
    cp = -(-c // mult) * mult
    return jnp.pad(w, ((0, 0), (0, cp - c)))


def _layer_weights(w_in, b_forget, d):
    h_f = d // (2 * HEAD_DIM)
    fw = h_f * HEAD_DIM
    dw = fw
    o = 0
    seg = {}
    for name, width in (("fq", fw), ("fk", fw), ("fv", fw), ("fl", h_f), ("dq", dw), ("dk", dw), ("dv", dw),
                        ("gates", 2 * d)):
        seg[name] = w_in[:, o:o + width]
        o += width
    assert o == w_in.shape[1]
    out = {k: v.astype(BF16) for k, v in seg.items() if k != "fl"}
    out["fl"] = _pad_cols(seg["fl"], LANES).astype(BF16)
    out["fl_bias"] = _pad_cols(b_forget.reshape(1, h_f), LANES)
    return out, h_f


def _project(x, norm_mix, pw, tables, h_f):
    qscale = LOG2E * HEAD_DIM ** -0.5
    xn, fq = _proj(x, pw["fq"], "norm_scale_bf", scale=qscale, bias=norm_mix.reshape(1, -1))
    fk, fk_b = _proj(xn, pw["fk"], "f32_bf", head_dim=HEAD_DIM)
    fv, fv_b = _proj(xn, pw["fv"], "f32_bf", head_dim=HEAD_DIM)
    logf, = _proj(xn, pw["fl"], "logsig", bias=pw["fl_bias"], out_cols=h_f)
    dq, = _proj(xn, pw["dq"], "rope_bf", scale=qscale, tables=tables)
    dk, dk_b = _proj(xn, pw["dk"], "rope_f32_bf", tables=tables, head_dim=2 * HEAD_DIM)
    dv, dv_b = _proj(xn, pw["dv"], "f32_bf", head_dim=2 * HEAD_DIM)
    gates, = _proj(xn, pw["gates"], "sigmoid_bf")
    return dict(fq=fq, fk=fk, fk_b=fk_b, fv=fv, fv_b=fv_b, logf=logf, dq=dq, dk=dk, dk_b=dk_b,
                dv=dv, dv_b=dv_b, gates=gates)


def _cum_logf(logf_bth):
    b, s, h = logf_bth.shape
    spad = -(-s // 1024) * 1024
    x = jnp.pad(jnp.swapaxes(logf_bth, 1, 2), ((0, 0), (0, 0), (0, spad - s)))
    return _cumsum(x.reshape(b * h, spad)).reshape(b, h, 1, spad)


def kernel(x_prompt, x_sample, cache_fox_k, cache_fox_v, cache_fox_logf, cache_diff_k, cache_diff_v,
           p_prompt, p_sample, norm_mix, w_in, b_forget, lambda_q1, lambda_k1, lambda_q2, lambda_k2,
           diff_subln, w_o_fox, w_o_diff, w_out, norm_ffn, w_router, b_router, w_gate, b_gate,
           w_up, b_up, w_down, b_down, norm_ple, w_ple_gate, w_ple_proj, norm_final):
    depth = w_in.shape[0]
    bp, tp, d = x_prompt.shape
    bs, ts, _ = x_sample.shape
    past = cache_fox_k.shape[2]
    n_p, n_s = bp * tp, bs * ts
    n_experts = w_router.shape[-1]
    tb = 512

    h_p = x_prompt.reshape(n_p, d)
    h_s = x_sample.reshape(n_s, d)
    tab_p = _rope_tables(jnp.arange(tp, dtype=jnp.int32))
    tab_s = tuple(jnp.tile(t, (bs, 1)) for t in _rope_tables(past + jnp.arange(ts, dtype=jnp.int32)))
    st_p, st_s = [], []
    for i in range(depth):
        lam_init = 0.8 - 0.6 * math.exp(-0.3 * i)
        lam = (jnp.exp(jnp.sum(lambda_q1[i].astype(F32) * lambda_k1[i].astype(F32)))
               - jnp.exp(jnp.sum(lambda_q2[i].astype(F32) * lambda_k2[i].astype(F32)))
               + lam_init).reshape(1).astype(F32)
        out_scale = 1.0 - lam_init
        pw, h_f = _layer_weights(w_in[i], b_forget[i], d)
        h_d = h_f // 2
        subln = diff_subln[i].reshape(1, 2 * HEAD_DIM)
        wof, wod, wo = w_o_fox[i].astype(BF16), w_o_diff[i].astype(BF16), w_out[i].astype(BF16)
        wg, wu, wd = w_gate[i].astype(BF16), w_up[i].astype(BF16), w_down[i].astype(BF16)
        wpg, wpp = w_ple_gate[i].astype(BF16), w_ple_proj[i].astype(BF16)

        pr = _project(h_p, norm_mix[i], pw, tab_p, h_f)
        c_p = _cum_logf(pr["logf"].reshape(bp, tp, h_f))
        o_fox_p = _fox_prompt(pr["fq"], pr["fk_b"], pr["fv_b"], c_p, bp, tp)
        o_diff_p = _diff_prompt(lam, pr["dq"], pr["dk_b"], pr["dv_b"], subln, bp, tp, out_scale)
        h1_p, xn_p, ridx_p, rw_p = _mix(_merge(o_fox_p, o_diff_p, pr["gates"], wof, wod), h_p, wo,
                                        norm_ffn[i], w_router[i], b_router[i])

        sr = _project(h_s, norm_mix[i], pw, tab_s, h_f)
        logf_all = jnp.concatenate([cache_fox_logf[i].astype(F32), sr["logf"].reshape(bs, ts, h_f)], axis=1)
        c_s = _cum_logf(logf_all)
        o_fox_s = _fox_decode(sr["fq"], cache_fox_k[i].reshape(bs * past, -1), cache_fox_v[i].reshape(bs * past, -1),
                              sr["fk_b"], sr["fv_b"], c_s, bs, ts, past)
        o_diff_s = _diff_decode(lam, sr["dq"], cache_diff_k[i].reshape(bs * past, -1),
                                cache_diff_v[i].reshape(bs * past, -1), sr["dk_b"], sr["dv_b"], subln,
                                bs, ts, past, out_scale)
        h1_s, xn_s, ridx_s, rw_s = _mix(_merge(o_fox_s, o_diff_s, sr["gates"], wof, wod), h_s, wo,
                                        norm_ffn[i], w_router[i], b_router[i])

        xn_all = jnp.concatenate([xn_p, xn_s], axis=0)
        top_idx = jnp.concatenate([ridx_p[:, :TOP_K], ridx_s[:, :TOP_K]], axis=0)
        block_exp, nact, nvalid, row_tok, row_slot = _routing(top_idx, n_experts, tb)
        n_all = n_p + n_s
        slots = _moe(block_exp, nact, nvalid, row_tok, row_slot, xn_all, wg, b_gate[i], wu, b_up[i], wd,
                     b_down[i], n_all * TOP_K)

        last = i == depth - 1
        h_p = _ple(h1_p, slots, n_all, 0, rw_p, p_prompt[i].reshape(n_p, -1), norm_ple[i], wpg, wpp,
                   norm_final, last)
        h_s = _ple(h1_s, slots, n_all, n_p, rw_s, p_sample[i].reshape(n_s, -1), norm_ple[i], wpg, wpp,
                   norm_final, last)

        st_p.append((pr["fk"].reshape(bp, tp, h_f, HEAD_DIM), pr["fv"].reshape(bp, tp, h_f, HEAD_DIM),
                     pr["logf"].reshape(bp, tp, h_f), pr["dk"].reshape(bp, tp, h_d, 2 * HEAD_DIM),
                     pr["dv"].reshape(bp, tp, h_d, 2 * HEAD_DIM)))
        st_s.append((sr["fk"].reshape(bs, ts, h_f, HEAD_DIM), sr["fv"].reshape(bs, ts, h_f, HEAD_DIM),
                     sr["logf"].reshape(bs, ts, h_f), sr["dk"].reshape(bs, ts, h_d, 2 * HEAD_DIM),
                     sr["dv"].reshape(bs, ts, h_d, 2 * HEAD_DIM)))

    y_prompt = h_p.reshape(bp, tp, d)
    y_sample = h_s.reshape(bs, ts, d)
    outs_p = [jnp.stack([s[j] for s in st_p]) for j in range(5)]
    outs_s = [jnp.stack([s[j] for s in st_s]) for j in range(5)]
    return (y_prompt, y_sample, *outs_p, *outs_s)
```

```python
import functools
import math

import jax
import jax.numpy as jnp
from jax import lax
from jax.experimental import pallas as pl
from jax.experimental.pallas import tpu as pltpu

F32 = jnp.float32
BF16 = jnp.bfloat16

HEAD_DIM = 128
CHUNK = 64
CHUNK_SHIFT = 6
assert 1 << CHUNK_SHIFT == CHUNK
ROPE_DIM = HEAD_DIM // 4
ROPE_THETA = 500000.0
TOP_K = 4
SWIGLU_LIMIT = 7.0
SWIGLU_ALPHA = 1.702
NORM_EPS = 1e-6
SUBLN_EPS = 1e-5
LOG2E = 1.4426950408889634
NEG = -1e30
LANES = 128
VMEM_LIMIT = 56 * 1024 * 1024


def _cparams(sem):
    return pltpu.CompilerParams(dimension_semantics=sem, vmem_limit_bytes=VMEM_LIMIT)


def _top_bits(v):
    bits = lax.bitcast_convert_type(v, jnp.uint32) & jnp.uint32(0xFFFF0000)
    return lax.bitcast_convert_type(bits, F32)


def _pack_halves(x):
    half = x.shape[1] // 2

    def rounded(v):
        bits = lax.bitcast_convert_type(v, jnp.uint32)
        return bits + jnp.uint32(0x7FFF) + ((bits >> 16) & jnp.uint32(1))

    return (rounded(x[:, half:]) & jnp.uint32(0xFFFF0000)) | (rounded(x[:, :half]) >> 16)


def _unpack_halves(w):
    lo = lax.bitcast_convert_type(w << 16, F32)
    hi = lax.bitcast_convert_type(w & jnp.uint32(0xFFFF0000), F32)
    return lo, hi


def _tile(n, pref):
    t = min(n, pref)
    assert n % t == 0, (n, pref)
    return t


def _rope_slab(x, c, s1, s2):
    return x * c + pltpu.roll(x, ROPE_DIM // 2, 1) * s1 + pltpu.roll(x, LANES - ROPE_DIM // 2, 1) * s2


def _proj_kernel(*refs, kind, scale):
    xn_ref, w_ref = refs[0], refs[1]
    if kind == "norm_scale_bf":
        x = xn_ref[...]
        xn = (x * lax.rsqrt(jnp.mean(x * x, axis=-1, keepdims=True) + NORM_EPS) * refs[2][...]).astype(BF16)
        refs[3][...] = xn
        refs[4][...] = (jnp.dot(xn, w_ref[...], preferred_element_type=F32) * scale).astype(BF16)
        return
    acc = jnp.dot(xn_ref[...], w_ref[...], preferred_element_type=F32)
    if kind == "scale_bf":
        refs[2][...] = (acc * scale).astype(BF16)
    elif kind == "f32_bf":
        hd = refs[2].shape[2]
        for h in range(refs[2].shape[1]):
            refs[2][:, h, :] = acc[:, h * hd:(h + 1) * hd]
        refs[3][...] = acc.astype(BF16)
    elif kind == "sigmoid_bf":
        refs[2][...] = jax.nn.sigmoid(acc).astype(BF16)
    elif kind == "logsig":
        z = acc + refs[2][...]
        val = jnp.minimum(z, 0.0) - jnp.log1p(jnp.exp(-jnp.abs(z)))
        refs[3][...] = val[:, :refs[3].shape[1]]
    elif kind in ("rope_bf", "rope_f32_bf"):
        c, s1, s2 = refs[2][...], refs[3][...], refs[4][...]
        for j in range(acc.shape[1] // LANES):
            sl = slice(j * LANES, (j + 1) * LANES)
            r = _rope_slab(acc[:, sl], c, s1, s2)
            if kind == "rope_bf":
                refs[5][:, sl] = (r * scale).astype(BF16)
            else:
                per_head = refs[5].shape[2] // LANES
                refs[5][:, j // per_head, (j % per_head) * LANES:(j % per_head + 1) * LANES] = r
                refs[6][:, sl] = r.astype(BF16)
    else:
        raise ValueError(kind)


def _proj(xn, w, kind, *, scale=1.0, tables=None, bias=None, out_cols=None, head_dim=None):
    n, d = xn.shape
    c = w.shape[1]
    tm = _tile(n, 1024)
    tn = _tile(c, 1024)
    grid = (c // tn, n // tm)
    in_specs = [pl.BlockSpec((tm, d), lambda j, i: (i, 0)),
                pl.BlockSpec((d, tn), lambda j, i: (0, j))]
    args = [xn, w]
    blk = pl.BlockSpec((tm, tn), lambda j, i: (i, j))
    if head_dim is not None:
        assert tn == c and c % head_dim == 0 and head_dim % LANES == 0
        heads_shape = jax.ShapeDtypeStruct((n, c // head_dim, head_dim), F32)
        heads_blk = pl.BlockSpec((tm, c // head_dim, head_dim), lambda j, i: (i, 0, 0))
    if kind == "scale_bf" or kind == "sigmoid_bf":
        out_shape = [jax.ShapeDtypeStruct((n, c), BF16)]
        out_specs = [blk]
    elif kind == "norm_scale_bf":
        assert tn == c
        in_specs.append(pl.BlockSpec((1, d), lambda j, i: (0, 0)))
        args.append(bias)
        out_shape = [jax.ShapeDtypeStruct((n, d), BF16), jax.ShapeDtypeStruct((n, c), BF16)]
        out_specs = [pl.BlockSpec((tm, d), lambda j, i: (i, 0)), blk]
    elif kind == "f32_bf":
        out_shape = [heads_shape, jax.ShapeDtypeStruct((n, c), BF16)]
        out_specs = [heads_blk, blk]
    elif kind == "logsig":
        in_specs.append(pl.BlockSpec((1, tn), lambda j, i: (0, j)))
        args.append(bias)
        out_shape = [jax.ShapeDtypeStruct((n, out_cols), F32)]
        out_specs = [pl.BlockSpec((tm, out_cols), lambda j, i: (i, 0))]
    else:
        nt = tables[0].shape[0] // tm
        for t in tables:
            in_specs.append(pl.BlockSpec((tm, LANES), lambda j, i, nt=nt: (i % nt, 0)))
            args.append(t)
        if kind == "rope_bf":
            out_shape = [jax.ShapeDtypeStruct((n, c), BF16)]
            out_specs = [blk]
        else:
            out_shape = [heads_shape, jax.ShapeDtypeStruct((n, c), BF16)]
            out_specs = [heads_blk, blk]
    return pl.pallas_call(
        functools.partial(_proj_kernel, kind=kind, scale=scale),
        out_shape=out_shape,
        grid=grid,
        in_specs=in_specs,
        out_specs=out_specs,
        compiler_params=_cparams(("parallel", "parallel")),
        name="proj_" + kind,
    )(*args)


def _cumsum_kernel(x_ref, o_ref):
    x = x_ref[...]
    r = x.shape[0]
    li = lax.broadcasted_iota(jnp.int32, (LANES, LANES), 0)
    lj = lax.broadcasted_iota(jnp.int32, (LANES, LANES), 1)
    upper = (li <= lj).astype(F32)
    within = jnp.dot(x, upper, preferred_element_type=F32, precision=lax.Precision.HIGHEST)
    tot = jnp.broadcast_to(within[:, LANES - 1:LANES], (r, LANES))
    ri = lax.broadcasted_iota(jnp.int32, (r, r), 0)
    rj = lax.broadcasted_iota(jnp.int32, (r, r), 1)
    strict = (rj < ri).astype(F32)
    off = jnp.dot(strict, tot, preferred_element_type=F32, precision=lax.Precision.HIGHEST)
    o_ref[...] = within + off


def _cumsum(x):
    g, s = x.shape
    r = s // LANES
    out = pl.pallas_call(
        _cumsum_kernel,
        out_shape=jax.ShapeDtypeStruct((g, r, LANES), F32),
        grid=(g,),
        in_specs=[pl.BlockSpec((None, r, LANES), lambda i: (i, 0, 0))],
        out_specs=pl.BlockSpec((None, r, LANES), lambda i: (i, 0, 0)),
        compiler_params=_cparams(("parallel",)),
        name="cumsum",
    )(x.reshape(g, r, LANES))
    return out.reshape(g, s)


def _qk(q, k):
    return lax.dot_general(q, k, (((1,), (1,)), ((), ())), preferred_element_type=F32)


def _lane_chunks(s):
    return [s[:, j * LANES:(j + 1) * LANES] for j in range(s.shape[1] // LANES)]


def _running_max(sj, m_s):
    smax = sj[0]
    for x in sj[1:]:
        smax = jnp.maximum(smax, x)
    m_prev = m_s[...]
    m_new = jnp.maximum(m_prev, jnp.max(smax, axis=-1, keepdims=True))
    m_s[...] = m_new
    return m_new, jnp.exp2(m_prev - m_new)


def _causal_tiles(t):
    tq = _tile(t, 1024)
    return tq, tq


def _fox_kernel(q_ref, k_ref, v_ref, c_ref, o_ref, m_s, acc_s, s_s, *, tq, tk):
    qi = pl.program_id(2)
    q = q_ref[...]
    ones = jnp.ones((tk, LANES), BF16)
    m_s[...] = jnp.full(m_s.shape, NEG, F32)
    acc_s[...] = jnp.zeros(acc_s.shape, F32)

    def scores(kj):
        off = pl.multiple_of(kj * tk, tk)
        return _qk(q, k_ref[pl.ds(off, tk), :]) - c_ref[:, pl.ds(off, tk)] * LOG2E

    def consume(s, kj, r0=0, rn=tq, c0=0, cn=tk):
        off = pl.multiple_of(kj * tk, tk)
        v_aug = jnp.concatenate([v_ref[pl.ds(off + c0, cn), :], ones[:cn]], axis=1)
        acc_r = acc_s.at[pl.ds(r0, rn)]
        sj = _lane_chunks(s)
        m_new, alpha = _running_max(sj, m_s.at[pl.ds(r0, rn)])
        p = jnp.concatenate([jnp.exp2(x - m_new).astype(BF16) for x in sj], axis=1)
        pv = jnp.dot(p, v_aug, preferred_element_type=F32)
        acc_r[...] = jnp.concatenate([alpha, alpha], axis=1) * acc_r[...] + pv

    nfull = qi
    s_s[...] = scores(0)

    def body(kj, carry):
        s = s_s[...]
        s_next = scores(kj + 1)
        consume(s, kj)
        s_s[...] = s_next
        return carry

    lax.fori_loop(0, nfull, body, 0)
    h = tq // 2
    s = s_s[...]
    keep = lax.broadcasted_iota(jnp.int32, (h, h), 1) <= lax.broadcasted_iota(jnp.int32, (h, h), 0)
    consume(jnp.where(keep, s[:h, :h], NEG), nfull, 0, h, 0, h)
    consume(s[h:, :h], nfull, h, h, 0, h)
    consume(jnp.where(keep, s[h:, h:], NEG), nfull, h, h, h, h)
    acc = acc_s[...]
    o_ref[...] = (acc[:, :HEAD_DIM] / acc[:, HEAD_DIM:]).astype(o_ref.dtype)


def _fox_prompt(q, k, v, c, b, t):
    n, w = q.shape
    h = w // HEAD_DIM
    tq, tk = _causal_tiles(t)
    assert tq == tk and (tq // 2) % LANES == 0
    nq = t // tq
    return pl.pallas_call(
        functools.partial(_fox_kernel, tq=tq, tk=tk),
        out_shape=jax.ShapeDtypeStruct((n, w), BF16),
        grid=(b, h, nq),
        in_specs=[pl.BlockSpec((tq, HEAD_DIM), lambda bi, hi, qi: (bi * nq + qi, hi)),
                  pl.BlockSpec((t, HEAD_DIM), lambda bi, hi, qi: (bi, hi)),
                  pl.BlockSpec((t, HEAD_DIM), lambda bi, hi, qi: (bi, hi)),
                  pl.BlockSpec((None, None, 1, c.shape[-1]), lambda bi, hi, qi: (bi, hi, 0, 0))],
        out_specs=pl.BlockSpec((tq, HEAD_DIM), lambda bi, hi, qi: (bi * nq + qi, hi)),
        scratch_shapes=[pltpu.VMEM((tq, LANES), F32), pltpu.VMEM((tq, 2 * HEAD_DIM), F32),
                        pltpu.VMEM((tq, tk), F32)],
        compiler_params=_cparams(("parallel", "parallel", "arbitrary")),
        name="fox_prompt",
    )(q, k, v, c)


def _diff_finish(acc1, l1, acc2, l2, lam, gain, out_scale):
    o = acc1 / l1 - lam * (acc2 / l2)
    o = o * lax.rsqrt(jnp.mean(o * o, axis=-1, keepdims=True) + SUBLN_EPS) * gain
    return o * out_scale


def _diff_kernel(lam_ref, q_ref, k_ref, v_ref, g_ref, o_ref, m1, l1, a1, s1, m2, l2, a2, s2, *, tq, tk, out_scale):
    qi = pl.program_id(2)
    state = ((0, m1, l1, a1, s1), (HEAD_DIM, m2, l2, a2, s2))
    for _, m_s, l_s, a_s, _ in state:
        m_s[...] = jnp.full(m_s.shape, NEG, F32)
        l_s[...] = jnp.zeros(l_s.shape, F32)
        a_s[...] = jnp.zeros(a_s.shape, F32)

    def scores(kj, lo):
        off = pl.multiple_of(kj * tk, tk)
        return _qk(q_ref[:, lo:lo + HEAD_DIM], k_ref[pl.ds(off, tk), lo:lo + HEAD_DIM])

    def consume(s, v, m_s, l_s, a_s, r0=0, rn=tq):
        l_r = l_s.at[pl.ds(r0, rn)]
        a_r = a_s.at[pl.ds(r0, rn)]
        sj = _lane_chunks(s)
        m_new, alpha = _running_max(sj, m_s.at[pl.ds(r0, rn)])
        pj = [jnp.exp2(x - m_new) for x in sj]
        psum = pj[0]
        for x in pj[1:]:
            psum = psum + x
        p = jnp.concatenate([x.astype(BF16) for x in pj], axis=1)
        l_r[...] = alpha * l_r[...] + psum
        a_r[...] = jnp.concatenate([alpha, alpha], axis=1) * a_r[...] + jnp.dot(p, v, preferred_element_type=F32)

    nfull = qi
    for lo, _, _, _, s_s in state:
        s_s[...] = scores(0, lo)

    def body(kj, carry):
        off = pl.multiple_of(kj * tk, tk)
        v = v_ref[pl.ds(off, tk), :]
        for lo, m_s, l_s, a_s, s_s in state:
            s = s_s[...]
            s_next = scores(kj + 1, lo)
            consume(s, v, m_s, l_s, a_s)
            s_s[...] = s_next
        return carry

    lax.fori_loop(0, nfull, body, 0)
    h = tq // 2
    off = pl.multiple_of(nfull * tk, tk)
    v_lo = v_ref[pl.ds(off, h), :]
    v_hi = v_ref[pl.ds(off + h, h), :]
    keep = ((lax.broadcasted_iota(jnp.int32, (h, h), 1) >> CHUNK_SHIFT)
            <= (lax.broadcasted_iota(jnp.int32, (h, h), 0) >> CHUNK_SHIFT))
    for lo, m_s, l_s, a_s, s_s in state:
        s = s_s[...]
        consume(jnp.where(keep, s[:h, :h], NEG), v_lo, m_s, l_s, a_s, 0, h)
        consume(s[h:, :h], v_lo, m_s, l_s, a_s, h, h)
        consume(jnp.where(keep, s[h:, h:], NEG), v_hi, m_s, l_s, a_s, h, h)
    o = _diff_finish(a1[...], jnp.sum(l1[...], axis=-1, keepdims=True),
                     a2[...], jnp.sum(l2[...], axis=-1, keepdims=True), lam_ref[0], g_ref[...], out_scale)
    o_ref[...] = o.astype(o_ref.dtype)


def _diff_prompt(lam, q, k, v, gain, b, t, out_scale):
    n, w = q.shape
    hw = 2 * HEAD_DIM
    h = w // hw
    tq, tk = _causal_tiles(t)
    assert tq == tk and (tq // 2) % LANES == 0 and (tq // 2) % CHUNK == 0
    nq = t // tq
    per_map = [pltpu.VMEM((tq, LANES), F32), pltpu.VMEM((tq, LANES), F32), pltpu.VMEM((tq, hw), F32),
               pltpu.VMEM((tq, tk), F32)]
    return pl.pallas_call(
        functools.partial(_diff_kernel, tq=tq, tk=tk, out_scale=out_scale),
        out_shape=jax.ShapeDtypeStruct((n, w), BF16),
        grid_spec=pltpu.PrefetchScalarGridSpec(
            num_scalar_prefetch=1,
            grid=(b, h, nq),
            in_specs=[pl.BlockSpec((tq, hw), lambda bi, hi, qi, lam: (bi * nq + qi, hi)),
                      pl.BlockSpec((t, hw), lambda bi, hi, qi, lam: (bi, hi)),
                      pl.BlockSpec((t, hw), lambda bi, hi, qi, lam: (bi, hi)),
                      pl.BlockSpec((1, hw), lambda bi, hi, qi, lam: (0, 0))],
            out_specs=pl.BlockSpec((tq, hw), lambda bi, hi, qi, lam: (bi * nq + qi, hi)),
            scratch_shapes=per_map + per_map),
        compiler_params=_cparams(("parallel", "parallel", "arbitrary")),
        name="diff_prompt",
    )(lam, q, k, v, gain)


def _fox_decode_kernel(q_ref, kc_ref, vc_ref, kn_ref, vn_ref, c_ref, o_ref, *, past, tq):
    q = q_ref[...]
    s_c = _qk(q, kc_ref[...].astype(BF16)) - c_ref[:, :past] * LOG2E
    c_new = c_ref[:, past:past + LANES]
    s_n = _qk(q, kn_ref[...]) - c_new[:, :tq] * LOG2E
    row = lax.broadcasted_iota(jnp.int32, (tq, tq), 0)
    col = lax.broadcasted_iota(jnp.int32, (tq, tq), 1)
    s_n = jnp.where(col <= row, s_n, NEG)
    m = jnp.maximum(jnp.max(s_c, axis=-1, keepdims=True), jnp.max(s_n, axis=-1, keepdims=True))
    p_c = jnp.exp2(s_c - m)
    p_n = jnp.exp2(s_n - m)
    l = jnp.sum(p_c, axis=-1, keepdims=True) + jnp.sum(p_n, axis=-1, keepdims=True)
    o = (jnp.dot(p_c.astype(BF16), vc_ref[...].astype(BF16), preferred_element_type=F32)
         + jnp.dot(p_n.astype(BF16), vn_ref[...], preferred_element_type=F32))
    o_ref[...] = (o / l).astype(o_ref.dtype)


def _fox_decode(q, kc, vc, kn, vn, c, b, t, past):
    n, w = q.shape
    h = w // HEAD_DIM
    assert past % LANES == 0 and t <= LANES
    spad = c.shape[-1]
    new = pl.BlockSpec((t, HEAD_DIM), lambda bi, hi: (bi, hi))
    old = pl.BlockSpec((past, HEAD_DIM), lambda bi, hi: (bi, hi))
    return pl.pallas_call(
        functools.partial(_fox_decode_kernel, past=past, tq=t),
        out_shape=jax.ShapeDtypeStruct((n, w), BF16),
        grid=(b, h),
        in_specs=[new, old, old, new, new,
                  pl.BlockSpec((None, None, 1, spad), lambda bi, hi: (bi, hi, 0, 0))],
        out_specs=new,
        compiler_params=_cparams(("parallel", "parallel")),
        name="fox_decode",
    )(q, kc, vc, kn, vn, c)


def _diff_decode_kernel(lam_ref, q_ref, kc_ref, vc_ref, kn_ref, vn_ref, g_ref, o_ref, *, past, tq, out_scale):
    vc = vc_ref[...].astype(BF16)
    vn = vn_ref[...]
    row = past + lax.broadcasted_iota(jnp.int32, (tq, tq), 0)
    col = past + lax.broadcasted_iota(jnp.int32, (tq, tq), 1)
    keep = (col >> CHUNK_SHIFT) <= (row >> CHUNK_SHIFT)
    accs, ls = [], []
    for lo in (0, HEAD_DIM):
        q = q_ref[:, lo:lo + HEAD_DIM]
        s_c = _qk(q, kc_ref[:, lo:lo + HEAD_DIM].astype(BF16))
        s_n = jnp.where(keep, _qk(q, kn_ref[:, lo:lo + HEAD_DIM]), NEG)
        m = jnp.maximum(jnp.max(s_c, axis=-1, keepdims=True), jnp.max(s_n, axis=-1, keepdims=True))
        p_c = jnp.exp2(s_c - m)
        p_n = jnp.exp2(s_n - m)
        ls.append(jnp.sum(p_c, axis=-1, keepdims=True) + jnp.sum(p_n, axis=-1, keepdims=True))
        accs.append(jnp.dot(p_c.astype(BF16), vc, preferred_element_type=F32)
                    + jnp.dot(p_n.astype(BF16), vn, preferred_element_type=F32))
    o = _diff_finish(accs[0], ls[0], accs[1], ls[1], lam_ref[0], g_ref[...], out_scale)
    o_ref[...] = o.astype(o_ref.dtype)


def _diff_decode(lam, q, kc, vc, kn, vn, gain, b, t, past, out_scale):
    n, w = q.shape
    hw = 2 * HEAD_DIM
    h = w // hw
    assert (past - 1) // CHUNK <= past // CHUNK
    new = pl.BlockSpec((t, hw), lambda bi, hi, lam: (bi, hi))
    old = pl.BlockSpec((past, hw), lambda bi, hi, lam: (bi, hi))
    return pl.pallas_call(
        functools.partial(_diff_decode_kernel, past=past, tq=t, out_scale=out_scale),
        out_shape=jax.ShapeDtypeStruct((n, w), BF16),
        grid_spec=pltpu.PrefetchScalarGridSpec(
            num_scalar_prefetch=1,
            grid=(b, h),
            in_specs=[new, old, old, new, new,
                      pl.BlockSpec((1, hw), lambda bi, hi, lam: (0, 0))],
            out_specs=new),
        compiler_params=_cparams(("parallel", "parallel")),
        name="diff_decode",
    )(lam, q, kc, vc, kn, vn, gain)


def _merge_kernel(of_ref, od_ref, gf_ref, gd_ref, wof_ref, wod_ref, o_ref):
    yf = jnp.dot(of_ref[...], wof_ref[...], preferred_element_type=F32)
    yd = jnp.dot(od_ref[...], wod_ref[...], preferred_element_type=F32)
    o_ref[...] = (gf_ref[...].astype(F32) * yf + gd_ref[...].astype(F32) * yd).astype(o_ref.dtype)


def _merge(o_fox, o_diff, gates, w_o_fox, w_o_diff):
    n = o_fox.shape[0]
    d = w_o_fox.shape[1]
    tm = _tile(n, 512)
    row = lambda c: pl.BlockSpec((tm, c), lambda i: (i, 0))
    full = lambda a: pl.BlockSpec(a.shape, lambda i: (0, 0))
    return pl.pallas_call(
        _merge_kernel,
        out_shape=jax.ShapeDtypeStruct((n, d), BF16),
        grid=(n // tm,),
        in_specs=[row(o_fox.shape[1]), row(o_diff.shape[1]),
                  pl.BlockSpec((tm, d), lambda i: (i, 0)), pl.BlockSpec((tm, d), lambda i: (i, 1)),
                  full(w_o_fox), full(w_o_diff)],
        out_specs=row(d),
        compiler_params=_cparams(("parallel",)),
        name="merge",
    )(o_fox, o_diff, gates, gates, w_o_fox, w_o_diff)


def _mix_kernel(m_ref, x_ref, wout_ref, nrm_ref, wr_ref, br_ref, h_ref, xn_ref, ridx_ref, rw_ref, *, n_experts):
    h = x_ref[...] + jnp.dot(m_ref[...], wout_ref[...], preferred_element_type=F32)
    h_ref[...] = h
    xf = h * lax.rsqrt(jnp.mean(h * h, axis=-1, keepdims=True) + NORM_EPS) * nrm_ref[...]
    xn_ref[...] = _pack_halves(xf)
    xh = xf.astype(BF16)
    xl = (xf - xh.astype(F32)).astype(BF16)
    wr = wr_ref[...]
    both = jnp.dot(xh, wr, preferred_element_type=F32)
    logits = (both[:, :LANES] + both[:, LANES:] + jnp.dot(xl, wr[:, :LANES], preferred_element_type=F32)
              + br_ref[...])
    lane = lax.broadcasted_iota(jnp.int32, logits.shape, 1).astype(F32)
    cur = jnp.where(lane < n_experts, logits, -jnp.inf)
    vals, idxs = [], []
    for _ in range(TOP_K):
        mk = jnp.max(cur, axis=-1, keepdims=True)
        ik = jnp.min(jnp.where(cur == mk, lane, float(LANES)), axis=-1, keepdims=True)
        cur = jnp.where(lane == ik, -jnp.inf, cur)
        vals.append(mk)
        idxs.append(ik)
    es = [jnp.exp(vk - vals[0]) for vk in vals]
    den = es[0] + es[1] + es[2] + es[3]
    ridx = jnp.zeros(logits.shape, F32)
    rw = jnp.zeros(logits.shape, F32)
    for k in range(TOP_K):
        ridx = jnp.where(lane == float(k), idxs[k], ridx)
        rw = jnp.where(lane == float(k), es[k] / den, rw)
    ridx_ref[...] = ridx.astype(jnp.int32)
    rw_ref[...] = rw


def _mix(merged, x, w_out, norm_ffn, w_router, b_router):
    n, d = x.shape
    e = w_router.shape[1]
    tm = _tile(n, 512)
    w_pad = _pad_cols(w_router, LANES)
    w_hi = _top_bits(w_pad)
    wr = jnp.concatenate([w_hi, _top_bits(w_pad - w_hi)], axis=1).astype(BF16)
    br = _pad_cols(b_router.reshape(1, e), LANES)
    row = lambda c: pl.BlockSpec((tm, c), lambda i: (i, 0))
    full = lambda a: pl.BlockSpec(a.shape, lambda i: (0, 0), pipeline_mode=pl.Buffered(1))
    args = [merged, x, w_out, norm_ffn.reshape(1, d), wr, br]
    in_specs = [row(d), row(d), full(w_out), full(args[3]), full(wr), full(br)]
    return pl.pallas_call(
        functools.partial(_mix_kernel, n_experts=e),
        out_shape=[jax.ShapeDtypeStruct((n, d), F32), jax.ShapeDtypeStruct((n, d // 2), jnp.uint32),
                   jax.ShapeDtypeStruct((n, LANES), jnp.int32), jax.ShapeDtypeStruct((n, LANES), F32)],
        grid=(n // tm,),
        in_specs=in_specs,
        out_specs=[row(d), row(d // 2), row(LANES), row(LANES)],
        compiler_params=_cparams(("parallel",)),
        name="mix",
    )(*args)


def _moe_kernel(be_ref, nact_ref, nv_ref, tok_hbm, slot_hbm, x_hbm, wg_ref, bg_ref, wu_ref, bu_ref, wd_ref,
                bd_ref, out_hbm, xbuf, xb, tok_s, slot_s, acc, ostage, gsem, ssem, isem, *, tb, nft, n_real):
    blk = pl.program_id(0)
    ft = pl.program_id(1)
    nact = nact_ref[0]
    active = blk < nact
    cur = lax.rem(blk, 2)
    nxt = 1 - cur
    per_step = tb // nft

    def tok_copy(b, s):
        return pltpu.make_async_copy(tok_hbm.at[pl.ds(b, 1)], tok_s.at[pl.ds(s, 1)], isem.at[0])

    def slot_copy(b, s):
        return pltpu.make_async_copy(slot_hbm.at[pl.ds(b, 1)], slot_s.at[pl.ds(s, 1)], isem.at[1])

    def gather_row(s, r):
        t = tok_s[s, r]
        pltpu.make_async_copy(x_hbm.at[pl.ds(t, 1)], xbuf.at[s, pl.ds(r, 1)], gsem.at[s]).start()

    def scatter_row(s, r):
        d = slot_s[s, r]
        pltpu.make_async_copy(ostage.at[s, pl.ds(r, 1)], out_hbm.at[pl.ds(d, 1)], ssem.at[s]).start(priority=1)

    def wait_gather(s):
        pltpu.make_async_copy(xbuf.at[s], xbuf.at[s], gsem.at[s]).wait()

    def wait_scatter(s):
        pltpu.make_async_copy(ostage.at[s], ostage.at[s], ssem.at[s]).wait()

    def loop_rows(fn, s):
        def body(r, carry):
            fn(s, r)
            return carry
        lax.fori_loop(0, tb, body, 0)

    @pl.when((blk == 0) & (ft == 0))
    def _():
        ostage[1] = jnp.zeros(ostage.shape[1:], ostage.dtype)

        def spare(r, carry):
            slot_s[1, r] = n_real + r
            return carry
        lax.fori_loop(0, tb, spare, 0)
        for b in range(2):
            c = tok_copy(b, b)
            c.start()
            c.wait()
        loop_rows(gather_row, 0)

    @pl.when((ft == 0) & (blk >= 1) & (blk <= nact))
    def _():
        tok_copy(blk + 1, nxt).wait()
        slot_copy(blk - 1, nxt).wait()

    @pl.when((ft == 0) & (blk <= nact))
    def _():
        wait_gather(cur)

    @pl.when((ft == 0) & (blk >= 1) & (blk - 2 < nact))
    def _():
        wait_scatter(cur)

    @pl.when((ft == 0) & active)
    def _():
        tok_copy(blk + 2, cur).start()
        slot_copy(blk, cur).start()
        lo, hi = _unpack_halves(xbuf[cur])
        xb[...] = jnp.concatenate([lo.astype(BF16), hi.astype(BF16)], axis=1)
        acc[...] = jnp.broadcast_to(bd_ref[...], acc.shape)

    @pl.when((ft == 0) & (blk == nact) & (blk >= 1))
    def _():
        loop_rows(scatter_row, nxt)

    def compute(rows):
        base = ft * per_step
        for j in range(per_step):
            gather_row(nxt, base + j)
        for j in range(per_step):
            scatter_row(nxt, base + j)
        x = xb[:rows]
        g = jnp.dot(x, wg_ref[...], preferred_element_type=F32) + bg_ref[...]
        u = jnp.dot(x, wu_ref[...], preferred_element_type=F32) + bu_ref[...]
        g = jnp.minimum(g, SWIGLU_LIMIT)
        u = jnp.clip(u, -SWIGLU_LIMIT, SWIGLU_LIMIT)
        act = (u + 1.0) * (g * jax.nn.sigmoid(SWIGLU_ALPHA * g))
        acc[:rows] += jnp.dot(act.astype(BF16), wd_ref[...], preferred_element_type=F32)

    half_full = nv_ref[blk] <= tb // 2

    @pl.when(active & half_full)
    def _():
        compute(tb // 2)

    @pl.when(active & jnp.logical_not(half_full))
    def _():
        compute(tb)

    @pl.when(active & (ft == nft - 1))
    def _():
        ostage[cur] = _pack_halves(acc[...])


def _moe(block_exp, nact, nvalid, row_tok, row_slot, xn, w_gate, b_gate, w_up, b_up, w_down, b_down, n_real):
    nb, tb = row_tok.shape
    e, d, f = w_gate.shape
    tf = _tile(f, 1024)
    nft = f // tf
    d2 = d // 2
    assert nft >= 2 and tb % nft == 0 and xn.shape[1] == d2

    def ftile(b, j, be, na, nv):
        return jnp.where(b < na[0], j, nft - 1)

    any_spec = pl.BlockSpec(memory_space=pl.ANY)
    in_specs = [any_spec, any_spec, any_spec,
                pl.BlockSpec((None, d, tf), lambda b, j, be, na, nv: (be[b], 0, ftile(b, j, be, na, nv))),
                pl.BlockSpec((None, 1, tf), lambda b, j, be, na, nv: (be[b], 0, ftile(b, j, be, na, nv))),
                pl.BlockSpec((None, d, tf), lambda b, j, be, na, nv: (be[b], 0, ftile(b, j, be, na, nv))),
                pl.BlockSpec((None, 1, tf), lambda b, j, be, na, nv: (be[b], 0, ftile(b, j, be, na, nv))),
                pl.BlockSpec((None, tf, d), lambda b, j, be, na, nv: (be[b], ftile(b, j, be, na, nv), 0)),
                pl.BlockSpec((None, 1, d), lambda b, j, be, na, nv: (be[b], 0, 0))]
    return pl.pallas_call(
        functools.partial(_moe_kernel, tb=tb, nft=nft, n_real=n_real),
        out_shape=jax.ShapeDtypeStruct((n_real + tb, d2), jnp.uint32),
        grid_spec=pltpu.PrefetchScalarGridSpec(
            num_scalar_prefetch=3,
            grid=(nb, nft),
            in_specs=in_specs,
            out_specs=any_spec,
            scratch_shapes=[pltpu.VMEM((2, tb, d2), jnp.uint32), pltpu.VMEM((tb, d), BF16),
                            pltpu.SMEM((2, tb), jnp.int32), pltpu.SMEM((2, tb), jnp.int32),
                            pltpu.VMEM((tb, d), F32), pltpu.VMEM((2, tb, d2), jnp.uint32),
                            pltpu.SemaphoreType.DMA((2,)), pltpu.SemaphoreType.DMA((2,)),
                            pltpu.SemaphoreType.DMA((2,))]),
        compiler_params=_cparams(("arbitrary", "arbitrary")),
        name="moe",
    )(block_exp, nact, nvalid, row_tok, row_slot, xn, w_gate, b_gate.reshape(e, 1, f), w_up,
      b_up.reshape(e, 1, f), w_down, b_down.reshape(e, 1, d))


def _routing(top_idx, n_experts, tb):
    n = top_idx.shape[0]
    a = n * TOP_K
    expert = top_idx.reshape(-1)
    order = jnp.argsort(expert).astype(jnp.int32)
    counts = jnp.sum(expert[:, None] == jnp.arange(n_experts, dtype=jnp.int32)[None, :], axis=0, dtype=jnp.int32)
    padded = (counts + tb - 1) // tb * tb
    start = jnp.cumsum(counts) - counts
    pend = jnp.cumsum(padded)
    pstart = pend - padded
    nb = -(-a // tb) + n_experts + 1
    row0 = jnp.arange(nb, dtype=jnp.int32) * tb
    block_exp = jnp.minimum(jnp.sum(pend[None, :] <= row0[:, None], axis=1), n_experts - 1).astype(jnp.int32)
    nvalid = jnp.clip(pstart[block_exp] + counts[block_exp] - row0, 0, tb)
    src0 = jnp.clip(start[block_exp] + row0 - pstart[block_exp], 0, a)
    order_pad = jnp.concatenate([order, jnp.zeros((tb,), jnp.int32)])
    r = jnp.arange(tb, dtype=jnp.int32)[None, :]
    win = order_pad[src0[:, None] + r]
    valid = r < nvalid[:, None]
    tok = win // TOP_K
    row_tok = jnp.where(valid, tok, 0)
    row_slot = jnp.where(valid, (win % TOP_K) * n + tok, a + r)
    nact = (pend[-1] // tb).reshape(1).astype(jnp.int32)
    return block_exp, nact, nvalid.astype(jnp.int32), row_tok.astype(jnp.int32), row_slot.astype(jnp.int32)


def _ple_kernel(h_ref, s0_ref, s1_ref, s2_ref, s3_ref, rw_ref, pe_ref, nple_ref, wg_ref, wp_ref, nfin_ref,
                o_ref, *, final):
    h = h_ref[...]
    rw = rw_ref[...]
    for k, s_ref in enumerate((s0_ref, s1_ref, s2_ref, s3_ref)):
        h = h + rw[:, k:k + 1] * jnp.concatenate(_unpack_halves(s_ref[...]), axis=1)
    xn = h * lax.rsqrt(jnp.mean(h * h, axis=-1, keepdims=True) + NORM_EPS) * nple_ref[...]
    gate = jax.nn.sigmoid(jnp.dot(xn.astype(BF16), wg_ref[...], preferred_element_type=F32))
    h = h + gate * jnp.dot(pe_ref[...].astype(BF16), wp_ref[...], preferred_element_type=F32)
    if final:
        h = h * lax.rsqrt(jnp.mean(h * h, axis=-1, keepdims=True) + NORM_EPS) * nfin_ref[...]
    o_ref[...] = h


def _ple(h, slots, n_all, tok0, rw, pe, norm_ple, w_ple_gate, w_ple_proj, norm_final, final):
    n, d = h.shape
    tm = _tile(n, 256)
    assert TOP_K == 4 and n_all % tm == 0 and tok0 % tm == 0
    row = lambda c: pl.BlockSpec((tm, c), lambda i: (i, 0))
    full = lambda a: pl.BlockSpec(a.shape, lambda i: (0, 0))
    slot = lambda k: pl.BlockSpec((tm, d // 2), lambda i, t0=(k * n_all + tok0) // tm: (t0 + i, 0))
    g1 = norm_ple.reshape(1, d)
    g2 = norm_final.reshape(1, d)
    return pl.pallas_call(
        functools.partial(_ple_kernel, final=final),
        out_shape=jax.ShapeDtypeStruct((n, d), F32),
        grid=(n // tm,),
        in_specs=[row(d), slot(0), slot(1), slot(2), slot(3), row(LANES), row(pe.shape[1]),
                  full(g1), full(w_ple_gate), full(w_ple_proj), full(g2)],
        out_specs=row(d),
        compiler_params=_cparams(("parallel",)),
        name="ple",
    )(h, slots, slots, slots, slots, rw, pe, g1, w_ple_gate, w_ple_proj, g2)


def _rope_tables(pos):
    half = ROPE_DIM // 2
    inv_freq = ROPE_THETA ** (-2.0 * jnp.arange(half, dtype=F32) / ROPE_DIM)
    ang = pos.astype(F32)[:, None] * inv_freq
    cos, sin = jnp.cos(ang), jnp.sin(ang)
    t = pos.shape[0]
    one = jnp.ones((t, LANES - ROPE_DIM), F32)
    zero = jnp.zeros((t, LANES - ROPE_DIM), F32)
    zh = jnp.zeros((t, half), F32)
    c = jnp.concatenate([cos, cos, one], axis=1)
    s1 = jnp.concatenate([zh, sin, zero], axis=1)
    s2 = jnp.concatenate([-sin, zh, zero], axis=1)
    return c, s1, s2


def _pad_cols(w, mult):
    c = w.shape[1]
    cp = -(-c // mult) * mult
    return jnp.pad(w, ((0, 0), (0, cp - c)))


def _layer_weights(w_in, b_forget, d):
    h_f = d // (2 * HEAD_DIM)
    fw = h_f * HEAD_DIM
    dw = fw
    o = 0
    seg = {}
    for name, width in (("fq", fw), ("fk", fw), ("fv", fw), ("fl", h_f), ("dq", dw), ("dk", dw), ("dv", dw),
                        ("gates", 2 * d)):
        seg[name] = w_in[:, o:o + width]
        o += width
    assert o == w_in.shape[1]
    out = {k: v.astype(BF16) for k, v in seg.items() if k != "fl"}
    out["fl"] = _pad_cols(seg["fl"], LANES).astype(BF16)
    out["fl_bias"] = _pad_cols(b_forget.reshape(1, h_f), LANES)
    return out, h_f


def _project(x, norm_mix, pw, tables, h_f):
    qscale = LOG2E * HEAD_DIM ** -0.5
    xn, fq = _proj(x, pw["fq"], "norm_scale_bf", scale=qscale, bias=norm_mix.reshape(1, -1))
    fk, fk_b = _proj(xn, pw["fk"], "f32_bf", head_dim=HEAD_DIM)
    fv, fv_b = _proj(xn, pw["fv"], "f32_bf", head_dim=HEAD_DIM)
    logf, = _proj(xn, pw["fl"], "logsig", bias=pw["fl_bias"], out_cols=h_f)
    dq, = _proj(xn, pw["dq"], "rope_bf", scale=qscale, tables=tables)
    dk, dk_b = _proj(xn, pw["dk"], "rope_f32_bf", tables=tables, head_dim=2 * HEAD_DIM)
    dv, dv_b = _proj(xn, pw["dv"], "f32_bf", head_dim=2 * HEAD_DIM)
    gates, = _proj(xn, pw["gates"], "sigmoid_bf")
    return dict(fq=fq, fk=fk, fk_b=fk_b, fv=fv, fv_b=fv_b, logf=logf, dq=dq, dk=dk, dk_b=dk_b,
                dv=dv, dv_b=dv_b, gates=gates)


def _cum_logf(logf_bth):
    b, s, h = logf_bth.shape
    spad = -(-s // 1024) * 1024
    x = jnp.pad(jnp.swapaxes(logf_bth, 1, 2), ((0, 0), (0, 0), (0, spad - s)))
    return _cumsum(x.reshape(b * h, spad)).reshape(b, h, 1, spad)


def kernel(x_prompt, x_sample, cache_fox_k, cache_fox_v, cache_fox_logf, cache_diff_k, cache_diff_v,
           p_prompt, p_sample, norm_mix, w_in, b_forget, lambda_q1, lambda_k1, lambda_q2, lambda_k2,
           diff_subln, w_o_fox, w_o_diff, w_out, norm_ffn, w_router, b_router, w_gate, b_gate,
           w_up, b_up, w_down, b_down, norm_ple, w_ple_gate, w_ple_proj, norm_final):
    depth = w_in.shape[0]
    bp, tp, d = x_prompt.shape
    bs, ts, _ = x_sample.shape
    past = cache_fox_k.shape[2]
    n_p, n_s = bp * tp, bs * ts
    n_experts = w_router.shape[-1]
    tb = 512

    h_p = x_prompt.reshape(n_p, d)
    h_s = x_sample.reshape(n_s, d)
    tab_p = _rope_tables(jnp.arange(tp, dtype=jnp.int32))
    tab_s = tuple(jnp.tile(t, (bs, 1)) for t in _rope_tables(past + jnp.arange(ts, dtype=jnp.int32)))
    st_p, st_s = [], []
    for i in range(depth):
        lam_init = 0.8 - 0.6 * math.exp(-0.3 * i)
        lam = (jnp.exp(jnp.sum(lambda_q1[i].astype(F32) * lambda_k1[i].astype(F32)))
               - jnp.exp(jnp.sum(lambda_q2[i].astype(F32) * lambda_k2[i].astype(F32)))
               + lam_init).reshape(1).astype(F32)
        out_scale = 1.0 - lam_init
        pw, h_f = _layer_weights(w_in[i], b_forget[i], d)
        h_d = h_f // 2
        subln = diff_subln[i].reshape(1, 2 * HEAD_DIM)
        wof, wod, wo = w_o_fox[i].astype(BF16), w_o_diff[i].astype(BF16), w_out[i].astype(BF16)
        wg, wu, wd = w_gate[i].astype(BF16), w_up[i].astype(BF16), w_down[i].astype(BF16)
        wpg, wpp = w_ple_gate[i].astype(BF16), w_ple_proj[i].astype(BF16)

        pr = _project(h_p, norm_mix[i], pw, tab_p, h_f)
        c_p = _cum_logf(pr["logf"].reshape(bp, tp, h_f))
        o_fox_p = _fox_prompt(pr["fq"], pr["fk_b"], pr["fv_b"], c_p, bp, tp)
        o_diff_p = _diff_prompt(lam, pr["dq"], pr["dk_b"], pr["dv_b"], subln, bp, tp, out_scale)
        h1_p, xn_p, ridx_p, rw_p = _mix(_merge(o_fox_p, o_diff_p, pr["gates"], wof, wod), h_p, wo,
                                        norm_ffn[i], w_router[i], b_router[i])

        sr = _project(h_s, norm_mix[i], pw, tab_s, h_f)
        logf_all = jnp.concatenate([cache_fox_logf[i].astype(F32), sr["logf"].reshape(bs, ts, h_f)], axis=1)
        c_s = _cum_logf(logf_all)
        o_fox_s = _fox_decode(sr["fq"], cache_fox_k[i].reshape(bs * past, -1), cache_fox_v[i].reshape(bs * past, -1),
                              sr["fk_b"], sr["fv_b"], c_s, bs, ts, past)
        o_diff_s = _diff_decode(lam, sr["dq"], cache_diff_k[i].reshape(bs * past, -1),
                                cache_diff_v[i].reshape(bs * past, -1), sr["dk_b"], sr["dv_b"], subln,
                                bs, ts, past, out_scale)
        h1_s, xn_s, ridx_s, rw_s = _mix(_merge(o_fox_s, o_diff_s, sr["gates"], wof, wod), h_s, wo,
                                        norm_ffn[i], w_router[i], b_router[i])

        xn_all = jnp.concatenate([xn_p, xn_s], axis=0)
        top_idx = jnp.concatenate([ridx_p[:, :TOP_K], ridx_s[:, :TOP_K]], axis=0)
        block_exp, nact, nvalid, row_tok, row_slot = _routing(top_idx, n_experts, tb)
        n_all = n_p + n_s
        slots = _moe(block_exp, nact, nvalid, row_tok, row_slot, xn_all, wg, b_gate[i], wu, b_up[i], wd,
                     b_down[i], n_all * TOP_K)

        last = i == depth - 1
        h_p = _ple(h1_p, slots, n_all, 0, rw_p, p_prompt[i].reshape(n_p, -1), norm_ple[i], wpg, wpp,
                   norm_final, last)
        h_s = _ple(h1_s, slots, n_all, n_p, rw_s, p_sample[i].reshape(n_s, -1), norm_ple[i], wpg, wpp,
                   norm_final, last)

        st_p.append((pr["fk"].reshape(bp, tp, h_f, HEAD_DIM), pr["fv"].reshape(bp, tp, h_f, HEAD_DIM),
                     pr["logf"].reshape(bp, tp, h_f), pr["dk"].reshape(bp, tp, h_d, 2 * HEAD_DIM),
                     pr["dv"].reshape(bp, tp, h_d, 2 * HEAD_DIM)))
        st_s.append((sr["fk"].reshape(bs, ts, h_f, HEAD_DIM), sr["fv"].reshape(bs, ts, h_f, HEAD_DIM),
                     sr["logf"].reshape(bs, ts, h_f), sr["dk"].reshape(bs, ts, h_d, 2 * HEAD_DIM),
                     sr["dv"].reshape(bs, ts, h_d, 2 * HEAD_DIM)))

    y_prompt = h_p.reshape(bp, tp, d)
    y_sample = h_s.reshape(bs, ts, d)
    outs_p = [jnp.stack([s[j] for s in st_p]) for j in range(5)]
    outs_s = [jnp.stack([s[j] for s in st_s]) for j in range(5)]
    return (y_prompt, y_sample, *outs_p, *outs_s)
```

```python
import functools
import math

import jax
import jax.numpy as jnp
from jax import lax
from jax.experimental import pallas as pl
from jax.experimental.pallas import tpu as pltpu

F32 = jnp.float32
BF16 = jnp.bfloat16

HEAD_DIM = 128
CHUNK = 64
CHUNK_SHIFT = 6
assert 1 << CHUNK_SHIFT == CHUNK
ROPE_DIM = HEAD_DIM // 4
ROPE_THETA = 500000.0
TOP_K = 4
SWIGLU_LIMIT = 7.0
SWIGLU_ALPHA = 1.702
NORM_EPS = 1e-6
SUBLN_EPS = 1e-5
LOG2E = 1.4426950408889634
NEG = -1e30
LANES = 128
VMEM_LIMIT = 56 * 1024 * 1024


def _cparams(sem):
    return pltpu.CompilerParams(dimension_semantics=sem, vmem_limit_bytes=VMEM_LIMIT)


def _top_bits(v):
    bits = lax.bitcast_convert_type(v, jnp.uint32) & jnp.uint32(0xFFFF0000)
    return lax.bitcast_convert_type(bits, F32)


def _pack_halves(x):
    half = x.shape[1] // 2

    def rounded(v):
        bits = lax.bitcast_convert_type(v, jnp.uint32)
        return bits + jnp.uint32(0x7FFF) + ((bits >> 16) & jnp.uint32(1))

    return (rounded(x[:, half:]) & jnp.uint32(0xFFFF0000)) | (rounded(x[:, :half]) >> 16)


def _unpack_halves(w):
    lo = lax.bitcast_convert_type(w << 16, F32)
    hi = lax.bitcast_convert_type(w & jnp.uint32(0xFFFF0000), F32)
    return lo, hi


def _tile(n, pref):
    t = min(n, pref)
    assert n % t == 0, (n, pref)
    return t


def _rope_slab(x, c, s1, s2):
    return x * c + pltpu.roll(x, ROPE_DIM // 2, 1) * s1 + pltpu.roll(x, LANES - ROPE_DIM // 2, 1) * s2


def _proj_kernel(*refs, kind, scale):
    xn_ref, w_ref = refs[0], refs[1]
    if kind == "norm_scale_bf":
        x = xn_ref[...]
        xn = (x * lax.rsqrt(jnp.mean(x * x, axis=-1, keepdims=True) + NORM_EPS) * refs[2][...]).astype(BF16)
        refs[3][...] = xn
        refs[4][...] = (jnp.dot(xn, w_ref[...], preferred_element_type=F32) * scale).astype(BF16)
        return
    acc = jnp.dot(xn_ref[...], w_ref[...], preferred_element_type=F32)
    if kind == "scale_bf":
        refs[2][...] = (acc * scale).astype(BF16)
    elif kind == "f32_bf":
        refs[2][...] = acc.reshape(refs[2].shape)
        refs[3][...] = acc.astype(BF16)
    elif kind == "sigmoid_bf":
        refs[2][...] = jax.nn.sigmoid(acc).astype(BF16)
    elif kind == "logsig":
        z = acc + refs[2][...]
        val = jnp.minimum(z, 0.0) - jnp.log1p(jnp.exp(-jnp.abs(z)))
        refs[3][...] = val[:, :refs[3].shape[1]]
    elif kind in ("rope_bf", "rope_f32_bf"):
        c, s1, s2 = refs[2][...], refs[3][...], refs[4][...]
        r = jnp.concatenate([_rope_slab(acc[:, j * LANES:(j + 1) * LANES], c, s1, s2)
                             for j in range(acc.shape[1] // LANES)], axis=1)
        if kind == "rope_bf":
            refs[5][...] = (r * scale).astype(BF16)
        else:
            refs[5][...] = r.reshape(refs[5].shape)
            refs[6][...] = r.astype(BF16)
    else:
        raise ValueError(kind)


def _proj(xn, w, kind, *, scale=1.0, tables=None, bias=None, out_cols=None, head_dim=None):
    n, d = xn.shape
    c = w.shape[1]
    tm = _tile(n, 1024)
    tn = _tile(c, 1024)
    grid = (c // tn, n // tm)
    in_specs = [pl.BlockSpec((tm, d), lambda j, i: (i, 0)),
                pl.BlockSpec((d, tn), lambda j, i: (0, j))]
    args = [xn, w]
    blk = pl.BlockSpec((tm, tn), lambda j, i: (i, j))
    if head_dim is not None:
        assert tn == c and c % head_dim == 0 and head_dim % LANES == 0
        heads_shape = jax.ShapeDtypeStruct((n, c // head_dim, head_dim), F32)
        heads_blk = pl.BlockSpec((tm, c // head_dim, head_dim), lambda j, i: (i, 0, 0))
    if kind == "scale_bf" or kind == "sigmoid_bf":
        out_shape = [jax.ShapeDtypeStruct((n, c), BF16)]
        out_specs = [blk]
    elif kind == "norm_scale_bf":
        assert tn == c
        in_specs.append(pl.BlockSpec((1, d), lambda j, i: (0, 0)))
        args.append(bias)
        out_shape = [jax.ShapeDtypeStruct((n, d), BF16), jax.ShapeDtypeStruct((n, c), BF16)]
        out_specs = [pl.BlockSpec((tm, d), lambda j, i: (i, 0)), blk]
    elif kind == "f32_bf":
        out_shape = [heads_shape, jax.ShapeDtypeStruct((n, c), BF16)]
        out_specs = [heads_blk, blk]
    elif kind == "logsig":
        in_specs.append(pl.BlockSpec((1, tn), lambda j, i: (0, j)))
        args.append(bias)
        out_shape = [jax.ShapeDtypeStruct((n, out_cols), F32)]
        out_specs = [pl.BlockSpec((tm, out_cols), lambda j, i: (i, 0))]
    else:
        nt = tables[0].shape[0] // tm
        for t in tables:
            in_specs.append(pl.BlockSpec((tm, LANES), lambda j, i, nt=nt: (i % nt, 0)))
            args.append(t)
        if kind == "rope_bf":
            out_shape = [jax.ShapeDtypeStruct((n, c), BF16)]
            out_specs = [blk]
        else:
            out_shape = [heads_shape, jax.ShapeDtypeStruct((n, c), BF16)]
            out_specs = [heads_blk, blk]
    return pl.pallas_call(
        functools.partial(_proj_kernel, kind=kind, scale=scale),
        out_shape=out_shape,
        grid=grid,
        in_specs=in_specs,
        out_specs=out_specs,
        compiler_params=_cparams(("parallel", "parallel")),
        name="proj_" + kind,
    )(*args)


def _cumsum_kernel(x_ref, o_ref):
    x = x_ref[...]
    r = x.shape[0]
    li = lax.broadcasted_iota(jnp.int32, (LANES, LANES), 0)
    lj = lax.broadcasted_iota(jnp.int32, (LANES, LANES), 1)
    upper = (li <= lj).astype(F32)
    within = jnp.dot(x, upper, preferred_element_type=F32, precision=lax.Precision.HIGHEST)
    tot = jnp.broadcast_to(within[:, LANES - 1:LANES], (r, LANES))
    ri = lax.broadcasted_iota(jnp.int32, (r, r), 0)
    rj = lax.broadcasted_iota(jnp.int32, (r, r), 1)
    strict = (rj < ri).astype(F32)
    off = jnp.dot(strict, tot, preferred_element_type=F32, precision=lax.Precision.HIGHEST)
    o_ref[...] = within + off


def _cumsum(x):
    g, s = x.shape
    r = s // LANES
    out = pl.pallas_call(
        _cumsum_kernel,
        out_shape=jax.ShapeDtypeStruct((g, r, LANES), F32),
        grid=(g,),
        in_specs=[pl.BlockSpec((None, r, LANES), lambda i: (i, 0, 0))],
        out_specs=pl.BlockSpec((None, r, LANES), lambda i: (i, 0, 0)),
        compiler_params=_cparams(("parallel",)),
        name="cumsum",
    )(x.reshape(g, r, LANES))
    return out.reshape(g, s)


def _qk(q, k):
    return lax.dot_general(q, k, (((1,), (1,)), ((), ())), preferred_element_type=F32)


def _lane_chunks(s):
    return [s[:, j * LANES:(j + 1) * LANES] for j in range(s.shape[1] // LANES)]


def _running_max(sj, m_s):
    smax = sj[0]
    for x in sj[1:]:
        smax = jnp.maximum(smax, x)
    m_prev = m_s[...]
    m_new = jnp.maximum(m_prev, jnp.max(smax, axis=-1, keepdims=True))
    m_s[...] = m_new
    return m_new, jnp.exp2(m_prev - m_new)


def _causal_tiles(t):
    tq = _tile(t, 1024)
    return tq, tq


def _fox_kernel(q_ref, k_ref, v_ref, c_ref, o_ref, m_s, acc_s, s_s, *, tq, tk):
    qi = pl.program_id(2)
    q = q_ref[...]
    ones = jnp.ones((tk, LANES), BF16)
    m_s[...] = jnp.full(m_s.shape, NEG, F32)
    acc_s[...] = jnp.zeros(acc_s.shape, F32)

    def scores(kj):
        off = pl.multiple_of(kj * tk, tk)
        return _qk(q, k_ref[pl.ds(off, tk), :]) - c_ref[:, pl.ds(off, tk)] * LOG2E

    def consume(s, kj, r0=0, rn=tq, c0=0, cn=tk):
        off = pl.multiple_of(kj * tk, tk)
        v_aug = jnp.concatenate([v_ref[pl.ds(off + c0, cn), :], ones[:cn]], axis=1)
        acc_r = acc_s.at[pl.ds(r0, rn)]
        sj = _lane_chunks(s)
        m_new, alpha = _running_max(sj, m_s.at[pl.ds(r0, rn)])
        p = jnp.concatenate([jnp.exp2(x - m_new).astype(BF16) for x in sj], axis=1)
        pv = jnp.dot(p, v_aug, preferred_element_type=F32)
        acc_r[...] = jnp.concatenate([alpha, alpha], axis=1) * acc_r[...] + pv

    nfull = qi
    s_s[...] = scores(0)

    def body(kj, carry):
        s = s_s[...]
        s_next = scores(kj + 1)
        consume(s, kj)
        s_s[...] = s_next
        return carry

    lax.fori_loop(0, nfull, body, 0)
    h = tq // 2
    s = s_s[...]
    keep = lax.broadcasted_iota(jnp.int32, (h, h), 1) <= lax.broadcasted_iota(jnp.int32, (h, h), 0)
    consume(jnp.where(keep, s[:h, :h], NEG), nfull, 0, h, 0, h)
    consume(s[h:, :h], nfull, h, h, 0, h)
    consume(jnp.where(keep, s[h:, h:], NEG), nfull, h, h, h, h)
    acc = acc_s[...]
    o_ref[...] = (acc[:, :HEAD_DIM] / acc[:, HEAD_DIM:]).astype(o_ref.dtype)


def _fox_prompt(q, k, v, c, b, t):
    n, w = q.shape
    h = w // HEAD_DIM
    tq, tk = _causal_tiles(t)
    assert tq == tk and (tq // 2) % LANES == 0
    nq = t // tq
    return pl.pallas_call(
        functools.partial(_fox_kernel, tq=tq, tk=tk),
        out_shape=jax.ShapeDtypeStruct((n, w), BF16),
        grid=(b, h, nq),
        in_specs=[pl.BlockSpec((tq, HEAD_DIM), lambda bi, hi, qi: (bi * nq + qi, hi)),
                  pl.BlockSpec((t, HEAD_DIM), lambda bi, hi, qi: (bi, hi)),
                  pl.BlockSpec((t, HEAD_DIM), lambda bi, hi, qi: (bi, hi)),
                  pl.BlockSpec((None, None, 1, c.shape[-1]), lambda bi, hi, qi: (bi, hi, 0, 0))],
        out_specs=pl.BlockSpec((tq, HEAD_DIM), lambda bi, hi, qi: (bi * nq + qi, hi)),
        scratch_shapes=[pltpu.VMEM((tq, LANES), F32), pltpu.VMEM((tq, 2 * HEAD_DIM), F32),
                        pltpu.VMEM((tq, tk), F32)],
        compiler_params=_cparams(("parallel", "parallel", "arbitrary")),
        name="fox_prompt",
    )(q, k, v, c)


def _diff_finish(acc1, l1, acc2, l2, lam, gain, out_scale):
    o = acc1 / l1 - lam * (acc2 / l2)
    o = o * lax.rsqrt(jnp.mean(o * o, axis=-1, keepdims=True) + SUBLN_EPS) * gain
    return o * out_scale


def _diff_kernel(lam_ref, q_ref, k_ref, v_ref, g_ref, o_ref, m1, l1, a1, s1, m2, l2, a2, s2, *, tq, tk, out_scale):
    qi = pl.program_id(2)
    state = ((0, m1, l1, a1, s1), (HEAD_DIM, m2, l2, a2, s2))
    for _, m_s, l_s, a_s, _ in state:
        m_s[...] = jnp.full(m_s.shape, NEG, F32)
        l_s[...] = jnp.zeros(l_s.shape, F32)
        a_s[...] = jnp.zeros(a_s.shape, F32)

    def scores(kj, lo):
        off = pl.multiple_of(kj * tk, tk)
        return _qk(q_ref[:, lo:lo + HEAD_DIM], k_ref[pl.ds(off, tk), lo:lo + HEAD_DIM])

    def consume(s, v, m_s, l_s, a_s, r0=0, rn=tq):
        l_r = l_s.at[pl.ds(r0, rn)]
        a_r = a_s.at[pl.ds(r0, rn)]
        sj = _lane_chunks(s)
        m_new, alpha = _running_max(sj, m_s.at[pl.ds(r0, rn)])
        pj = [jnp.exp2(x - m_new) for x in sj]
        psum = pj[0]
        for x in pj[1:]:
            psum = psum + x
        p = jnp.concatenate([x.astype(BF16) for x in pj], axis=1)
        l_r[...] = alpha * l_r[...] + psum
        a_r[...] = jnp.concatenate([alpha, alpha], axis=1) * a_r[...] + jnp.dot(p, v, preferred_element_type=F32)

    nfull = qi
    for lo, _, _, _, s_s in state:
        s_s[...] = scores(0, lo)

    def body(kj, carry):
        off = pl.multiple_of(kj * tk, tk)
        v = v_ref[pl.ds(off, tk), :]
        for lo, m_s, l_s, a_s, s_s in state:
            s = s_s[...]
            s_next = scores(kj + 1, lo)
            consume(s, v, m_s, l_s, a_s)
            s_s[...] = s_next
        return carry

    lax.fori_loop(0, nfull, body, 0)
    h = tq // 2
    off = pl.multiple_of(nfull * tk, tk)
    v_lo = v_ref[pl.ds(off, h), :]
    v_hi = v_ref[pl.ds(off + h, h), :]
    keep = ((lax.broadcasted_iota(jnp.int32, (h, h), 1) >> CHUNK_SHIFT)
            <= (lax.broadcasted_iota(jnp.int32, (h, h), 0) >> CHUNK_SHIFT))
    for lo, m_s, l_s, a_s, s_s in state:
        s = s_s[...]
        consume(jnp.where(keep, s[:h, :h], NEG), v_lo, m_s, l_s, a_s, 0, h)
        consume(s[h:, :h], v_lo, m_s, l_s, a_s, h, h)
        consume(jnp.where(keep, s[h:, h:], NEG), v_hi, m_s, l_s, a_s, h, h)
    o = _diff_finish(a1[...], jnp.sum(l1[...], axis=-1, keepdims=True),
                     a2[...], jnp.sum(l2[...], axis=-1, keepdims=True), lam_ref[0], g_ref[...], out_scale)
    o_ref[...] = o.astype(o_ref.dtype)


def _diff_prompt(lam, q, k, v, gain, b, t, out_scale):
    n, w = q.shape
    hw = 2 * HEAD_DIM
    h = w // hw
    tq, tk = _causal_tiles(t)
    assert tq == tk and (tq // 2) % LANES == 0 and (tq // 2) % CHUNK == 0
    nq = t // tq
    per_map = [pltpu.VMEM((tq, LANES), F32), pltpu.VMEM((tq, LANES), F32), pltpu.VMEM((tq, hw), F32),
               pltpu.VMEM((tq, tk), F32)]
    return pl.pallas_call(
        functools.partial(_diff_kernel, tq=tq, tk=tk, out_scale=out_scale),
        out_shape=jax.ShapeDtypeStruct((n, w), BF16),
        grid_spec=pltpu.PrefetchScalarGridSpec(
            num_scalar_prefetch=1,
            grid=(b, h, nq),
            in_specs=[pl.BlockSpec((tq, hw), lambda bi, hi, qi, lam: (bi * nq + qi, hi)),
                      pl.BlockSpec((t, hw), lambda bi, hi, qi, lam: (bi, hi)),
                      pl.BlockSpec((t, hw), lambda bi, hi, qi, lam: (bi, hi)),
                      pl.BlockSpec((1, hw), lambda bi, hi, qi, lam: (0, 0))],
            out_specs=pl.BlockSpec((tq, hw), lambda bi, hi, qi, lam: (bi * nq + qi, hi)),
            scratch_shapes=per_map + per_map),
        compiler_params=_cparams(("parallel", "parallel", "arbitrary")),
        name="diff_prompt",
    )(lam, q, k, v, gain)


def _fox_decode_kernel(q_ref, kc_ref, vc_ref, kn_ref, vn_ref, c_ref, o_ref, *, past, tq):
    q = q_ref[...]
    s_c = _qk(q, kc_ref[...].astype(BF16)) - c_ref[:, :past] * LOG2E
    c_new = c_ref[:, past:past + LANES]
    s_n = _qk(q, kn_ref[...]) - c_new[:, :tq] * LOG2E
    row = lax.broadcasted_iota(jnp.int32, (tq, tq), 0)
    col = lax.broadcasted_iota(jnp.int32, (tq, tq), 1)
    s_n = jnp.where(col <= row, s_n, NEG)
    m = jnp.maximum(jnp.max(s_c, axis=-1, keepdims=True), jnp.max(s_n, axis=-1, keepdims=True))
    p_c = jnp.exp2(s_c - m)
    p_n = jnp.exp2(s_n - m)
    l = jnp.sum(p_c, axis=-1, keepdims=True) + jnp.sum(p_n, axis=-1, keepdims=True)
    o = (jnp.dot(p_c.astype(BF16), vc_ref[...].astype(BF16), preferred_element_type=F32)
         + jnp.dot(p_n.astype(BF16), vn_ref[...], preferred_element_type=F32))
    o_ref[...] = (o / l).astype(o_ref.dtype)


def _fox_decode(q, kc, vc, kn, vn, c, b, t, past):
    n, w = q.shape
    h = w // HEAD_DIM
    assert past % LANES == 0 and t <= LANES
    spad = c.shape[-1]
    new = pl.BlockSpec((t, HEAD_DIM), lambda bi, hi: (bi, hi))
    old = pl.BlockSpec((past, HEAD_DIM), lambda bi, hi: (bi, hi))
    return pl.pallas_call(
        functools.partial(_fox_decode_kernel, past=past, tq=t),
        out_shape=jax.ShapeDtypeStruct((n, w), BF16),
        grid=(b, h),
        in_specs=[new, old, old, new, new,
                  pl.BlockSpec((None, None, 1, spad), lambda bi, hi: (bi, hi, 0, 0))],
        out_specs=new,
        compiler_params=_cparams(("parallel", "parallel")),
        name="fox_decode",
    )(q, kc, vc, kn, vn, c)


def _diff_decode_kernel(lam_ref, q_ref, kc_ref, vc_ref, kn_ref, vn_ref, g_ref, o_ref, *, past, tq, out_scale):
    vc = vc_ref[...].astype(BF16)
    vn = vn_ref[...]
    row = past + lax.broadcasted_iota(jnp.int32, (tq, tq), 0)
    col = past + lax.broadcasted_iota(jnp.int32, (tq, tq), 1)
    keep = (col >> CHUNK_SHIFT) <= (row >> CHUNK_SHIFT)
    accs, ls = [], []
    for lo in (0, HEAD_DIM):
        q = q_ref[:, lo:lo + HEAD_DIM]
        s_c = _qk(q, kc_ref[:, lo:lo + HEAD_DIM].astype(BF16))
        s_n = jnp.where(keep, _qk(q, kn_ref[:, lo:lo + HEAD_DIM]), NEG)
        m = jnp.maximum(jnp.max(s_c, axis=-1, keepdims=True), jnp.max(s_n, axis=-1, keepdims=True))
        p_c = jnp.exp2(s_c - m)
        p_n = jnp.exp2(s_n - m)
        ls.append(jnp.sum(p_c, axis=-1, keepdims=True) + jnp.sum(p_n, axis=-1, keepdims=True))
        accs.append(jnp.dot(p_c.astype(BF16), vc, preferred_element_type=F32)
                    + jnp.dot(p_n.astype(BF16), vn, preferred_element_type=F32))
    o = _diff_finish(accs[0], ls[0], accs[1], ls[1], lam_ref[0], g_ref[...], out_scale)
    o_ref[...] = o.astype(o_ref.dtype)


def _diff_decode(lam, q, kc, vc, kn, vn, gain, b, t, past, out_scale):
    n, w = q.shape
    hw = 2 * HEAD_DIM
    h = w // hw
    assert (past - 1) // CHUNK <= past // CHUNK
    new = pl.BlockSpec((t, hw), lambda bi, hi, lam: (bi, hi))
    old = pl.BlockSpec((past, hw), lambda bi, hi, lam: (bi, hi))
    return pl.pallas_call(
        functools.partial(_diff_decode_kernel, past=past, tq=t, out_scale=out_scale),
        out_shape=jax.ShapeDtypeStruct((n, w), BF16),
        grid_spec=pltpu.PrefetchScalarGridSpec(
            num_scalar_prefetch=1,
            grid=(b, h),
            in_specs=[new, old, old, new, new,
                      pl.BlockSpec((1, hw), lambda bi, hi, lam: (0, 0))],
            out_specs=new),
        compiler_params=_cparams(("parallel", "parallel")),
        name="diff_decode",
    )(lam, q, kc, vc, kn, vn, gain)


def _merge_kernel(of_ref, od_ref, gf_ref, gd_ref, wof_ref, wod_ref, o_ref):
    yf = jnp.dot(of_ref[...], wof_ref[...], preferred_element_type=F32)
    yd = jnp.dot(od_ref[...], wod_ref[...], preferred_element_type=F32)
    o_ref[...] = (gf_ref[...].astype(F32) * yf + gd_ref[...].astype(F32) * yd).astype(o_ref.dtype)


def _merge(o_fox, o_diff, gates, w_o_fox, w_o_diff):
    n = o_fox.shape[0]
    d = w_o_fox.shape[1]
    tm = _tile(n, 512)
    row = lambda c: pl.BlockSpec((tm, c), lambda i: (i, 0))
    full = lambda a: pl.BlockSpec(a.shape, lambda i: (0, 0))
    return pl.pallas_call(
        _merge_kernel,
        out_shape=jax.ShapeDtypeStruct((n, d), BF16),
        grid=(n // tm,),
        in_specs=[row(o_fox.shape[1]), row(o_diff.shape[1]),
                  pl.BlockSpec((tm, d), lambda i: (i, 0)), pl.BlockSpec((tm, d), lambda i: (i, 1)),
                  full(w_o_fox), full(w_o_diff)],
        out_specs=row(d),
        compiler_params=_cparams(("parallel",)),
        name="merge",
    )(o_fox, o_diff, gates, gates, w_o_fox, w_o_diff)


def _mix_kernel(m_ref, x_ref, wout_ref, nrm_ref, wr_ref, br_ref, h_ref, xn_ref, ridx_ref, rw_ref, *, n_experts):
    h = x_ref[...] + jnp.dot(m_ref[...], wout_ref[...], preferred_element_type=F32)
    h_ref[...] = h
    xf = h * lax.rsqrt(jnp.mean(h * h, axis=-1, keepdims=True) + NORM_EPS) * nrm_ref[...]
    xn_ref[...] = _pack_halves(xf)
    xh = xf.astype(BF16)
    xl = (xf - xh.astype(F32)).astype(BF16)
    wr = wr_ref[...]
    both = jnp.dot(xh, wr, preferred_element_type=F32)
    logits = (both[:, :LANES] + both[:, LANES:] + jnp.dot(xl, wr[:, :LANES], preferred_element_type=F32)
              + br_ref[...])
    lane = lax.broadcasted_iota(jnp.int32, logits.shape, 1).astype(F32)
    cur = jnp.where(lane < n_experts, logits, -jnp.inf)
    vals, idxs = [], []
    for _ in range(TOP_K):
        mk = jnp.max(cur, axis=-1, keepdims=True)
        ik = jnp.min(jnp.where(cur == mk, lane, float(LANES)), axis=-1, keepdims=True)
        cur = jnp.where(lane == ik, -jnp.inf, cur)
        vals.append(mk)
        idxs.append(ik)
    es = [jnp.exp(vk - vals[0]) for vk in vals]
    den = es[0] + es[1] + es[2] + es[3]
    ridx = jnp.zeros(logits.shape, F32)
    rw = jnp.zeros(logits.shape, F32)
    for k in range(TOP_K):
        ridx = jnp.where(lane == float(k), idxs[k], ridx)
        rw = jnp.where(lane == float(k), es[k] / den, rw)
    ridx_ref[...] = ridx.astype(jnp.int32)
    rw_ref[...] = rw


def _mix(merged, x, w_out, norm_ffn, w_router, b_router):
    n, d = x.shape
    e = w_router.shape[1]
    tm = _tile(n, 512)
    w_pad = _pad_cols(w_router, LANES)
    w_hi = _top_bits(w_pad)
    wr = jnp.concatenate([w_hi, _top_bits(w_pad - w_hi)], axis=1).astype(BF16)
    br = _pad_cols(b_router.reshape(1, e), LANES)
    row = lambda c: pl.BlockSpec((tm, c), lambda i: (i, 0))
    full = lambda a: pl.BlockSpec(a.shape, lambda i: (0, 0), pipeline_mode=pl.Buffered(1))
    args = [merged, x, w_out, norm_ffn.reshape(1, d), wr, br]
    in_specs = [row(d), row(d), full(w_out), full(args[3]), full(wr), full(br)]
    return pl.pallas_call(
        functools.partial(_mix_kernel, n_experts=e),
        out_shape=[jax.ShapeDtypeStruct((n, d), F32), jax.ShapeDtypeStruct((n, d // 2), jnp.uint32),
                   jax.ShapeDtypeStruct((n, LANES), jnp.int32), jax.ShapeDtypeStruct((n, LANES), F32)],
        grid=(n // tm,),
        in_specs=in_specs,
        out_specs=[row(d), row(d // 2), row(LANES), row(LANES)],
        compiler_params=_cparams(("parallel",)),
        name="mix",
    )(*args)


def _moe_kernel(be_ref, nact_ref, nv_ref, tok_hbm, slot_hbm, x_hbm, wg_ref, bg_ref, wu_ref, bu_ref, wd_ref,
                bd_ref, out_hbm, xbuf, xb, tok_s, slot_s, acc, ostage, gsem, ssem, isem, *, tb, nft, n_real):
    blk = pl.program_id(0)
    ft = pl.program_id(1)
    nact = nact_ref[0]
    active = blk < nact
    cur = lax.rem(blk, 2)
    nxt = 1 - cur
    per_step = tb // nft

    def tok_copy(b, s):
        return pltpu.make_async_copy(tok_hbm.at[pl.ds(b, 1)], tok_s.at[pl.ds(s, 1)], isem.at[0])

    def slot_copy(b, s):
        return pltpu.make_async_copy(slot_hbm.at[pl.ds(b, 1)], slot_s.at[pl.ds(s, 1)], isem.at[1])

    def gather_row(s, r):
        t = tok_s[s, r]
        pltpu.make_async_copy(x_hbm.at[pl.ds(t, 1)], xbuf.at[s, pl.ds(r, 1)], gsem.at[s]).start()

    def scatter_row(s, r):
        d = slot_s[s, r]
        pltpu.make_async_copy(ostage.at[s, pl.ds(r, 1)], out_hbm.at[pl.ds(d, 1)], ssem.at[s]).start(priority=1)

    def wait_gather(s):
        pltpu.make_async_copy(xbuf.at[s], xbuf.at[s], gsem.at[s]).wait()

    def wait_scatter(s):
        pltpu.make_async_copy(ostage.at[s], ostage.at[s], ssem.at[s]).wait()

    def loop_rows(fn, s):
        def body(r, carry):
            fn(s, r)
            return carry
        lax.fori_loop(0, tb, body, 0)

    @pl.when((blk == 0) & (ft == 0))
    def _():
        ostage[1] = jnp.zeros(ostage.shape[1:], ostage.dtype)

        def spare(r, carry):
            slot_s[1, r] = n_real + r
            return carry
        lax.fori_loop(0, tb, spare, 0)
        for b in range(2):
            c = tok_copy(b, b)
            c.start()
            c.wait()
        loop_rows(gather_row, 0)

    @pl.when((ft == 0) & (blk >= 1) & (blk <= nact))
    def _():
        tok_copy(blk + 1, nxt).wait()
        slot_copy(blk - 1, nxt).wait()

    @pl.when((ft == 0) & (blk <= nact))
    def _():
        wait_gather(cur)

    @pl.when((ft == 0) & (blk >= 1) & (blk - 2 < nact))
    def _():
        wait_scatter(cur)

    @pl.when((ft == 0) & active)
    def _():
        tok_copy(blk + 2, cur).start()
        slot_copy(blk, cur).start()
        lo, hi = _unpack_halves(xbuf[cur])
        xb[...] = jnp.concatenate([lo.astype(BF16), hi.astype(BF16)], axis=1)
        acc[...] = jnp.broadcast_to(bd_ref[...], acc.shape)

    @pl.when((ft == 0) & (blk == nact) & (blk >= 1))
    def _():
        loop_rows(scatter_row, nxt)

    def compute(rows):
        base = ft * per_step
        for j in range(per_step):
            gather_row(nxt, base + j)
        for j in range(per_step):
            scatter_row(nxt, base + j)
        x = xb[:rows]
        g = jnp.dot(x, wg_ref[...], preferred_element_type=F32) + bg_ref[...]
        u = jnp.dot(x, wu_ref[...], preferred_element_type=F32) + bu_ref[...]
        g = jnp.minimum(g, SWIGLU_LIMIT)
        u = jnp.clip(u, -SWIGLU_LIMIT, SWIGLU_LIMIT)
        act = (u + 1.0) * (g * jax.nn.sigmoid(SWIGLU_ALPHA * g))
        acc[:rows] += jnp.dot(act.astype(BF16), wd_ref[...], preferred_element_type=F32)

    half_full = nv_ref[blk] <= tb // 2

    @pl.when(active & half_full)
    def _():
        compute(tb // 2)

    @pl.when(active & jnp.logical_not(half_full))
    def _():
        compute(tb)

    @pl.when(active & (ft == nft - 1))
    def _():
        ostage[cur] = _pack_halves(acc[...])


def _moe(block_exp, nact, nvalid, row_tok, row_slot, xn, w_gate, b_gate, w_up, b_up, w_down, b_down, n_real):
    nb, tb = row_tok.shape
    e, d, f = w_gate.shape
    tf = _tile(f, 1024)
    nft = f // tf
    d2 = d // 2
    assert nft >= 2 and tb % nft == 0 and xn.shape[1] == d2

    def ftile(b, j, be, na, nv):
        return jnp.where(b < na[0], j, nft - 1)

    any_spec = pl.BlockSpec(memory_space=pl.ANY)
    in_specs = [any_spec, any_spec, any_spec,
                pl.BlockSpec((None, d, tf), lambda b, j, be, na, nv: (be[b], 0, ftile(b, j, be, na, nv))),
                pl.BlockSpec((None, 1, tf), lambda b, j, be, na, nv: (be[b], 0, ftile(b, j, be, na, nv))),
                pl.BlockSpec((None, d, tf), lambda b, j, be, na, nv: (be[b], 0, ftile(b, j, be, na, nv))),
                pl.BlockSpec((None, 1, tf), lambda b, j, be, na, nv: (be[b], 0, ftile(b, j, be, na, nv))),
                pl.BlockSpec((None, tf, d), lambda b, j, be, na, nv: (be[b], ftile(b, j, be, na, nv), 0)),
                pl.BlockSpec((None, 1, d), lambda b, j, be, na, nv: (be[b], 0, 0))]
    return pl.pallas_call(
        functools.partial(_moe_kernel, tb=tb, nft=nft, n_real=n_real),
        out_shape=jax.ShapeDtypeStruct((n_real + tb, d2), jnp.uint32),
        grid_spec=pltpu.PrefetchScalarGridSpec(
            num_scalar_prefetch=3,
            grid=(nb, nft),
            in_specs=in_specs,
            out_specs=any_spec,
            scratch_shapes=[pltpu.VMEM((2, tb, d2), jnp.uint32), pltpu.VMEM((tb, d), BF16),
                            pltpu.SMEM((2, tb), jnp.int32), pltpu.SMEM((2, tb), jnp.int32),
                            pltpu.VMEM((tb, d), F32), pltpu.VMEM((2, tb, d2), jnp.uint32),
                            pltpu.SemaphoreType.DMA((2,)), pltpu.SemaphoreType.DMA((2,)),
                            pltpu.SemaphoreType.DMA((2,))]),
        compiler_params=_cparams(("arbitrary", "arbitrary")),
        name="moe",
    )(block_exp, nact, nvalid, row_tok, row_slot, xn, w_gate, b_gate.reshape(e, 1, f), w_up,
      b_up.reshape(e, 1, f), w_down, b_down.reshape(e, 1, d))


def _routing(top_idx, n_experts, tb):
    n = top_idx.shape[0]
    a = n * TOP_K
    expert = top_idx.reshape(-1)
    order = jnp.argsort(expert).astype(jnp.int32)
    counts = jnp.sum(expert[:, None] == jnp.arange(n_experts, dtype=jnp.int32)[None, :], axis=0, dtype=jnp.int32)
    padded = (counts + tb - 1) // tb * tb
    start = jnp.cumsum(counts) - counts
    pend = jnp.cumsum(padded)
    pstart = pend - padded
    nb = -(-a // tb) + n_experts + 1
    row0 = jnp.arange(nb, dtype=jnp.int32) * tb
    block_exp = jnp.minimum(jnp.sum(pend[None, :] <= row0[:, None], axis=1), n_experts - 1).astype(jnp.int32)
    nvalid = jnp.clip(pstart[block_exp] + counts[block_exp] - row0, 0, tb)
    src0 = jnp.clip(start[block_exp] + row0 - pstart[block_exp], 0, a)
    order_pad = jnp.concatenate([order, jnp.zeros((tb,), jnp.int32)])
    r = jnp.arange(tb, dtype=jnp.int32)[None, :]
    win = order_pad[src0[:, None] + r]
    valid = r < nvalid[:, None]
    tok = win // TOP_K
    row_tok = jnp.where(valid, tok, 0)
    row_slot = jnp.where(valid, (win % TOP_K) * n + tok, a + r)
    nact = (pend[-1] // tb).reshape(1).astype(jnp.int32)
    return block_exp, nact, nvalid.astype(jnp.int32), row_tok.astype(jnp.int32), row_slot.astype(jnp.int32)


def _ple_kernel(h_ref, s0_ref, s1_ref, s2_ref, s3_ref, rw_ref, pe_ref, nple_ref, wg_ref, wp_ref, nfin_ref,
                o_ref, *, final):
    h = h_ref[...]
    rw = rw_ref[...]
    for k, s_ref in enumerate((s0_ref, s1_ref, s2_ref, s3_ref)):
        h = h + rw[:, k:k + 1] * jnp.concatenate(_unpack_halves(s_ref[...]), axis=1)
    xn = h * lax.rsqrt(jnp.mean(h * h, axis=-1, keepdims=True) + NORM_EPS) * nple_ref[...]
    gate = jax.nn.sigmoid(jnp.dot(xn.astype(BF16), wg_ref[...], preferred_element_type=F32))
    h = h + gate * jnp.dot(pe_ref[...].astype(BF16), wp_ref[...], preferred_element_type=F32)
    if final:
        h = h * lax.rsqrt(jnp.mean(h * h, axis=-1, keepdims=True) + NORM_EPS) * nfin_ref[...]
    o_ref[...] = h


def _ple(h, slots, n_all, tok0, rw, pe, norm_ple, w_ple_gate, w_ple_proj, norm_final, final):
    n, d = h.shape
    tm = _tile(n, 256)
    assert TOP_K == 4 and n_all % tm == 0 and tok0 % tm == 0
    row = lambda c: pl.BlockSpec((tm, c), lambda i: (i, 0))
    full = lambda a: pl.BlockSpec(a.shape, lambda i: (0, 0))
    slot = lambda k: pl.BlockSpec((tm, d // 2), lambda i, t0=(k * n_all + tok0) // tm: (t0 + i, 0))
    g1 = norm_ple.reshape(1, d)
    g2 = norm_final.reshape(1, d)
    return pl.pallas_call(
        functools.partial(_ple_kernel, final=final),
        out_shape=jax.ShapeDtypeStruct((n, d), F32),
        grid=(n // tm,),
        in_specs=[row(d), slot(0), slot(1), slot(2), slot(3), row(LANES), row(pe.shape[1]),
                  full(g1), full(w_ple_gate), full(w_ple_proj), full(g2)],
        out_specs=row(d),
        compiler_params=_cparams(("parallel",)),
        name="ple",
    )(h, slots, slots, slots, slots, rw, pe, g1, w_ple_gate, w_ple_proj, g2)


def _rope_tables(pos):
    half = ROPE_DIM // 2
    inv_freq = ROPE_THETA ** (-2.0 * jnp.arange(half, dtype=F32) / ROPE_DIM)
    ang = pos.astype(F32)[:, None] * inv_freq
    cos, sin = jnp.cos(ang), jnp.sin(ang)
    t = pos.shape[0]
    one = jnp.ones((t, LANES - ROPE_DIM), F32)
    zero = jnp.zeros((t, LANES - ROPE_DIM), F32)
    zh = jnp.zeros((t, half), F32)
    c = jnp.concatenate([cos, cos, one], axis=1)
    s1 = jnp.concatenate([zh, sin, zero], axis=1)
    s2 = jnp.concatenate([-sin, zh, zero], axis=1)
    return c, s1, s2


def _pad_cols(w, mult):
    c = w.shape[1]
    cp = -(-c // mult) * mult
    return jnp.pad(w, ((0, 0), (0, cp - c)))


def _layer_weights(w_in, b_forget, d):
    h_f = d // (2 * HEAD_DIM)
    fw = h_f * HEAD_DIM
    dw = fw
    o = 0
    seg = {}
    for name, width in (("fq", fw), ("fk", fw), ("fv", fw), ("fl", h_f), ("dq", dw), ("dk", dw), ("dv", dw),
                        ("gates", 2 * d)):
        seg[name] = w_in[:, o:o + width]
        o += width
    assert o == w_in.shape[1]
    out = {k: v.astype(BF16) for k, v in seg.items() if k != "fl"}
    out["fl"] = _pad_cols(seg["fl"], LANES).astype(BF16)
    out["fl_bias"] = _pad_cols(b_forget.reshape(1, h_f), LANES)
    return out, h_f


def _project(x, norm_mix, pw, tables, h_f):
    qscale = LOG2E * HEAD_DIM ** -0.5
    xn, fq = _proj(x, pw["fq"], "norm_scale_bf", scale=qscale, bias=norm_mix.reshape(1, -1))
    fk, fk_b = _proj(xn, pw["fk"], "f32_bf", head_dim=HEAD_DIM)
    fv, fv_b = _proj(xn, pw["fv"], "f32_bf", head_dim=HEAD_DIM)
    logf, = _proj(xn, pw["fl"], "logsig", bias=pw["fl_bias"], out_cols=h_f)
    dq, = _proj(xn, pw["dq"], "rope_bf", scale=qscale, tables=tables)
    dk, dk_b = _proj(xn, pw["dk"], "rope_f32_bf", tables=tables, head_dim=2 * HEAD_DIM)
    dv, dv_b = _proj(xn, pw["dv"], "f32_bf", head_dim=2 * HEAD_DIM)
    gates, = _proj(xn, pw["gates"], "sigmoid_bf")
    return dict(fq=fq, fk=fk, fk_b=fk_b, fv=fv, fv_b=fv_b, logf=logf, dq=dq, dk=dk, dk_b=dk_b,
                dv=dv, dv_b=dv_b, gates=gates)


def _cum_logf(logf_bth):
    b, s, h = logf_bth.shape
    spad = -(-s // 1024) * 1024
    x = jnp.pad(jnp.swapaxes(logf_bth, 1, 2), ((0, 0), (0, 0), (0, spad - s)))
    return _cumsum(x.reshape(b * h, spad)).reshape(b, h, 1, spad)


def kernel(x_prompt, x_sample, cache_fox_k, cache_fox_v, cache_fox_logf, cache_diff_k, cache_diff_v,
           p_prompt, p_sample, norm_mix, w_in, b_forget, lambda_q1, lambda_k1, lambda_q2, lambda_k2,
           diff_subln, w_o_fox, w_o_diff, w_out, norm_ffn, w_router, b_router, w_gate, b_gate,
           w_up, b_up, w_down, b_down, norm_ple, w_ple_gate, w_ple_proj, norm_final):
    depth = w_in.shape[0]
    bp, tp, d = x_prompt.shape
    bs, ts, _ = x_sample.shape
    past = cache_fox_k.shape[2]
    n_p, n_s = bp * tp, bs * ts
    n_experts = w_router.shape[-1]
    tb = 512

    h_p = x_prompt.reshape(n_p, d)
    h_s = x_sample.reshape(n_s, d)
    tab_p = _rope_tables(jnp.arange(tp, dtype=jnp.int32))
    tab_s = tuple(jnp.tile(t, (bs, 1)) for t in _rope_tables(past + jnp.arange(ts, dtype=jnp.int32)))
    st_p, st_s = [], []
    for i in range(depth):
        lam_init = 0.8 - 0.6 * math.exp(-0.3 * i)
        lam = (jnp.exp(jnp.sum(lambda_q1[i].astype(F32) * lambda_k1[i].astype(F32)))
               - jnp.exp(jnp.sum(lambda_q2[i].astype(F32) * lambda_k2[i].astype(F32)))
               + lam_init).reshape(1).astype(F32)
        out_scale = 1.0 - lam_init
        pw, h_f = _layer_weights(w_in[i], b_forget[i], d)
        h_d = h_f // 2
        subln = diff_subln[i].reshape(1, 2 * HEAD_DIM)
        wof, wod, wo = w_o_fox[i].astype(BF16), w_o_diff[i].astype(BF16), w_out[i].astype(BF16)
        wg, wu, wd = w_gate[i].astype(BF16), w_up[i].astype(BF16), w_down[i].astype(BF16)
        wpg, wpp = w_ple_gate[i].astype(BF16), w_ple_proj[i].astype(BF16)

        pr = _project(h_p, norm_mix[i], pw, tab_p, h_f)
        c_p = _cum_logf(pr["logf"].reshape(bp, tp, h_f))
        o_fox_p = _fox_prompt(pr["fq"], pr["fk_b"], pr["fv_b"], c_p, bp, tp)
        o_diff_p = _diff_prompt(lam, pr["dq"], pr["dk_b"], pr["dv_b"], subln, bp, tp, out_scale)
        h1_p, xn_p, ridx_p, rw_p = _mix(_merge(o_fox_p, o_diff_p, pr["gates"], wof, wod), h_p, wo,
                                        norm_ffn[i], w_router[i], b_router[i])

        sr = _project(h_s, norm_mix[i], pw, tab_s, h_f)
        logf_all = jnp.concatenate([cache_fox_logf[i].astype(F32), sr["logf"].reshape(bs, ts, h_f)], axis=1)
        c_s = _cum_logf(logf_all)
        o_fox_s = _fox_decode(sr["fq"], cache_fox_k[i].reshape(bs * past, -1), cache_fox_v[i].reshape(bs * past, -1),
                              sr["fk_b"], sr["fv_b"], c_s, bs, ts, past)
        o_diff_s = _diff_decode(lam, sr["dq"], cache_diff_k[i].reshape(bs * past, -1),
                                cache_diff_v[i].reshape(bs * past, -1), sr["dk_b"], sr["dv_b"], subln,
                                bs, ts, past, out_scale)
        h1_s, xn_s, ridx_s, rw_s = _mix(_merge(o_fox_s, o_diff_s, sr["gates"], wof, wod), h_s, wo,
                                        norm_ffn[i], w_router[i], b_router[i])

        xn_all = jnp.concatenate([xn_p, xn_s], axis=0)
        top_idx = jnp.concatenate([ridx_p[:, :TOP_K], ridx_s[:, :TOP_K]], axis=0)
        block_exp, nact, nvalid, row_tok, row_slot = _routing(top_idx, n_experts, tb)
        n_all = n_p + n_s
        slots = _moe(block_exp, nact, nvalid, row_tok, row_slot, xn_all, wg, b_gate[i], wu, b_up[i], wd,
                     b_down[i], n_all * TOP_K)

        last = i == depth - 1
        h_p = _ple(h1_p, slots, n_all, 0, rw_p, p_prompt[i].reshape(n_p, -1), norm_ple[i], wpg, wpp,
                   norm_final, last)
        h_s = _ple(h1_s, slots, n_all, n_p, rw_s, p_sample[i].reshape(n_s, -1), norm_ple[i], wpg, wpp,
                   norm_final, last)

        st_p.append((pr["fk"].reshape(bp, tp, h_f, HEAD_DIM), pr["fv"].reshape(bp, tp, h_f, HEAD_DIM),
                     pr["logf"].reshape(bp, tp, h_f), pr["dk"].reshape(bp, tp, h_d, 2 * HEAD_DIM),
                     pr["dv"].reshape(bp, tp, h_d, 2 * HEAD_DIM)))
        st_s.append((sr["fk"].reshape(bs, ts, h_f, HEAD_DIM), sr["fv"].reshape(bs, ts, h_f, HEAD_DIM),
                     sr["logf"].reshape(bs, ts, h_f), sr["dk"].reshape(bs, ts, h_d, 2 * HEAD_DIM),
                     sr["dv"].reshape(bs, ts, h_d, 2 * HEAD_DIM)))

    y_prompt = h_p.reshape(bp, tp, d)
    y_sample = h_s.reshape(bs, ts, d)
    outs_p = [jnp.stack([s[j] for s in st_p]) for j in range(5)]
    outs_s = [jnp.stack([s[j] for s in st_s]) for j in range(5)]
    return (y_prompt, y_sample, *outs_p, *outs_s)
```

```python
import functools
import math

import jax
import jax.numpy as jnp
from jax import lax
from jax.experimental import pallas as pl
from jax.experimental.pallas import tpu as pltpu

F32 = jnp.float32
BF16 = jnp.bfloat16

HEAD_DIM = 128
CHUNK = 64
CHUNK_SHIFT = 6
assert 1 << CHUNK_SHIFT == CHUNK
ROPE_DIM = HEAD_DIM // 4
ROPE_THETA = 500000.0
TOP_K = 4
SWIGLU_LIMIT = 7.0
SWIGLU_ALPHA = 1.702
NORM_EPS = 1e-6
SUBLN_EPS = 1e-5
LOG2E = 1.4426950408889634
NEG = -1e30
LANES = 128
VMEM_LIMIT = 56 * 1024 * 1024


def _cparams(sem):
    return pltpu.CompilerParams(dimension_semantics=sem, vmem_limit_bytes=VMEM_LIMIT)


def _top_bits(v):
    bits = lax.bitcast_convert_type(v, jnp.uint32) & jnp.uint32(0xFFFF0000)
    return lax.bitcast_convert_type(bits, F32)


def _pack_halves(x):
    half = x.shape[1] // 2

    def rounded(v):
        bits = lax.bitcast_convert_type(v, jnp.uint32)
        return bits + jnp.uint32(0x7FFF) + ((bits >> 16) & jnp.uint32(1))

    return (rounded(x[:, half:]) & jnp.uint32(0xFFFF0000)) | (rounded(x[:, :half]) >> 16)


def _unpack_halves(w):
    lo = lax.bitcast_convert_type(w << 16, F32)
    hi = lax.bitcast_convert_type(w & jnp.uint32(0xFFFF0000), F32)
    return lo, hi


def _tile(n, pref):
    t = min(n, pref)
    assert n % t == 0, (n, pref)
    return t


def _rope_slab(x, c, s1, s2):
    return x * c + pltpu.roll(x, ROPE_DIM // 2, 1) * s1 + pltpu.roll(x, LANES - ROPE_DIM // 2, 1) * s2


def _proj_kernel(*refs, kind, scale):
    xn_ref, w_ref = refs[0], refs[1]
    if kind == "norm_scale_bf":
        x = xn_ref[...]
        xn = (x * lax.rsqrt(jnp.mean(x * x, axis=-1, keepdims=True) + NORM_EPS) * refs[2][...]).astype(BF16)
        refs[3][...] = xn
        refs[4][...] = (jnp.dot(xn, w_ref[...], preferred_element_type=F32) * scale).astype(BF16)
        return
    acc = jnp.dot(xn_ref[...], w_ref[...], preferred_element_type=F32)
    if kind == "scale_bf":
        refs[2][...] = (acc * scale).astype(BF16)
    elif kind == "f32_bf":
        refs[2][...] = acc.reshape(refs[2].shape)
        refs[3][...] = acc.astype(BF16)
    elif kind == "sigmoid_bf":
        refs[2][...] = jax.nn.sigmoid(acc).astype(BF16)
    elif kind == "logsig":
        z = acc + refs[2][...]
        val = jnp.minimum(z, 0.0) - jnp.log1p(jnp.exp(-jnp.abs(z)))
        refs[3][...] = val[:, :refs[3].shape[1]]
    elif kind in ("rope_bf", "rope_f32_bf"):
        c, s1, s2 = refs[2][...], refs[3][...], refs[4][...]
        r = jnp.concatenate([_rope_slab(acc[:, j * LANES:(j + 1) * LANES], c, s1, s2)
                             for j in range(acc.shape[1] // LANES)], axis=1)
        if kind == "rope_bf":
            refs[5][...] = (r * scale).astype(BF16)
        else:
            refs[5][...] = r.reshape(refs[5].shape)
            refs[6][...] = r.astype(BF16)
    else:
        raise ValueError(kind)


def _proj(xn, w, kind, *, scale=1.0, tables=None, bias=None, out_cols=None, head_dim=None):
    n, d = xn.shape
    c = w.shape[1]
    tm = _tile(n, 1024)
    tn = _tile(c, 1024)
    grid = (c // tn, n // tm)
    in_specs = [pl.BlockSpec((tm, d), lambda j, i: (i, 0)),
                pl.BlockSpec((d, tn), lambda j, i: (0, j))]
    args = [xn, w]
    blk = pl.BlockSpec((tm, tn), lambda j, i: (i, j))
    if head_dim is not None:
        assert tn == c and c % head_dim == 0 and head_dim % LANES == 0
        heads_shape = jax.ShapeDtypeStruct((n, c // head_dim, head_dim), F32)
        heads_blk = pl.BlockSpec((tm, c // head_dim, head_dim), lambda j, i: (i, 0, 0))
    if kind == "scale_bf" or kind == "sigmoid_bf":
        out_shape = [jax.ShapeDtypeStruct((n, c), BF16)]
        out_specs = [blk]
    elif kind == "norm_scale_bf":
        assert tn == c
        in_specs.append(pl.BlockSpec((1, d), lambda j, i: (0, 0)))
        args.append(bias)
        out_shape = [jax.ShapeDtypeStruct((n, d), BF16), jax.ShapeDtypeStruct((n, c), BF16)]
        out_specs = [pl.BlockSpec((tm, d), lambda j, i: (i, 0)), blk]
    elif kind == "f32_bf":
        out_shape = [heads_shape, jax.ShapeDtypeStruct((n, c), BF16)]
        out_specs = [heads_blk, blk]
    elif kind == "logsig":
        in_specs.append(pl.BlockSpec((1, tn), lambda j, i: (0, j)))
        args.append(bias)
        out_shape = [jax.ShapeDtypeStruct((n, out_cols), F32)]
        out_specs = [pl.BlockSpec((tm, out_cols), lambda j, i: (i, 0))]
    else:
        nt = tables[0].shape[0] // tm
        for t in tables:
            in_specs.append(pl.BlockSpec((tm, LANES), lambda j, i, nt=nt: (i % nt, 0)))
            args.append(t)
        if kind == "rope_bf":
            out_shape = [jax.ShapeDtypeStruct((n, c), BF16)]
            out_specs = [blk]
        else:
            out_shape = [heads_shape, jax.ShapeDtypeStruct((n, c), BF16)]
            out_specs = [heads_blk, blk]
    return pl.pallas_call(
        functools.partial(_proj_kernel, kind=kind, scale=scale),
        out_shape=out_shape,
        grid=grid,
        in_specs=in_specs,
        out_specs=out_specs,
        compiler_params=_cparams(("parallel", "parallel")),
        name="proj_" + kind,
    )(*args)


def _cumsum_kernel(x_ref, o_ref):
    x = x_ref[...]
    r = x.shape[0]
    li = lax.broadcasted_iota(jnp.int32, (LANES, LANES), 0)
    lj = lax.broadcasted_iota(jnp.int32, (LANES, LANES), 1)
    upper = (li <= lj).astype(F32)
    within = jnp.dot(x, upper, preferred_element_type=F32, precision=lax.Precision.HIGHEST)
    tot = jnp.broadcast_to(within[:, LANES - 1:LANES], (r, LANES))
    ri = lax.broadcasted_iota(jnp.int32, (r, r), 0)
    rj = lax.broadcasted_iota(jnp.int32, (r, r), 1)
    strict = (rj < ri).astype(F32)
    off = jnp.dot(strict, tot, preferred_element_type=F32, precision=lax.Precision.HIGHEST)
    o_ref[...] = within + off


def _cumsum(x):
    g, s = x.shape
    r = s // LANES
    out = pl.pallas_call(
        _cumsum_kernel,
        out_shape=jax.ShapeDtypeStruct((g, r, LANES), F32),
        grid=(g,),
        in_specs=[pl.BlockSpec((None, r, LANES), lambda i: (i, 0, 0))],
        out_specs=pl.BlockSpec((None, r, LANES), lambda i: (i, 0, 0)),
        compiler_params=_cparams(("parallel",)),
        name="cumsum",
    )(x.reshape(g, r, LANES))
    return out.reshape(g, s)


def _qk(q, k):
    return lax.dot_general(q, k, (((1,), (1,)), ((), ())), preferred_element_type=F32)


def _lane_chunks(s):
    return [s[:, j * LANES:(j + 1) * LANES] for j in range(s.shape[1] // LANES)]


def _running_max(sj, m_s):
    smax = sj[0]
    for x in sj[1:]:
        smax = jnp.maximum(smax, x)
    m_prev = m_s[...]
    m_new = jnp.maximum(m_prev, jnp.max(smax, axis=-1, keepdims=True))
    m_s[...] = m_new
    return m_new, jnp.exp2(m_prev - m_new)


def _causal_tiles(t):
    tq = _tile(t, 1024)
    return tq, tq


def _fox_kernel(q_ref, k_ref, v_ref, c_ref, o_ref, m_s, acc_s, s_s, *, tq, tk):
    qi = pl.program_id(2)
    q = q_ref[...]
    ones = jnp.ones((tk, LANES), BF16)
    m_s[...] = jnp.full(m_s.shape, NEG, F32)
    acc_s[...] = jnp.zeros(acc_s.shape, F32)

    def scores(kj):
        off = pl.multiple_of(kj * tk, tk)
        return _qk(q, k_ref[pl.ds(off, tk), :]) - c_ref[:, pl.ds(off, tk)] * LOG2E

    def consume(s, kj, r0=0, rn=tq, c0=0, cn=tk):
        off = pl.multiple_of(kj * tk, tk)
        v_aug = jnp.concatenate([v_ref[pl.ds(off + c0, cn), :], ones[:cn]], axis=1)
        acc_r = acc_s.at[pl.ds(r0, rn)]
        sj = _lane_chunks(s)
        m_new, alpha = _running_max(sj, m_s.at[pl.ds(r0, rn)])
        p = jnp.concatenate([jnp.exp2(x - m_new).astype(BF16) for x in sj], axis=1)
        pv = jnp.dot(p, v_aug, preferred_element_type=F32)
        acc_r[...] = jnp.concatenate([alpha, alpha], axis=1) * acc_r[...] + pv

    nfull = qi
    s_s[...] = scores(0)

    def body(kj, carry):
        s = s_s[...]
        s_next = scores(kj + 1)
        consume(s, kj)
        s_s[...] = s_next
        return carry

    lax.fori_loop(0, nfull, body, 0)
    h = tq // 2
    s = s_s[...]
    keep = lax.broadcasted_iota(jnp.int32, (h, h), 1) <= lax.broadcasted_iota(jnp.int32, (h, h), 0)
    consume(jnp.where(keep, s[:h, :h], NEG), nfull, 0, h, 0, h)
    consume(s[h:, :h], nfull, h, h, 0, h)
    consume(jnp.where(keep, s[h:, h:], NEG), nfull, h, h, h, h)
    acc = acc_s[...]
    o_ref[...] = (acc[:, :HEAD_DIM] / acc[:, HEAD_DIM:]).astype(o_ref.dtype)


def _fox_prompt(q, k, v, c, b, t):
    n, w = q.shape
    h = w // HEAD_DIM
    tq, tk = _causal_tiles(t)
    assert tq == tk and (tq // 2) % LANES == 0
    nq = t // tq
    return pl.pallas_call(
        functools.partial(_fox_kernel, tq=tq, tk=tk),
        out_shape=jax.ShapeDtypeStruct((n, w), BF16),
        grid=(b, h, nq),
        in_specs=[pl.BlockSpec((tq, HEAD_DIM), lambda bi, hi, qi: (bi * nq + qi, hi)),
                  pl.BlockSpec((t, HEAD_DIM), lambda bi, hi, qi: (bi, hi)),
                  pl.BlockSpec((t, HEAD_DIM), lambda bi, hi, qi: (bi, hi)),
                  pl.BlockSpec((None, None, 1, c.shape[-1]), lambda bi, hi, qi: (bi, hi, 0, 0))],
        out_specs=pl.BlockSpec((tq, HEAD_DIM), lambda bi, hi, qi: (bi * nq + qi, hi)),
        scratch_shapes=[pltpu.VMEM((tq, LANES), F32), pltpu.VMEM((tq, 2 * HEAD_DIM), F32),
                        pltpu.VMEM((tq, tk), F32)],
        compiler_params=_cparams(("parallel", "parallel", "arbitrary")),
        name="fox_prompt",
    )(q, k, v, c)


def _diff_finish(acc1, l1, acc2, l2, lam, gain, out_scale):
    o = acc1 / l1 - lam * (acc2 / l2)
    o = o * lax.rsqrt(jnp.mean(o * o, axis=-1, keepdims=True) + SUBLN_EPS) * gain
    return o * out_scale


def _diff_kernel(lam_ref, q_ref, k_ref, v_ref, g_ref, o_ref, m1, l1, a1, s1, m2, l2, a2, s2, *, tq, tk, out_scale):
    qi = pl.program_id(2)
    state = ((0, m1, l1, a1, s1), (HEAD_DIM, m2, l2, a2, s2))
    for _, m_s, l_s, a_s, _ in state:
        m_s[...] = jnp.full(m_s.shape, NEG, F32)
        l_s[...] = jnp.zeros(l_s.shape, F32)
        a_s[...] = jnp.zeros(a_s.shape, F32)

    def scores(kj, lo):
        off = pl.multiple_of(kj * tk, tk)
        return _qk(q_ref[:, lo:lo + HEAD_DIM], k_ref[pl.ds(off, tk), lo:lo + HEAD_DIM])

    def consume(s, v, m_s, l_s, a_s, r0=0, rn=tq):
        l_r = l_s.at[pl.ds(r0, rn)]
        a_r = a_s.at[pl.ds(r0, rn)]
        sj = _lane_chunks(s)
        m_new, alpha = _running_max(sj, m_s.at[pl.ds(r0, rn)])
        pj = [jnp.exp2(x - m_new) for x in sj]
        psum = pj[0]
        for x in pj[1:]:
            psum = psum + x
        p = jnp.concatenate([x.astype(BF16) for x in pj], axis=1)
        l_r[...] = alpha * l_r[...] + psum
        a_r[...] = jnp.concatenate([alpha, alpha], axis=1) * a_r[...] + jnp.dot(p, v, preferred_element_type=F32)

    nfull = qi
    for lo, _, _, _, s_s in state:
        s_s[...] = scores(0, lo)

    def body(kj, carry):
        off = pl.multiple_of(kj * tk, tk)
        v = v_ref[pl.ds(off, tk), :]
        for lo, m_s, l_s, a_s, s_s in state:
            s = s_s[...]
            s_next = scores(kj + 1, lo)
            consume(s, v, m_s, l_s, a_s)
            s_s[...] = s_next
        return carry

    lax.fori_loop(0, nfull, body, 0)
    h = tq // 2
    off = pl.multiple_of(nfull * tk, tk)
    v_lo = v_ref[pl.ds(off, h), :]
    v_hi = v_ref[pl.ds(off + h, h), :]
    keep = ((lax.broadcasted_iota(jnp.int32, (h, h), 1) >> CHUNK_SHIFT)
            <= (lax.broadcasted_iota(jnp.int32, (h, h), 0) >> CHUNK_SHIFT))
    for lo, m_s, l_s, a_s, s_s in state:
        s = s_s[...]
        consume(jnp.where(keep, s[:h, :h], NEG), v_lo, m_s, l_s, a_s, 0, h)
        consume(s[h:, :h], v_lo, m_s, l_s, a_s, h, h)
        consume(jnp.where(keep, s[h:, h:], NEG), v_hi, m_s, l_s, a_s, h, h)
    o = _diff_finish(a1[...], jnp.sum(l1[...], axis=-1, keepdims=True),
                     a2[...], jnp.sum(l2[...], axis=-1, keepdims=True), lam_ref[0], g_ref[...], out_scale)
    o_ref[...] = o.astype(o_ref.dtype)


def _diff_prompt(lam, q, k, v, gain, b, t, out_scale):
    n, w = q.shape
    hw = 2 * HEAD_DIM
    h = w // hw
    tq, tk = _causal_tiles(t)
    assert tq == tk and (tq // 2) % LANES == 0 and (tq // 2) % CHUNK == 0
    nq = t // tq
    per_map = [pltpu.VMEM((tq, LANES), F32), pltpu.VMEM((tq, LANES), F32), pltpu.VMEM((tq, hw), F32),
               pltpu.VMEM((tq, tk), F32)]
    return pl.pallas_call(
        functools.partial(_diff_kernel, tq=tq, tk=tk, out_scale=out_scale),
        out_shape=jax.ShapeDtypeStruct((n, w), BF16),
        grid_spec=pltpu.PrefetchScalarGridSpec(
            num_scalar_prefetch=1,
            grid=(b, h, nq),
            in_specs=[pl.BlockSpec((tq, hw), lambda bi, hi, qi, lam: (bi * nq + qi, hi)),
                      pl.BlockSpec((t, hw), lambda bi, hi, qi, lam: (bi, hi)),
                      pl.BlockSpec((t, hw), lambda bi, hi, qi, lam: (bi, hi)),
                      pl.BlockSpec((1, hw), lambda bi, hi, qi, lam: (0, 0))],
            out_specs=pl.BlockSpec((tq, hw), lambda bi, hi, qi, lam: (bi * nq + qi, hi)),
            scratch_shapes=per_map + per_map),
        compiler_params=_cparams(("parallel", "parallel", "arbitrary")),
        name="diff_prompt",
    )(lam, q, k, v, gain)


def _fox_decode_kernel(q_ref, kc_ref, vc_ref, kn_ref, vn_ref, c_ref, o_ref, *, past, tq):
    q = q_ref[...]
    s_c = _qk(q, kc_ref[...].astype(BF16)) - c_ref[:, :past] * LOG2E
    c_new = c_ref[:, past:past + LANES]
    s_n = _qk(q, kn_ref[...]) - c_new[:, :tq] * LOG2E
    row = lax.broadcasted_iota(jnp.int32, (tq, tq), 0)
    col = lax.broadcasted_iota(jnp.int32, (tq, tq), 1)
    s_n = jnp.where(col <= row, s_n, NEG)
    m = jnp.maximum(jnp.max(s_c, axis=-1, keepdims=True), jnp.max(s_n, axis=-1, keepdims=True))
    p_c = jnp.exp2(s_c - m)
    p_n = jnp.exp2(s_n - m)
    l = jnp.sum(p_c, axis=-1, keepdims=True) + jnp.sum(p_n, axis=-1, keepdims=True)
    o = (jnp.dot(p_c.astype(BF16), vc_ref[...].astype(BF16), preferred_element_type=F32)
         + jnp.dot(p_n.astype(BF16), vn_ref[...], preferred_element_type=F32))
    o_ref[...] = (o / l).astype(o_ref.dtype)


def _fox_decode(q, kc, vc, kn, vn, c, b, t, past):
    n, w = q.shape
    h = w // HEAD_DIM
    assert past % LANES == 0 and t <= LANES
    spad = c.shape[-1]
    new = pl.BlockSpec((t, HEAD_DIM), lambda bi, hi: (bi, hi))
    old = pl.BlockSpec((past, HEAD_DIM), lambda bi, hi: (bi, hi))
    return pl.pallas_call(
        functools.partial(_fox_decode_kernel, past=past, tq=t),
        out_shape=jax.ShapeDtypeStruct((n, w), BF16),
        grid=(b, h),
        in_specs=[new, old, old, new, new,
                  pl.BlockSpec((None, None, 1, spad), lambda bi, hi: (bi, hi, 0, 0))],
        out_specs=new,
        compiler_params=_cparams(("parallel", "parallel")),
        name="fox_decode",
    )(q, kc, vc, kn, vn, c)


def _diff_decode_kernel(lam_ref, q_ref, kc_ref, vc_ref, kn_ref, vn_ref, g_ref, o_ref, *, past, tq, out_scale):
    vc = vc_ref[...].astype(BF16)
    vn = vn_ref[...]
    row = past + lax.broadcasted_iota(jnp.int32, (tq, tq), 0)
    col = past + lax.broadcasted_iota(jnp.int32, (tq, tq), 1)
    keep = (col >> CHUNK_SHIFT) <= (row >> CHUNK_SHIFT)
    accs, ls = [], []
    for lo in (0, HEAD_DIM):
        q = q_ref[:, lo:lo + HEAD_DIM]
        s_c = _qk(q, kc_ref[:, lo:lo + HEAD_DIM].astype(BF16))
        s_n = jnp.where(keep, _qk(q, kn_ref[:, lo:lo + HEAD_DIM]), NEG)
        m = jnp.maximum(jnp.max(s_c, axis=-1, keepdims=True), jnp.max(s_n, axis=-1, keepdims=True))
        p_c = jnp.exp2(s_c - m)
        p_n = jnp.exp2(s_n - m)
        ls.append(jnp.sum(p_c, axis=-1, keepdims=True) + jnp.sum(p_n, axis=-1, keepdims=True))
        accs.append(jnp.dot(p_c.astype(BF16), vc, preferred_element_type=F32)
                    + jnp.dot(p_n.astype(BF16), vn, preferred_element_type=F32))
    o = _diff_finish(accs[0], ls[0], accs[1], ls[1], lam_ref[0], g_ref[...], out_scale)
    o_ref[...] = o.astype(o_ref.dtype)


def _diff_decode(lam, q, kc, vc, kn, vn, gain, b, t, past, out_scale):
    n, w = q.shape
    hw = 2 * HEAD_DIM
    h = w // hw
    assert (past - 1) // CHUNK <= past // CHUNK
    new = pl.BlockSpec((t, hw), lambda bi, hi, lam: (bi, hi))
    old = pl.BlockSpec((past, hw), lambda bi, hi, lam: (bi, hi))
    return pl.pallas_call(
        functools.partial(_diff_decode_kernel, past=past, tq=t, out_scale=out_scale),
        out_shape=jax.ShapeDtypeStruct((n, w), BF16),
        grid_spec=pltpu.PrefetchScalarGridSpec(
            num_scalar_prefetch=1,
            grid=(b, h),
            in_specs=[new, old, old, new, new,
                      pl.BlockSpec((1, hw), lambda bi, hi, lam: (0, 0))],
            out_specs=new),
        compiler_params=_cparams(("parallel", "parallel")),
        name="diff_decode",
    )(lam, q, kc, vc, kn, vn, gain)


def _merge_kernel(of_ref, od_ref, gf_ref, gd_ref, wof_ref, wod_ref, o_ref):
    yf = jnp.dot(of_ref[...], wof_ref[...], preferred_element_type=F32)
    yd = jnp.dot(od_ref[...], wod_ref[...], preferred_element_type=F32)
    o_ref[...] = (gf_ref[...].astype(F32) * yf + gd_ref[...].astype(F32) * yd).astype(o_ref.dtype)


def _merge(o_fox, o_diff, gates, w_o_fox, w_o_diff):
    n = o_fox.shape[0]
    d = w_o_fox.shape[1]
    tm = _tile(n, 512)
    row = lambda c: pl.BlockSpec((tm, c), lambda i: (i, 0))
    full = lambda a: pl.BlockSpec(a.shape, lambda i: (0, 0))
    return pl.pallas_call(
        _merge_kernel,
        out_shape=jax.ShapeDtypeStruct((n, d), BF16),
        grid=(n // tm,),
        in_specs=[row(o_fox.shape[1]), row(o_diff.shape[1]),
                  pl.BlockSpec((tm, d), lambda i: (i, 0)), pl.BlockSpec((tm, d), lambda i: (i, 1)),
                  full(w_o_fox), full(w_o_diff)],
        out_specs=row(d),
        compiler_params=_cparams(("parallel",)),
        name="merge",
    )(o_fox, o_diff, gates, gates, w_o_fox, w_o_diff)


def _mix_kernel(m_ref, x_ref, wout_ref, nrm_ref, wr_ref, br_ref, h_ref, xn_ref, ridx_ref, rw_ref, *, n_experts):
    h = x_ref[...] + jnp.dot(m_ref[...], wout_ref[...], preferred_element_type=F32)
    h_ref[...] = h
    xf = h * lax.rsqrt(jnp.mean(h * h, axis=-1, keepdims=True) + NORM_EPS) * nrm_ref[...]
    xn_ref[...] = _pack_halves(xf)
    xh = xf.astype(BF16)
    xl = (xf - xh.astype(F32)).astype(BF16)
    wr = wr_ref[...]
    both = jnp.dot(xh, wr, preferred_element_type=F32)
    logits = (both[:, :LANES] + both[:, LANES:] + jnp.dot(xl, wr[:, :LANES], preferred_element_type=F32)
              + br_ref[...])
    lane = lax.broadcasted_iota(jnp.int32, logits.shape, 1).astype(F32)
    cur = jnp.where(lane < n_experts, logits, -jnp.inf)
    vals, idxs = [], []
    for _ in range(TOP_K):
        mk = jnp.max(cur, axis=-1, keepdims=True)
        ik = jnp.min(jnp.where(cur == mk, lane, float(LANES)), axis=-1, keepdims=True)
        cur = jnp.where(lane == ik, -jnp.inf, cur)
        vals.append(mk)
        idxs.append(ik)
    es = [jnp.exp(vk - vals[0]) for vk in vals]
    den = es[0] + es[1] + es[2] + es[3]
    ridx = jnp.zeros(logits.shape, F32)
    rw = jnp.zeros(logits.shape, F32)
    for k in range(TOP_K):
        ridx = jnp.where(lane == float(k), idxs[k], ridx)
        rw = jnp.where(lane == float(k), es[k] / den, rw)
    ridx_ref[...] = ridx.astype(jnp.int32)
    rw_ref[...] = rw


def _mix(merged, x, w_out, norm_ffn, w_router, b_router):
    n, d = x.shape
    e = w_router.shape[1]
    tm = _tile(n, 512)
    w_pad = _pad_cols(w_router, LANES)
    w_hi = _top_bits(w_pad)
    wr = jnp.concatenate([w_hi, _top_bits(w_pad - w_hi)], axis=1).astype(BF16)
    br = _pad_cols(b_router.reshape(1, e), LANES)
    row = lambda c: pl.BlockSpec((tm, c), lambda i: (i, 0))
    full = lambda a: pl.BlockSpec(a.shape, lambda i: (0, 0), pipeline_mode=pl.Buffered(1))
    args = [merged, x, w_out, norm_ffn.reshape(1, d), wr, br]
    in_specs = [row(d), row(d), full(w_out), full(args[3]), full(wr), full(br)]
    return pl.pallas_call(
        functools.partial(_mix_kernel, n_experts=e),
        out_shape=[jax.ShapeDtypeStruct((n, d), F32), jax.ShapeDtypeStruct((n, d // 2), jnp.uint32),
                   jax.ShapeDtypeStruct((n, LANES), jnp.int32), jax.ShapeDtypeStruct((n, LANES), F32)],
        grid=(n // tm,),
        in_specs=in_specs,
        out_specs=[row(d), row(d // 2), row(LANES), row(LANES)],
        compiler_params=_cparams(("parallel",)),
        name="mix",
    )(*args)


def _moe_kernel(be_ref, nact_ref, nv_ref, tok_hbm, slot_hbm, x_hbm, wg_ref, bg_ref, wu_ref, bu_ref, wd_ref,
                bd_ref, out_hbm, xbuf, xb, tok_s, slot_s, acc, ostage, gsem, ssem, isem, *, tb, nft, n_real):
    blk = pl.program_id(0)
    ft = pl.program_id(1)
    nact = nact_ref[0]
    active = blk < nact
    cur = lax.rem(blk, 2)
    nxt = 1 - cur
    per_step = tb // nft

    def tok_copy(b, s):
        return pltpu.make_async_copy(tok_hbm.at[pl.ds(b, 1)], tok_s.at[pl.ds(s, 1)], isem.at[0])

    def slot_copy(b, s):
        return pltpu.make_async_copy(slot_hbm.at[pl.ds(b, 1)], slot_s.at[pl.ds(s, 1)], isem.at[1])

    def gather_row(s, r):
        t = tok_s[s, r]
        pltpu.make_async_copy(x_hbm.at[pl.ds(t, 1)], xbuf.at[s, pl.ds(r, 1)], gsem.at[s]).start()

    def scatter_row(s, r):
        d = slot_s[s, r]
        pltpu.make_async_copy(ostage.at[s, pl.ds(r, 1)], out_hbm.at[pl.ds(d, 1)], ssem.at[s]).start(priority=1)

    def wait_gather(s):
        pltpu.make_async_copy(xbuf.at[s], xbuf.at[s], gsem.at[s]).wait()

    def wait_scatter(s):
        pltpu.make_async_copy(ostage.at[s], ostage.at[s], ssem.at[s]).wait()

    def loop_rows(fn, s):
        def body(r, carry):
            fn(s, r)
            return carry
        lax.fori_loop(0, tb, body, 0)

    @pl.when((blk == 0) & (ft == 0))
    def _():
        ostage[1] = jnp.zeros(ostage.shape[1:], ostage.dtype)

        def spare(r, carry):
            slot_s[1, r] = n_real + r
            return carry
        lax.fori_loop(0, tb, spare, 0)
        for b in range(2):
            c = tok_copy(b, b)
            c.start()
            c.wait()
        loop_rows(gather_row, 0)

    @pl.when((ft == 0) & (blk >= 1) & (blk <= nact))
    def _():
        tok_copy(blk + 1, nxt).wait()
        slot_copy(blk - 1, nxt).wait()

    @pl.when((ft == 0) & (blk <= nact))
    def _():
        wait_gather(cur)

    @pl.when((ft == 0) & (blk >= 1) & (blk - 2 < nact))
    def _():
        wait_scatter(cur)

    @pl.when((ft == 0) & active)
    def _():
        tok_copy(blk + 2, cur).start()
        slot_copy(blk, cur).start()
        lo, hi = _unpack_halves(xbuf[cur])
        xb[...] = jnp.concatenate([lo.astype(BF16), hi.astype(BF16)], axis=1)
        acc[...] = jnp.broadcast_to(bd_ref[...], acc.shape)

    @pl.when((ft == 0) & (blk == nact) & (blk >= 1))
    def _():
        loop_rows(scatter_row, nxt)

    def compute(rows):
        base = ft * per_step
        for j in range(per_step):
            gather_row(nxt, base + j)
        for j in range(per_step):
            scatter_row(nxt, base + j)
        x = xb[:rows]
        g = jnp.dot(x, wg_ref[...], preferred_element_type=F32) + bg_ref[...]
        u = jnp.dot(x, wu_ref[...], preferred_element_type=F32) + bu_ref[...]
        g = jnp.minimum(g, SWIGLU_LIMIT)
        u = jnp.clip(u, -SWIGLU_LIMIT, SWIGLU_LIMIT)
        act = (u + 1.0) * (g * jax.nn.sigmoid(SWIGLU_ALPHA * g))
        acc[:rows] += jnp.dot(act.astype(BF16), wd_ref[...], preferred_element_type=F32)

    half_full = nv_ref[blk] <= tb // 2

    @pl.when(active & half_full)
    def _():
        compute(tb // 2)

    @pl.when(active & jnp.logical_not(half_full))
    def _():
        compute(tb)

    @pl.when(active & (ft == nft - 1))
    def _():
        ostage[cur] = _pack_halves(acc[...])


def _moe(block_exp, nact, nvalid, row_tok, row_slot, xn, w_gate, b_gate, w_up, b_up, w_down, b_down, n_real):
    nb, tb = row_tok.shape
    e, d, f = w_gate.shape
    tf = _tile(f, 1024)
    nft = f // tf
    d2 = d // 2
    assert nft >= 2 and tb % nft == 0 and xn.shape[1] == d2

    def ftile(b, j, be, na, nv):
        return jnp.where(b < na[0], j, nft - 1)

    any_spec = pl.BlockSpec(memory_space=pl.ANY)
    in_specs = [any_spec, any_spec, any_spec,
                pl.BlockSpec((None, d, tf), lambda b, j, be, na, nv: (be[b], 0, ftile(b, j, be, na, nv))),
                pl.BlockSpec((None, 1, tf), lambda b, j, be, na, nv: (be[b], 0, ftile(b, j, be, na, nv))),
                pl.BlockSpec((None, d, tf), lambda b, j, be, na, nv: (be[b], 0, ftile(b, j, be, na, nv))),
                pl.BlockSpec((None, 1, tf), lambda b, j, be, na, nv: (be[b], 0, ftile(b, j, be, na, nv))),
                pl.BlockSpec((None, tf, d), lambda b, j, be, na, nv: (be[b], ftile(b, j, be, na, nv), 0)),
                pl.BlockSpec((None, 1, d), lambda b, j, be, na, nv: (be[b], 0, 0))]
    return pl.pallas_call(
        functools.partial(_moe_kernel, tb=tb, nft=nft, n_real=n_real),
        out_shape=jax.ShapeDtypeStruct((n_real + tb, d2), jnp.uint32),
        grid_spec=pltpu.PrefetchScalarGridSpec(
            num_scalar_prefetch=3,
            grid=(nb, nft),
            in_specs=in_specs,
            out_specs=any_spec,
            scratch_shapes=[pltpu.VMEM((2, tb, d2), jnp.uint32), pltpu.VMEM((tb, d), BF16),
                            pltpu.SMEM((2, tb), jnp.int32), pltpu.SMEM((2, tb), jnp.int32),
                            pltpu.VMEM((tb, d), F32), pltpu.VMEM((2, tb, d2), jnp.uint32),
                            pltpu.SemaphoreType.DMA((2,)), pltpu.SemaphoreType.DMA((2,)),
                            pltpu.SemaphoreType.DMA((2,))]),
        compiler_params=_cparams(("arbitrary", "arbitrary")),
        name="moe",
    )(block_exp, nact, nvalid, row_tok, row_slot, xn, w_gate, b_gate.reshape(e, 1, f), w_up,
      b_up.reshape(e, 1, f), w_down, b_down.reshape(e, 1, d))


def _routing(top_idx, n_experts, tb):
    n = top_idx.shape[0]
    a = n * TOP_K
    expert = top_idx.reshape(-1)
    bits = max(1, (a - 1).bit_length())
    assert n_experts << bits < 2 ** 31
    idx = jnp.arange(a, dtype=jnp.int32)
    order = jnp.sort((expert.astype(jnp.int32) << bits) | idx) & ((1 << bits) - 1)
    counts = jnp.sum(expert[:, None] == jnp.arange(n_experts, dtype=jnp.int32)[None, :], axis=0, dtype=jnp.int32)
    padded = (counts + tb - 1) // tb * tb
    start = jnp.cumsum(counts) - counts
    pend = jnp.cumsum(padded)
    pstart = pend - padded
    nb = -(-a // tb) + n_experts + 1
    row0 = jnp.arange(nb, dtype=jnp.int32) * tb
    block_exp = jnp.minimum(jnp.sum(pend[None, :] <= row0[:, None], axis=1), n_experts - 1).astype(jnp.int32)
    nvalid = jnp.clip(pstart[block_exp] + counts[block_exp] - row0, 0, tb)
    src0 = jnp.clip(start[block_exp] + row0 - pstart[block_exp], 0, a)
    order_pad = jnp.concatenate([order, jnp.zeros((tb,), jnp.int32)])
    r = jnp.arange(tb, dtype=jnp.int32)[None, :]
    win = order_pad[src0[:, None] + r]
    valid = r < nvalid[:, None]
    tok = win // TOP_K
    row_tok = jnp.where(valid, tok, 0)
    row_slot = jnp.where(valid, (win % TOP_K) * n + tok, a + r)
    nact = (pend[-1] // tb).reshape(1).astype(jnp.int32)
    return block_exp, nact, nvalid.astype(jnp.int32), row_tok.astype(jnp.int32), row_slot.astype(jnp.int32)


def _ple_kernel(h_ref, s0_ref, s1_ref, s2_ref, s3_ref, rw_ref, pe_ref, nple_ref, wg_ref, wp_ref, nfin_ref,
                o_ref, *, final):
    h = h_ref[...]
    rw = rw_ref[...]
    for k, s_ref in enumerate((s0_ref, s1_ref, s2_ref, s3_ref)):
        h = h + rw[:, k:k + 1] * jnp.concatenate(_unpack_halves(s_ref[...]), axis=1)
    xn = h * lax.rsqrt(jnp.mean(h * h, axis=-1, keepdims=True) + NORM_EPS) * nple_ref[...]
    gate = jax.nn.sigmoid(jnp.dot(xn.astype(BF16), wg_ref[...], preferred_element_type=F32))
    h = h + gate * jnp.dot(pe_ref[...].astype(BF16), wp_ref[...], preferred_element_type=F32)
    if final:
        h = h * lax.rsqrt(jnp.mean(h * h, axis=-1, keepdims=True) + NORM_EPS) * nfin_ref[...]
    o_ref[...] = h


def _ple(h, slots, n_all, tok0, rw, pe, norm_ple, w_ple_gate, w_ple_proj, norm_final, final):
    n, d = h.shape
    tm = _tile(n, 256)
    assert TOP_K == 4 and n_all % tm == 0 and tok0 % tm == 0
    row = lambda c: pl.BlockSpec((tm, c), lambda i: (i, 0))
    full = lambda a: pl.BlockSpec(a.shape, lambda i: (0, 0))
    slot = lambda k: pl.BlockSpec((tm, d // 2), lambda i, t0=(k * n_all + tok0) // tm: (t0 + i, 0))
    g1 = norm_ple.reshape(1, d)
    g2 = norm_final.reshape(1, d)
    return pl.pallas_call(
        functools.partial(_ple_kernel, final=final),
        out_shape=jax.ShapeDtypeStruct((n, d), F32),
        grid=(n // tm,),
        in_specs=[row(d), slot(0), slot(1), slot(2), slot(3), row(LANES), row(pe.shape[1]),
                  full(g1), full(w_ple_gate), full(w_ple_proj), full(g2)],
        out_specs=row(d),
        compiler_params=_cparams(("parallel",)),
        name="ple",
    )(h, slots, slots, slots, slots, rw, pe, g1, w_ple_gate, w_ple_proj, g2)


def _rope_tables(pos):
    half = ROPE_DIM // 2
    inv_freq = ROPE_THETA ** (-2.0 * jnp.arange(half, dtype=F32) / ROPE_DIM)
    ang = pos.astype(F32)[:, None] * inv_freq
    cos, sin = jnp.cos(ang), jnp.sin(ang)
    t = pos.shape[0]
    one = jnp.ones((t, LANES - ROPE_DIM), F32)
    zero = jnp.zeros((t, LANES - ROPE_DIM), F32)
    zh = jnp.zeros((t, half), F32)
    c = jnp.concatenate([cos, cos, one], axis=1)
    s1 = jnp.concatenate([zh, sin, zero], axis=1)
    s2 = jnp.concatenate([-sin, zh, zero], axis=1)
    return c, s1, s2


def _pad_cols(w, mult):
    c = w.shape[1]
    cp = -(-c // mult) * mult
    return jnp.pad(w, ((0, 0), (0, cp - c)))


def _layer_weights(w_in, b_forget, d):
    h_f = d // (2 * HEAD_DIM)
    fw = h_f * HEAD_DIM
    dw = fw
    o = 0
    seg = {}
    for name, width in (("fq", fw), ("fk", fw), ("fv", fw), ("fl", h_f), ("dq", dw), ("dk", dw), ("dv", dw),
                        ("gates", 2 * d)):
        seg[name] = w_in[:, o:o + width]
        o += width
    assert o == w_in.shape[1]
    out = {k: v.astype(BF16) for k, v in seg.items() if k != "fl"}
    out["fl"] = _pad_cols(seg["fl"], LANES).astype(BF16)
    out["fl_bias"] = _pad_cols(b_forget.reshape(1, h_f), LANES)
    return out, h_f


def _project(x, norm_mix, pw, tables, h_f):
    qscale = LOG2E * HEAD_DIM ** -0.5
    xn, fq = _proj(x, pw["fq"], "norm_scale_bf", scale=qscale, bias=norm_mix.reshape(1, -1))
    fk, fk_b = _proj(xn, pw["fk"], "f32_bf", head_dim=HEAD_DIM)
    fv, fv_b = _proj(xn, pw["fv"], "f32_bf", head_dim=HEAD_DIM)
    logf, = _proj(xn, pw["fl"], "logsig", bias=pw["fl_bias"], out_cols=h_f)
    dq, = _proj(xn, pw["dq"], "rope_bf", scale=qscale, tables=tables)
    dk, dk_b = _proj(xn, pw["dk"], "rope_f32_bf", tables=tables, head_dim=2 * HEAD_DIM)
    dv, dv_b = _proj(xn, pw["dv"], "f32_bf", head_dim=2 * HEAD_DIM)
    gates, = _proj(xn, pw["gates"], "sigmoid_bf")
    return dict(fq=fq, fk=fk, fk_b=fk_b, fv=fv, fv_b=fv_b, logf=logf, dq=dq, dk=dk, dk_b=dk_b,
                dv=dv, dv_b=dv_b, gates=gates)


def _cum_logf(logf_bth):
    b, s, h = logf_bth.shape
    spad = -(-s // 1024) * 1024
    x = jnp.pad(jnp.swapaxes(logf_bth, 1, 2), ((0, 0), (0, 0), (0, spad - s)))
    return _cumsum(x.reshape(b * h, spad)).reshape(b, h, 1, spad)


def kernel(x_prompt, x_sample, cache_fox_k, cache_fox_v, cache_fox_logf, cache_diff_k, cache_diff_v,
           p_prompt, p_sample, norm_mix, w_in, b_forget, lambda_q1, lambda_k1, lambda_q2, lambda_k2,
           diff_subln, w_o_fox, w_o_diff, w_out, norm_ffn, w_router, b_router, w_gate, b_gate,
           w_up, b_up, w_down, b_down, norm_ple, w_ple_gate, w_ple_proj, norm_final):
    depth = w_in.shape[0]
    bp, tp, d = x_prompt.shape
    bs, ts, _ = x_sample.shape
    past = cache_fox_k.shape[2]
    n_p, n_s = bp * tp, bs * ts
    n_experts = w_router.shape[-1]
    tb = 512

    h_p = x_prompt.reshape(n_p, d)
    h_s = x_sample.reshape(n_s, d)
    tab_p = _rope_tables(jnp.arange(tp, dtype=jnp.int32))
    tab_s = tuple(jnp.tile(t, (bs, 1)) for t in _rope_tables(past + jnp.arange(ts, dtype=jnp.int32)))
    st_p, st_s = [], []
    for i in range(depth):
        lam_init = 0.8 - 0.6 * math.exp(-0.3 * i)
        lam = (jnp.exp(jnp.sum(lambda_q1[i].astype(F32) * lambda_k1[i].astype(F32)))
               - jnp.exp(jnp.sum(lambda_q2[i].astype(F32) * lambda_k2[i].astype(F32)))
               + lam_init).reshape(1).astype(F32)
        out_scale = 1.0 - lam_init
        pw, h_f = _layer_weights(w_in[i], b_forget[i], d)
        h_d = h_f // 2
        subln = diff_subln[i].reshape(1, 2 * HEAD_DIM)
        wof, wod, wo = w_o_fox[i].astype(BF16), w_o_diff[i].astype(BF16), w_out[i].astype(BF16)
        wg, wu, wd = w_gate[i].astype(BF16), w_up[i].astype(BF16), w_down[i].astype(BF16)
        wpg, wpp = w_ple_gate[i].astype(BF16), w_ple_proj[i].astype(BF16)

        pr = _project(h_p, norm_mix[i], pw, tab_p, h_f)
        c_p = _cum_logf(pr["logf"].reshape(bp, tp, h_f))
        o_fox_p = _fox_prompt(pr["fq"], pr["fk_b"], pr["fv_b"], c_p, bp, tp)
        o_diff_p = _diff_prompt(lam, pr["dq"], pr["dk_b"], pr["dv_b"], subln, bp, tp, out_scale)
        h1_p, xn_p, ridx_p, rw_p = _mix(_merge(o_fox_p, o_diff_p, pr["gates"], wof, wod), h_p, wo,
                                        norm_ffn[i], w_router[i], b_router[i])

        sr = _project(h_s, norm_mix[i], pw, tab_s, h_f)
        logf_all = jnp.concatenate([cache_fox_logf[i].astype(F32), sr["logf"].reshape(bs, ts, h_f)], axis=1)
        c_s = _cum_logf(logf_all)
        o_fox_s = _fox_decode(sr["fq"], cache_fox_k[i].reshape(bs * past, -1), cache_fox_v[i].reshape(bs * past, -1),
                              sr["fk_b"], sr["fv_b"], c_s, bs, ts, past)
        o_diff_s = _diff_decode(lam, sr["dq"], cache_diff_k[i].reshape(bs * past, -1),
                                cache_diff_v[i].reshape(bs * past, -1), sr["dk_b"], sr["dv_b"], subln,
                                bs, ts, past, out_scale)
        h1_s, xn_s, ridx_s, rw_s = _mix(_merge(o_fox_s, o_diff_s, sr["gates"], wof, wod), h_s, wo,
                                        norm_ffn[i], w_router[i], b_router[i])

        xn_all = jnp.concatenate([xn_p, xn_s], axis=0)
        top_idx = jnp.concatenate([ridx_p[:, :TOP_K], ridx_s[:, :TOP_K]], axis=0)
        block_exp, nact, nvalid, row_tok, row_slot = _routing(top_idx, n_experts, tb)
        n_all = n_p + n_s
        slots = _moe(block_exp, nact, nvalid, row_tok, row_slot, xn_all, wg, b_gate[i], wu, b_up[i], wd,
                     b_down[i], n_all * TOP_K)

        last = i == depth - 1
        h_p = _ple(h1_p, slots, n_all, 0, rw_p, p_prompt[i].reshape(n_p, -1), norm_ple[i], wpg, wpp,
                   norm_final, last)
        h_s = _ple(h1_s, slots, n_all, n_p, rw_s, p_sample[i].reshape(n_s, -1), norm_ple[i], wpg, wpp,
                   norm_final, last)

        st_p.append((pr["fk"].reshape(bp, tp, h_f, HEAD_DIM), pr["fv"].reshape(bp, tp, h_f, HEAD_DIM),
                     pr["logf"].reshape(bp, tp, h_f), pr["dk"].reshape(bp, tp, h_d, 2 * HEAD_DIM),
                     pr["dv"].reshape(bp, tp, h_d, 2 * HEAD_DIM)))
        st_s.append((sr["fk"].reshape(bs, ts, h_f, HEAD_DIM), sr["fv"].reshape(bs, ts, h_f, HEAD_DIM),
                     sr["logf"].reshape(bs, ts, h_f), sr["dk"].reshape(bs, ts, h_d, 2 * HEAD_DIM),
                     sr["dv"].reshape(bs, ts, h_d, 2 * HEAD_DIM)))

    y_prompt = h_p.reshape(bp, tp, d)
    y_sample = h_s.reshape(bs, ts, d)
    outs_p = [jnp.stack([s[j] for s in st_p]) for j in range(5)]
    outs_s = [jnp.stack([s[j] for s in st_s]) for j in range(5)]
    return (y_prompt, y_sample, *outs_p, *outs_s)
```
